```python
import jax, jax.numpy as jnp
from jax import lax
import numpy as np

D_MODEL = 4096
BATCH = 4
SEQ = 2048
DEPTH = 2
DEC_BATCH = 128
DEC_SEQ = 4
PAST_LEN = 16384
PAGE_SIZE = 128

MIX_W = D_MODEL
W_A = MIX_W // 2
W_B = MIX_W - W_A
LRU_HEAD = 256
N_LRU_HEADS = W_A // LRU_HEAD
LRU_C = 8.0
CONV_W = 4
POOL_WINDOWS = (2, 4, 8, 16)
N_POOL_GROUPS = len(POOL_WINDOWS)
POOL_GC = W_B // N_POOL_GROUPS
POOL_PAD = max(POOL_WINDOWS) - 1
D_FF = ((8 * D_MODEL // 3 + 255) // 256) * 256
PLE_DIM = 256
EPS = 1e-6

kernel_name = "hybrid_rglru_pool_decoder_step"


def rmsnorm(x, g):
    xf = x.astype(jnp.float32)
    r = xf * lax.rsqrt(jnp.mean(xf * xf, axis=-1, keepdims=True) + EPS)
    return (r * g.astype(jnp.float32)).astype(x.dtype)


def _lin_combine(e1, e2):
    a1, b1 = e1
    a2, b2 = e2
    return a1 * a2, a2 * b1 + b2


def rg_lru(xc, w_r, b_r, w_i, b_i, lam, pos, h0):
    B, T, _ = xc.shape
    xf = xc.astype(jnp.float32)
    xh = xf.reshape(B, T, N_LRU_HEADS, LRU_HEAD)
    r = jax.nn.sigmoid(jnp.einsum('bthi,hij->bthj', xh, w_r.astype(jnp.float32)) + b_r.astype(jnp.float32)).reshape(B, T, W_A)
    i = jax.nn.sigmoid(jnp.einsum('bthi,hij->bthj', xh, w_i.astype(jnp.float32)) + b_i.astype(jnp.float32)).reshape(B, T, W_A)
    log_a = LRU_C * r * jax.nn.log_sigmoid(lam.astype(jnp.float32))
    a = jnp.exp(log_a)
    mult = jnp.sqrt(-jnp.expm1(2.0 * log_a))
    reset = (pos == 0)[None, :, None]
    a = jnp.where(reset, 0.0, a)
    mult = jnp.where(reset, 1.0, mult)
    b = mult * (i * xf)
    b = b.at[:, 0].add(a[:, 0] * h0.astype(jnp.float32))
    _, h = lax.associative_scan(_lin_combine, (a, b), axis=1)
    return h, h[:, -1]


def pool_mixer(u, buf, pos, w_pool, scale):
    B, T, _ = u.shape
    ext = jnp.concatenate([buf.astype(u.dtype), u], axis=1).astype(jnp.float32)
    cs = jnp.concatenate([jnp.zeros((B, 1, W_B), jnp.float32), jnp.cumsum(ext, axis=1)], axis=1)
    end = cs[:, POOL_PAD + 1:]
    cur = ext[:, POOL_PAD:]
    outs = []
    for g, w in enumerate(POOL_WINDOWS):
        sl = slice(g * POOL_GC, (g + 1) * POOL_GC)
        start = cs[:, POOL_PAD + 1 - w: POOL_PAD + 1 - w + T, sl]
        cnt = jnp.minimum(pos + 1, w).astype(jnp.float32)[None, :, None]
        outs.append((end[..., sl] - start) / cnt - cur[..., sl])
    d = jnp.stack(outs, axis=2)
    y = jnp.einsum('btgc,gcd->btgd', d, w_pool.astype(jnp.float32)).reshape(B, T, W_B)
    return (y * scale.astype(jnp.float32)).astype(u.dtype)


def layer(x, p, h0, conv_buf, pool_buf, pos, g_mix, w_in, conv_w, conv_b, w_rg, b_rg, w_ig, b_ig,
          lam, w_pool, pool_scale, w_out, g_ffn, w_gate, w_up, w_down, g_pe, w_pe_gate, w_pe_proj):
    T = x.shape[1]
    n = rmsnorm(x, g_mix)
    z = n @ w_in
    xa, ga, ub = z[..., :W_A], z[..., W_A:2 * W_A], z[..., 2 * W_A:]
    ext = jnp.concatenate([conv_buf.astype(xa.dtype), xa], axis=1)
    xc = conv_b + sum(conv_w[k] * ext[:, k:k + T] for k in range(CONV_W))
    h, h_last = rg_lru(xc, w_rg, b_rg, w_ig, b_ig, lam, pos, h0)
    ya = (jax.nn.gelu(ga.astype(jnp.float32)) * h).astype(x.dtype)
    yb = pool_mixer(ub, pool_buf, pos, w_pool, pool_scale)
    x = x + jnp.concatenate([ya, yb], axis=-1) @ w_out
    n2 = rmsnorm(x, g_ffn)
    x = x + (jax.nn.silu(n2 @ w_gate) * (n2 @ w_up)) @ w_down
    n3 = rmsnorm(x, g_pe)
    x = x + jax.nn.sigmoid(n3 @ w_pe_gate) * (p.astype(x.dtype) @ w_pe_proj)
    new_conv = ext[:, -(CONV_W - 1):]
    new_pool = jnp.concatenate([pool_buf.astype(ub.dtype), ub], axis=1)[:, -POOL_PAD:]
    return x, h_last, new_conv, new_pool


def setup_inputs(seed: int = 0) -> dict:
    key = jax.random.key(seed)
    ks = jax.random.split(key, 32)
    f32 = jnp.float32
    nrm = lambda k, shape, s: jax.random.normal(k, shape, f32) * s
    u = jax.random.uniform(ks[10], (DEPTH, W_A), f32, 0.9, 0.999)
    a0 = u ** (1.0 / LRU_C)
    lam = jnp.log(a0) - jnp.log1p(-a0)
    return {
        "x_prompt": nrm(ks[0], (BATCH, SEQ, D_MODEL), 1.0),
        "x_sample": nrm(ks[1], (DEC_BATCH, DEC_SEQ, D_MODEL), 1.0),
        "state_h": nrm(ks[2], (DEPTH, DEC_BATCH, W_A), 0.5),
        "state_conv": nrm(ks[3], (DEPTH, DEC_BATCH, CONV_W - 1, W_A), 1.0),
        "state_pool": nrm(ks[4], (DEPTH, DEC_BATCH, POOL_PAD, W_B), 1.0),
        "p_prompt": nrm(ks[5], (DEPTH, BATCH, SEQ, PLE_DIM), 1.0),
        "p_sample": nrm(ks[6], (DEPTH, DEC_BATCH, DEC_SEQ, PLE_DIM), 1.0),
        "g_mix": 1.0 + nrm(ks[7], (DEPTH, D_MODEL), 0.05),
        "w_in": nrm(ks[8], (DEPTH, D_MODEL, 2 * W_A + W_B), D_MODEL ** -0.5),
        "conv_w": nrm(ks[9], (DEPTH, CONV_W, W_A), CONV_W ** -0.5),
        "conv_b": nrm(ks[11], (DEPTH, W_A), 0.01),
        "w_rg": nrm(ks[12], (DEPTH, N_LRU_HEADS, LRU_HEAD, LRU_HEAD), LRU_HEAD ** -0.5),
        "b_rg": nrm(ks[13], (DEPTH, N_LRU_HEADS, LRU_HEAD), 0.01),
        "w_ig": nrm(ks[14], (DEPTH, N_LRU_HEADS, LRU_HEAD, LRU_HEAD), LRU_HEAD ** -0.5),
        "b_ig": nrm(ks[15], (DEPTH, N_LRU_HEADS, LRU_HEAD), 0.01),
        "lam": lam,
        "w_pool": nrm(ks[16], (DEPTH, N_POOL_GROUPS, POOL_GC, POOL_GC), POOL_GC ** -0.5),
        "pool_scale": 1.0 + nrm(ks[17], (DEPTH, W_B), 0.1),
        "w_out": nrm(ks[18], (DEPTH, MIX_W, D_MODEL), MIX_W ** -0.5),
        "g_ffn": 1.0 + nrm(ks[19], (DEPTH, D_MODEL), 0.05),
        "w_gate": nrm(ks[20], (DEPTH, D_MODEL, D_FF), D_MODEL ** -0.5),
        "w_up": nrm(ks[21], (DEPTH, D_MODEL, D_FF), D_MODEL ** -0.5),
        "w_down": nrm(ks[22], (DEPTH, D_FF, D_MODEL), D_FF ** -0.5),
        "g_pe": 1.0 + nrm(ks[23], (DEPTH, D_MODEL), 0.05),
        "w_pe_gate": nrm(ks[24], (DEPTH, D_MODEL, D_MODEL), D_MODEL ** -0.5),
        "w_pe_proj": nrm(ks[25], (DEPTH, PLE_DIM, D_MODEL), PLE_DIM ** -0.5),
        "g_final": 1.0 + nrm(ks[26], (D_MODEL,), 0.05),
    }


def reference(x_prompt, x_sample, state_h, state_conv, state_pool, p_prompt, p_sample,
              g_mix, w_in, conv_w, conv_b, w_rg, b_rg, w_ig, b_ig, lam, w_pool, pool_scale,
              w_out, g_ffn, w_gate, w_up, w_down, g_pe, w_pe_gate, w_pe_proj, g_final):
    pos_p = jnp.arange(SEQ, dtype=jnp.int32)
    pos_s = PAST_LEN + jnp.arange(DEC_SEQ, dtype=jnp.int32)
    xp, xs = x_prompt, x_sample
    hp_l, cp_l, pp_l, hs_l, cs_l, ps_l = [], [], [], [], [], []
    for i in range(DEPTH):
        w = (g_mix[i], w_in[i], conv_w[i], conv_b[i], w_rg[i], b_rg[i], w_ig[i], b_ig[i], lam[i],
             w_pool[i], pool_scale[i], w_out[i], g_ffn[i], w_gate[i], w_up[i], w_down[i],
             g_pe[i], w_pe_gate[i], w_pe_proj[i])
        h0 = jnp.zeros((BATCH, W_A), jnp.float32)
        cb0 = jnp.zeros((BATCH, CONV_W - 1, W_A), xp.dtype)
        pb0 = jnp.zeros((BATCH, POOL_PAD, W_B), xp.dtype)
        xp, hp, cp, pp = layer(xp, p_prompt[i], h0, cb0, pb0, pos_p, *w)
        xs, hs, cs, ps = layer(xs, p_sample[i], state_h[i], state_conv[i], state_pool[i], pos_s, *w)
        hp_l.append(hp); cp_l.append(cp); pp_l.append(pp)
        hs_l.append(hs); cs_l.append(cs); ps_l.append(ps)
    y_prompt = rmsnorm(xp, g_final)
    y_sample = rmsnorm(xs, g_final)
    return (y_prompt, y_sample,
            jnp.stack(hp_l), jnp.stack(cp_l), jnp.stack(pp_l),
            jnp.stack(hs_l), jnp.stack(cs_l), jnp.stack(ps_l))
```

```python
import functools

import jax
import jax.numpy as jnp
from jax import lax
from jax.experimental import pallas as pl
from jax.experimental.pallas import tpu as pltpu

D_MODEL = 4096
BATCH = 4
SEQ = 2048
DEPTH = 2
DEC_BATCH = 128
DEC_SEQ = 4
PAST_LEN = 16384
W_A = D_MODEL // 2
W_B = D_MODEL - W_A
LRU_HEAD = 256
N_LRU_HEADS = W_A // LRU_HEAD
LRU_C = 8.0
CONV_W = 4
POOL_WINDOWS = (2, 4, 8, 16)
N_POOL_GROUPS = len(POOL_WINDOWS)
POOL_GC = W_B // N_POOL_GROUPS
POOL_PAD = max(POOL_WINDOWS) - 1
D_FF = 11008
PLE_DIM = 256
EPS = 1e-6

N_PROMPT = BATCH * SEQ
N_SAMPLE = DEC_BATCH * DEC_SEQ
N_TOK = N_PROMPT + N_SAMPLE
Z_W = 2 * W_A + W_B

F32 = jnp.float32
BF16 = jnp.bfloat16

VMEM_LIMIT_BYTES = 60 * 1024 * 1024

TM = 512
TM_OUT = 1088
TN_IN = 1024
TN_OUT = 512
TF = 256
TN_FFN = 256
TN_PE = 512
TB = 256
CONV_CARRY = 8
POOL_CARRY = 16
SAMPLE_CB = 512


def _params(*sem):
    return pltpu.CompilerParams(dimension_semantics=sem, vmem_limit_bytes=VMEM_LIMIT_BYTES)


def _rms_scale(x, g):
    ms = jnp.mean(x * x, axis=-1, keepdims=True)
    return x * lax.rsqrt(ms + EPS) * g


def _log_sigmoid(x):
    return jnp.minimum(x, 0.0) - jnp.log1p(jnp.exp(-jnp.abs(x)))


def _lru_coeffs(xc, r, i, logsig):
    log_a = LRU_C * r * logsig
    a = jnp.exp(log_a)
    mult = jnp.sqrt(1.0 - a * a)
    return a, mult, i * xc


def _norm_matmul_kernel(x_ref, g_ref, w_ref, o_ref, xn_ref):
    @pl.when(pl.program_id(1) == 0)
    def _():
        xn_ref[...] = _rms_scale(x_ref[...], g_ref[...]).astype(BF16)

    o_ref[...] = jnp.dot(xn_ref[...], w_ref[...], preferred_element_type=F32)


def _norm_matmul(x, g, w, tm, tn):
    m, k = x.shape
    n = w.shape[1]
    return pl.pallas_call(
        _norm_matmul_kernel,
        grid=(m // tm, n // tn),
        in_specs=[
            pl.BlockSpec((tm, k), lambda i, j: (i, 0)),
            pl.BlockSpec((1, k), lambda i, j: (0, 0)),
            pl.BlockSpec((k, tn), lambda i, j: (0, j)),
        ],
        out_specs=pl.BlockSpec((tm, tn), lambda i, j: (i, j)),
        out_shape=jax.ShapeDtypeStruct((m, n), F32),
        scratch_shapes=[pltpu.VMEM((tm, k), BF16)],
        compiler_params=_params("parallel", "arbitrary"),
        name="in_proj",
    )(x, g, w)


def _prompt_mixer_kernel(xa_ref, ga_ref, ub_ref, cw_ref, cb_ref, wr_ref, br_ref, wi_ref, bi_ref,
                         lam_ref, wp_ref, ps_ref,
                         y_ref, hl_ref, ct_ref, pt_ref,
                         hc_ref, cc_ref, pc_ref, a_ref, b_ref):
    t = pl.program_id(1)

    @pl.when(t == 0)
    def _():
        hc_ref[...] = jnp.zeros_like(hc_ref)
        cc_ref[...] = jnp.zeros_like(cc_ref)
        pc_ref[...] = jnp.zeros_like(pc_ref)

    pos = lax.broadcasted_iota(jnp.int32, (TB, 1), 0) + t * TB

    xa = xa_ref[...]
    ext = jnp.concatenate([cc_ref[...], xa], axis=0)
    xc = cb_ref[...] + cw_ref[CONV_W - 1:CONV_W, :] * xa
    for s in range(1, CONV_W):
        xc = xc + cw_ref[CONV_W - 1 - s:CONV_W - s, :] * pltpu.roll(ext, s, axis=0)[CONV_CARRY:]
    xcb = xc.astype(BF16)
    r_parts, i_parts = [], []
    for hd in range(N_LRU_HEADS):
        sl = slice(hd * LRU_HEAD, (hd + 1) * LRU_HEAD)
        r_parts.append(jnp.dot(xcb[:, sl], wr_ref[hd], preferred_element_type=F32))
        i_parts.append(jnp.dot(xcb[:, sl], wi_ref[hd], preferred_element_type=F32))
    r = jax.nn.sigmoid(jnp.concatenate(r_parts, axis=1) + br_ref[...])
    i = jax.nn.sigmoid(jnp.concatenate(i_parts, axis=1) + bi_ref[...])
    a, mult, ix = _lru_coeffs(xc, r, i, _log_sigmoid(lam_ref[...]))
    start = pos == 0
    a_ref[...] = jnp.where(start, 0.0, a)
    b_ref[...] = jnp.where(start, 1.0, mult) * ix

    def step(row, h):
        h = a_ref[pl.ds(row, 1), :] * h + b_ref[pl.ds(row, 1), :]
        b_ref[pl.ds(row, 1), :] = h
        return h

    h_last = lax.fori_loop(0, TB, step, hc_ref[0:1, :], unroll=8)
    hc_ref[0:1, :] = h_last
    hl_ref[0] = h_last
    y_ref[:, 0:W_A] = (jax.nn.gelu(ga_ref[...]) * b_ref[...]).astype(BF16)
    cc_ref[...] = xa[TB - CONV_CARRY:]
    ct_ref[0] = xa[TB - CONV_CARRY:]

    ub = ub_ref[...]
    pext = jnp.concatenate([pc_ref[...], ub], axis=0)
    for g, w in enumerate(POOL_WINDOWS):
        sl = slice(g * POOL_GC, (g + 1) * POOL_GC)
        s = pext[:, sl]
        shift = 1
        while shift < w:
            s = s + pltpu.roll(s, shift, axis=0)
            shift *= 2
        cnt = jnp.minimum(pos + 1, w).astype(F32)
        d = s[POOL_CARRY:] / cnt - ub[:, sl]
        yb = jnp.dot(d.astype(BF16), wp_ref[g], preferred_element_type=F32) * ps_ref[:, sl]
        y_ref[:, W_A + g * POOL_GC:W_A + (g + 1) * POOL_GC] = yb.astype(BF16)
    pc_ref[...] = ub[TB - POOL_CARRY:]
    pt_ref[0] = ub[TB - POOL_CARRY:]


def _prompt_mixer(z, cw, cb, wr, br, wi, bi, lam, wp, ps):
    nt = SEQ // TB
    row = lambda b, t: b * nt + t
    full2 = lambda shape: pl.BlockSpec(shape, lambda b, t: (0, 0))
    full3 = lambda shape: pl.BlockSpec(shape, lambda b, t: (0, 0, 0))
    return pl.pallas_call(
        _prompt_mixer_kernel,
        grid=(BATCH, nt),
        in_specs=[
            pl.BlockSpec((TB, W_A), lambda b, t: (row(b, t), 0)),
            pl.BlockSpec((TB, W_A), lambda b, t: (row(b, t), 1)),
            pl.BlockSpec((TB, W_B), lambda b, t: (row(b, t), 2)),
            full2((CONV_W, W_A)), full2((1, W_A)),
            full3((N_LRU_HEADS, LRU_HEAD, LRU_HEAD)), full2((1, W_A)),
            full3((N_LRU_HEADS, LRU_HEAD, LRU_HEAD)), full2((1, W_A)),
            full2((1, W_A)),
            full3((N_POOL_GROUPS, POOL_GC, POOL_GC)), full2((1, W_B)),
        ],
        out_specs=[
            pl.BlockSpec((TB, D_MODEL), lambda b, t: (row(b, t), 0)),
            pl.BlockSpec((1, 1, W_A), lambda b, t: (b, 0, 0)),
            pl.BlockSpec((1, CONV_CARRY, W_A), lambda b, t: (b, 0, 0)),
            pl.BlockSpec((1, POOL_CARRY, W_B), lambda b, t: (b, 0, 0)),
        ],
        out_shape=[
            jax.ShapeDtypeStruct((N_PROMPT, D_MODEL), BF16),
            jax.ShapeDtypeStruct((BATCH, 1, W_A), F32),
            jax.ShapeDtypeStruct((BATCH, CONV_CARRY, W_A), F32),
            jax.ShapeDtypeStruct((BATCH, POOL_CARRY, W_B), F32),
        ],
        scratch_shapes=[
            pltpu.VMEM((8, W_A), F32),
            pltpu.VMEM((CONV_CARRY, W_A), F32),
            pltpu.VMEM((POOL_CARRY, W_B), F32),
            pltpu.VMEM((TB, W_A), F32),
            pltpu.VMEM((TB, W_A), F32),
        ],
        compiler_params=_params("parallel", "arbitrary"),
        name="prompt_mixer",
    )(z, z, z, cw, cb, wr, br, wi, bi, lam, wp, ps)


def _sample_mixer_kernel(xa_ref, ga_ref, ub_ref, h0_ref, cs_ref, pst_ref,
                         cw_ref, cb_ref, wr_ref, br_ref, wi_ref, bi_ref, lam_ref, wp_ref, ps_ref,
                         ya_ref, yb_ref, hn_ref, cn_ref, pn_ref):
    c = pl.program_id(0)
    heads = SAMPLE_CB // LRU_HEAD

    ext = [cs_ref[k] for k in range(CONV_W - 1)] + [xa_ref[t] for t in range(DEC_SEQ)]
    logsig = _log_sigmoid(lam_ref[...])
    h = h0_ref[...]
    for t in range(DEC_SEQ):
        xc = cb_ref[...] + cw_ref[0:1, :] * ext[t]
        for k in range(1, CONV_W):
            xc = xc + cw_ref[k:k + 1, :] * ext[t + k]
        xcb = xc.astype(BF16)
        r_parts, i_parts = [], []
        for hd in range(heads):
            sl = slice(hd * LRU_HEAD, (hd + 1) * LRU_HEAD)
            r_parts.append(jnp.dot(xcb[:, sl], wr_ref[hd], preferred_element_type=F32))
            i_parts.append(jnp.dot(xcb[:, sl], wi_ref[hd], preferred_element_type=F32))
        r = jax.nn.sigmoid(jnp.concatenate(r_parts, axis=1) + br_ref[...])
        i = jax.nn.sigmoid(jnp.concatenate(i_parts, axis=1) + bi_ref[...])
        a, mult, ix = _lru_coeffs(xc, r, i, logsig)
        if PAST_LEN + t == 0:
            a, mult = jnp.zeros_like(a), jnp.ones_like(mult)
        h = a * h + mult * ix
        ya_ref[t] = (jax.nn.gelu(ga_ref[t]) * h).astype(BF16)
    hn_ref[...] = h
    for k in range(CONV_W - 1):
        cn_ref[k] = ext[DEC_SEQ + k]

    for k in range(POOL_PAD):
        src = k + DEC_SEQ
        pn_ref[k] = pst_ref[src] if src < POOL_PAD else ub_ref[src - POOL_PAD]

    for g, w in enumerate(POOL_WINDOWS):
        @pl.when(c == g)
        def _(w=w):
            pext = [pst_ref[k] for k in range(POOL_PAD - w + 1, POOL_PAD)] + [ub_ref[t] for t in range(DEC_SEQ)]
            for t in range(DEC_SEQ):
                s = pext[t]
                for j in range(1, w):
                    s = s + pext[t + j]
                cnt = float(min(PAST_LEN + t + 1, w))
                d = s / cnt - pext[t + w - 1]
                yb = jnp.dot(d.astype(BF16), wp_ref[0], preferred_element_type=F32) * ps_ref[...]
                yb_ref[t] = yb.astype(BF16)


def _sample_mixer(z_s, h0, cs, pst, cw, cb, wr, br, wi, bi, lam, wp, ps):
    cbw = SAMPLE_CB
    na = W_A // cbw
    heads = cbw // LRU_HEAD
    col = lambda off: (lambda c: (0, 0, off + c))
    vec = pl.BlockSpec((1, cbw), lambda c: (0, c))
    return pl.pallas_call(
        _sample_mixer_kernel,
        grid=(na,),
        in_specs=[
            pl.BlockSpec((DEC_SEQ, DEC_BATCH, cbw), col(0)),
            pl.BlockSpec((DEC_SEQ, DEC_BATCH, cbw), col(na)),
            pl.BlockSpec((DEC_SEQ, DEC_BATCH, cbw), col(2 * na)),
            pl.BlockSpec((DEC_BATCH, cbw), lambda c: (0, c)),
            pl.BlockSpec((CONV_W - 1, DEC_BATCH, cbw), col(0)),
            pl.BlockSpec((POOL_PAD, DEC_BATCH, cbw), col(0)),
            pl.BlockSpec((CONV_W, cbw), lambda c: (0, c)), vec,
            pl.BlockSpec((heads, LRU_HEAD, LRU_HEAD), lambda c: (c, 0, 0)), vec,
            pl.BlockSpec((heads, LRU_HEAD, LRU_HEAD), lambda c: (c, 0, 0)), vec,
            vec,
            pl.BlockSpec((1, POOL_GC, POOL_GC), lambda c: (c, 0, 0)), vec,
        ],
        out_specs=[
            pl.BlockSpec((DEC_SEQ, DEC_BATCH, cbw), col(0)),
            pl.BlockSpec((DEC_SEQ, DEC_BATCH, cbw), col(0)),
            pl.BlockSpec((DEC_BATCH, cbw), lambda c: (0, c)),
            pl.BlockSpec((CONV_W - 1, DEC_BATCH, cbw), col(0)),
            pl.BlockSpec((POOL_PAD, DEC_BATCH, cbw), col(0)),
        ],
        out_shape=[
            jax.ShapeDtypeStruct((DEC_SEQ, DEC_BATCH, W_A), BF16),
            jax.ShapeDtypeStruct((DEC_SEQ, DEC_BATCH, W_B), BF16),
            jax.ShapeDtypeStruct((DEC_BATCH, W_A), F32),
            jax.ShapeDtypeStruct((CONV_W - 1, DEC_BATCH, W_A), F32),
            jax.ShapeDtypeStruct((POOL_PAD, DEC_BATCH, W_B), F32),
        ],
        compiler_params=_params("parallel"),
        name="sample_mixer",
    )(z_s, z_s, z_s, h0, cs, pst, cw, cb, wr, br, wi, bi, lam, wp, ps)


def _matmul_res_kernel(y_ref, w_ref, x_ref, o_ref):
    o_ref[...] = x_ref[...] + jnp.dot(y_ref[...], w_ref[...], preferred_element_type=F32)


def _matmul_res(y, w, x, tm, tn):
    m, k = y.shape
    n = w.shape[1]
    return pl.pallas_call(
        _matmul_res_kernel,
        grid=(m // tm, n // tn),
        in_specs=[
            pl.BlockSpec((tm, k), lambda i, j: (i, 0)),
            pl.BlockSpec((k, tn), lambda i, j: (0, j)),
            pl.BlockSpec((tm, tn), lambda i, j: (i, j)),
        ],
        out_specs=pl.BlockSpec((tm, tn), lambda i, j: (i, j)),
        out_shape=jax.ShapeDtypeStruct((m, n), F32),
        compiler_params=_params("parallel", "arbitrary"),
        name="out_proj",
    )(y, w, x)


def _ffn_kernel(x_ref, g_ref, wg_ref, wu_ref, wd_ref, xres_ref, o_ref, xn_ref, h_ref, *, nf):
    j = pl.program_id(1)

    @pl.when(j == 0)
    def _():
        xn_ref[...] = _rms_scale(x_ref[...], g_ref[...]).astype(BF16)

    @pl.when(j < nf)
    def _():
        xn = xn_ref[...]
        gate = jnp.dot(xn, wg_ref[...], preferred_element_type=F32)
        up = jnp.dot(xn, wu_ref[...], preferred_element_type=F32)
        h_ref[j] = (jax.nn.silu(gate) * up).astype(BF16)

    @pl.when(j >= nf)
    def _():
        acc = xres_ref[...]
        for f in range(nf):
            acc = acc + jnp.dot(h_ref[f], wd_ref[f * TF:(f + 1) * TF, :], preferred_element_type=F32)
        o_ref[...] = acc


def _ffn(x, g, wg, wu, wd):
    m, k = x.shape
    nf = D_FF // TF
    nn = k // TN_FFN
    fidx = lambda j: jnp.minimum(j, nf - 1)
    nidx = lambda j: jnp.maximum(j - nf, 0)
    return pl.pallas_call(
        functools.partial(_ffn_kernel, nf=nf),
        grid=(m // TM, nf + nn),
        in_specs=[
            pl.BlockSpec((TM, k), lambda i, j: (i, 0)),
            pl.BlockSpec((1, k), lambda i, j: (0, 0)),
            pl.BlockSpec((k, TF), lambda i, j: (0, fidx(j))),
            pl.BlockSpec((k, TF), lambda i, j: (0, fidx(j))),
            pl.BlockSpec((D_FF, TN_FFN), lambda i, j: (0, nidx(j))),
            pl.BlockSpec((TM, TN_FFN), lambda i, j: (i, nidx(j))),
        ],
        out_specs=pl.BlockSpec((TM, TN_FFN), lambda i, j: (i, nidx(j))),
        out_shape=jax.ShapeDtypeStruct((m, k), F32),
        scratch_shapes=[pltpu.VMEM((TM, k), BF16), pltpu.VMEM((nf, TM, TF), BF16)],
        compiler_params=_params("parallel", "arbitrary"),
        name="ffn",
    )(x, g, wg, wu, wd, x)


def _ple_kernel(x_ref, g_ref, wg_ref, p_ref, wp_ref, xres_ref, o_ref, xn_ref):
    @pl.when(pl.program_id(1) == 0)
    def _():
        xn_ref[...] = _rms_scale(x_ref[...], g_ref[...]).astype(BF16)

    gate = jax.nn.sigmoid(jnp.dot(xn_ref[...], wg_ref[...], preferred_element_type=F32))
    proj = jnp.dot(p_ref[...], wp_ref[...], preferred_element_type=F32)
    o_ref[...] = xres_ref[...] + gate * proj


def _ple(x, g, wg, p, wp):
    m, k = x.shape
    return pl.pallas_call(
        _ple_kernel,
        grid=(m // TM, k // TN_PE),
        in_specs=[
            pl.BlockSpec((TM, k), lambda i, j: (i, 0)),
            pl.BlockSpec((1, k), lambda i, j: (0, 0)),
            pl.BlockSpec((k, TN_PE), lambda i, j: (0, j)),
            pl.BlockSpec((TM, PLE_DIM), lambda i, j: (i, 0)),
            pl.BlockSpec((PLE_DIM, TN_PE), lambda i, j: (0, j)),
            pl.BlockSpec((TM, TN_PE), lambda i, j: (i, j)),
        ],
        out_specs=pl.BlockSpec((TM, TN_PE), lambda i, j: (i, j)),
        out_shape=jax.ShapeDtypeStruct((m, k), F32),
        scratch_shapes=[pltpu.VMEM((TM, k), BF16)],
        compiler_params=_params("parallel", "arbitrary"),
        name="ple",
    )(x, g, wg, p, wp, x)


def _rmsnorm_kernel(x_ref, g_ref, o_ref):
    o_ref[...] = _rms_scale(x_ref[...], g_ref[...])


def _rmsnorm(x, g):
    m, k = x.shape
    return pl.pallas_call(
        _rmsnorm_kernel,
        grid=(m // TM,),
        in_specs=[pl.BlockSpec((TM, k), lambda i: (i, 0)), pl.BlockSpec((1, k), lambda i: (0, 0))],
        out_specs=pl.BlockSpec((TM, k), lambda i: (i, 0)),
        out_shape=jax.ShapeDtypeStruct((m, k), F32),
        compiler_params=_params("parallel"),
        name="final_norm",
    )(x, g)


def _time_major(a):
    return jnp.swapaxes(a, 0, 1)


def kernel(x_prompt, x_sample, state_h, state_conv, state_pool, p_prompt, p_sample, g_mix, w_in, conv_w, conv_b, w_rg, b_rg, w_ig, b_ig, lam, w_pool, pool_scale, w_out, g_ffn, w_gate, w_up, w_down, g_pe, w_pe_gate, w_pe_proj, g_final):
    row = lambda v: v.reshape(1, -1)
    x = jnp.concatenate([x_prompt.reshape(N_PROMPT, D_MODEL),
                         _time_major(x_sample).reshape(N_SAMPLE, D_MODEL)], axis=0)
    hp_l, cp_l, pp_l, hs_l, cs_l, ps_l = [], [], [], [], [], []
    for l in range(DEPTH):
        cw, cb = conv_w[l], row(conv_b[l])
        wr, br = w_rg[l].astype(BF16), row(b_rg[l])
        wi, bi = w_ig[l].astype(BF16), row(b_ig[l])
        lm, wp, ps = row(lam[l]), w_pool[l].astype(BF16), row(pool_scale[l])

        z = _norm_matmul(x, row(g_mix[l]), w_in[l].astype(BF16), TM, TN_IN)

        y_p, hp, ct, pt = _prompt_mixer(z, cw, cb, wr, br, wi, bi, lm, wp, ps)
        z_s = z[N_PROMPT:].reshape(DEC_SEQ, DEC_BATCH, Z_W)
        ya_s, yb_s, hs, cs, pst = _sample_mixer(
            z_s, state_h[l], _time_major(state_conv[l]), _time_major(state_pool[l]),
            cw, cb, wr, br, wi, bi, lm, wp, ps)
        y_s = jnp.concatenate([ya_s, yb_s], axis=-1).reshape(N_SAMPLE, D_MODEL)
        y = jnp.concatenate([y_p, y_s], axis=0)

        x = _matmul_res(y, w_out[l].astype(BF16), x, TM_OUT, TN_OUT)
        x = _ffn(x, row(g_ffn[l]), w_gate[l].astype(BF16), w_up[l].astype(BF16), w_down[l].astype(BF16))
        p = jnp.concatenate([p_prompt[l].reshape(N_PROMPT, PLE_DIM),
                             _time_major(p_sample[l]).reshape(N_SAMPLE, PLE_DIM)], axis=0).astype(BF16)
        x = _ple(x, row(g_pe[l]), w_pe_gate[l].astype(BF16), p, w_pe_proj[l].astype(BF16))

        hp_l.append(hp.reshape(BATCH, W_A))
        cp_l.append(ct[:, CONV_CARRY - (CONV_W - 1):])
        pp_l.append(pt[:, POOL_CARRY - POOL_PAD:])
        hs_l.append(hs)
        cs_l.append(_time_major(cs))
        ps_l.append(_time_major(pst))

    y = _rmsnorm(x, row(g_final))
    y_prompt = y[:N_PROMPT].reshape(BATCH, SEQ, D_MODEL)
    y_sample = _time_major(y[N_PROMPT:].reshape(DEC_SEQ, DEC_BATCH, D_MODEL))
    return (y_prompt, y_sample,
            jnp.stack(hp_l), jnp.stack(cp_l), jnp.stack(pp_l),
            jnp.stack(hs_l), jnp.stack(cs_l), jnp.stack(ps_l))
```

```python
import functools

import jax
import jax.numpy as jnp
from jax import lax
from jax.experimental import pallas as pl
from jax.experimental.pallas import tpu as pltpu

D_MODEL = 4096
BATCH = 4
SEQ = 2048
DEPTH = 2
DEC_BATCH = 128
DEC_SEQ = 4
PAST_LEN = 16384
W_A = D_MODEL // 2
W_B = D_MODEL - W_A
LRU_HEAD = 256
N_LRU_HEADS = W_A // LRU_HEAD
LRU_C = 8.0
CONV_W = 4
POOL_WINDOWS = (2, 4, 8, 16)
N_POOL_GROUPS = len(POOL_WINDOWS)
POOL_GC = W_B // N_POOL_GROUPS
POOL_PAD = max(POOL_WINDOWS) - 1
D_FF = 11008
PLE_DIM = 256
EPS = 1e-6

N_PROMPT = BATCH * SEQ
N_SAMPLE = DEC_BATCH * DEC_SEQ
N_TOK = N_PROMPT + N_SAMPLE
Z_W = 2 * W_A + W_B

F32 = jnp.float32
BF16 = jnp.bfloat16

VMEM_LIMIT_BYTES = 60 * 1024 * 1024

TM = 1088
TM_DOWN = 544
TM_NORM = 544
TN = 512
TF = 256
TB = 256
CONV_CARRY = 8
POOL_CARRY = 16
SAMPLE_CB = 512


def _params(*sem):
    return pltpu.CompilerParams(dimension_semantics=sem, vmem_limit_bytes=VMEM_LIMIT_BYTES)


def _dot(a, b):
    return jnp.dot(a, b, preferred_element_type=F32)


def _rms_scale(x, g):
    ms = jnp.mean(x * x, axis=-1, keepdims=True)
    return x * lax.rsqrt(ms + EPS) * g


def _log_sigmoid(x):
    return jnp.minimum(x, 0.0) - jnp.log1p(jnp.exp(-jnp.abs(x)))


def _lru_coeffs(xc, r, i, logsig):
    log_a = LRU_C * r * logsig
    a = jnp.exp(log_a)
    mult = jnp.sqrt(1.0 - a * a)
    return a, mult, i * xc


def _rmsnorm_kernel(x_ref, g_ref, o_ref):
    o_ref[...] = _rms_scale(x_ref[...], g_ref[...]).astype(o_ref.dtype)


def _rmsnorm(x, g, dtype, tm=TM_NORM, rows=None, row0=0):
    k = x.shape[1]
    rows = x.shape[0] if rows is None else rows
    assert rows % tm == 0 and row0 % tm == 0
    row_block0 = row0 // tm
    return pl.pallas_call(
        _rmsnorm_kernel,
        grid=(rows // tm,),
        in_specs=[pl.BlockSpec((tm, k), lambda i: (row_block0 + i, 0)),
                  pl.BlockSpec((1, k), lambda i: (0, 0))],
        out_specs=pl.BlockSpec((tm, k), lambda i: (i, 0)),
        out_shape=jax.ShapeDtypeStruct((rows, k), dtype),
        compiler_params=_params("parallel"),
        name="rmsnorm",
    )(x, g)


def _in_proj_kernel(xn_ref, w_ref, o_ref):
    o_ref[...] = _dot(xn_ref[...], w_ref[...].astype(BF16))


def _in_proj(xn, w_all, l):
    m, k = xn.shape
    n = w_all.shape[2]
    return pl.pallas_call(
        _in_proj_kernel,
        grid=(m // TM, n // TN),
        in_specs=[
            pl.BlockSpec((TM, k), lambda i, j: (i, 0)),
            pl.BlockSpec((None, k, TN), lambda i, j: (l, 0, j)),
        ],
        out_specs=pl.BlockSpec((TM, TN), lambda i, j: (i, j)),
        out_shape=jax.ShapeDtypeStruct((m, n), F32),
        compiler_params=_params("parallel", "arbitrary"),
        name="in_proj",
    )(xn, w_all)


def _prompt_mixer_kernel(xa_ref, ga_ref, ub_ref, cw_ref, cb_ref, wr_ref, br_ref, wi_ref, bi_ref,
                         lam_ref, wp_ref, ps_ref,
                         y_ref, hl_ref, ct_ref, pt_ref,
                         hc_ref, cc_ref, pc_ref, a_ref, b_ref):
    t = pl.program_id(1)

    @pl.when(t == 0)
    def _():
        hc_ref[...] = jnp.zeros_like(hc_ref)
        cc_ref[...] = jnp.zeros_like(cc_ref)
        pc_ref[...] = jnp.zeros_like(pc_ref)

    pos = lax.broadcasted_iota(jnp.int32, (TB, 1), 0) + t * TB

    xa = xa_ref[...]
    ext = jnp.concatenate([cc_ref[...], xa], axis=0)
    xc = cb_ref[...] + cw_ref[CONV_W - 1:CONV_W, :] * xa
    for s in range(1, CONV_W):
        xc = xc + cw_ref[CONV_W - 1 - s:CONV_W - s, :] * pltpu.roll(ext, s, axis=0)[CONV_CARRY:]
    xcb = xc.astype(BF16)
    r_parts, i_parts = [], []
    for hd in range(N_LRU_HEADS):
        sl = slice(hd * LRU_HEAD, (hd + 1) * LRU_HEAD)
        r_parts.append(_dot(xcb[:, sl], wr_ref[hd]))
        i_parts.append(_dot(xcb[:, sl], wi_ref[hd]))
    r = jax.nn.sigmoid(jnp.concatenate(r_parts, axis=1) + br_ref[...])
    i = jax.nn.sigmoid(jnp.concatenate(i_parts, axis=1) + bi_ref[...])
    a, mult, ix = _lru_coeffs(xc, r, i, _log_sigmoid(lam_ref[...]))
    start = pos == 0
    a_ref[...] = jnp.where(start, 0.0, a)
    b_ref[...] = jnp.where(start, 1.0, mult) * ix

    def step(row, h):
        h = a_ref[pl.ds(row, 1), :] * h + b_ref[pl.ds(row, 1), :]
        b_ref[pl.ds(row, 1), :] = h
        return h

    h_last = lax.fori_loop(0, TB, step, hc_ref[0:1, :], unroll=8)
    hc_ref[0:1, :] = h_last
    hl_ref[0] = h_last
    y_ref[:, 0:W_A] = (jax.nn.gelu(ga_ref[...]) * b_ref[...]).astype(BF16)
    cc_ref[...] = xa[TB - CONV_CARRY:]
    ct_ref[0] = xa[TB - CONV_CARRY:]

    ub = ub_ref[...]
    pext = jnp.concatenate([pc_ref[...], ub], axis=0)
    for g, w in enumerate(POOL_WINDOWS):
        sl = slice(g * POOL_GC, (g + 1) * POOL_GC)
        s = pext[:, sl]
        shift = 1
        while shift < w:
            s = s + pltpu.roll(s, shift, axis=0)
            shift *= 2
        cnt = jnp.minimum(pos + 1, w).astype(F32)
        d = s[POOL_CARRY:] / cnt - ub[:, sl]
        yb = _dot(d.astype(BF16), wp_ref[g]) * ps_ref[:, sl]
        y_ref[:, W_A + g * POOL_GC:W_A + (g + 1) * POOL_GC] = yb.astype(BF16)
    pc_ref[...] = ub[TB - POOL_CARRY:]
    pt_ref[0] = ub[TB - POOL_CARRY:]


def _prompt_mixer(z, cw, cb, wr, br, wi, bi, lam, wp, ps):
    nt = SEQ // TB
    row = lambda b, t: b * nt + t
    full2 = lambda shape: pl.BlockSpec(shape, lambda b, t: (0, 0))
    full3 = lambda shape: pl.BlockSpec(shape, lambda b, t: (0, 0, 0))
    return pl.pallas_call(
        _prompt_mixer_kernel,
        grid=(BATCH, nt),
        in_specs=[
            pl.BlockSpec((TB, W_A), lambda b, t: (row(b, t), 0)),
            pl.BlockSpec((TB, W_A), lambda b, t: (row(b, t), 1)),
            pl.BlockSpec((TB, W_B), lambda b, t: (row(b, t), 2)),
            full2((CONV_W, W_A)), full2((1, W_A)),
            full3((N_LRU_HEADS, LRU_HEAD, LRU_HEAD)), full2((1, W_A)),
            full3((N_LRU_HEADS, LRU_HEAD, LRU_HEAD)), full2((1, W_A)),
            full2((1, W_A)),
            full3((N_POOL_GROUPS, POOL_GC, POOL_GC)), full2((1, W_B)),
        ],
        out_specs=[
            pl.BlockSpec((TB, D_MODEL), lambda b, t: (row(b, t), 0)),
            pl.BlockSpec((1, 1, W_A), lambda b, t: (b, 0, 0)),
            pl.BlockSpec((1, CONV_CARRY, W_A), lambda b, t: (b, 0, 0)),
            pl.BlockSpec((1, POOL_CARRY, W_B), lambda b, t: (b, 0, 0)),
        ],
        out_shape=[
            jax.ShapeDtypeStruct((N_TOK, D_MODEL), BF16),
            jax.ShapeDtypeStruct((BATCH, 1, W_A), F32),
            jax.ShapeDtypeStruct((BATCH, CONV_CARRY, W_A), F32),
            jax.ShapeDtypeStruct((BATCH, POOL_CARRY, W_B), F32),
        ],
        scratch_shapes=[
            pltpu.VMEM((8, W_A), F32),
            pltpu.VMEM((CONV_CARRY, W_A), F32),
            pltpu.VMEM((POOL_CARRY, W_B), F32),
            pltpu.VMEM((TB, W_A), F32),
            pltpu.VMEM((TB, W_A), F32),
        ],
        compiler_params=_params("parallel", "arbitrary"),
        name="prompt_mixer",
    )(z, z, z, cw, cb, wr, br, wi, bi, lam, wp, ps)


def _sample_mixer_kernel(xa_ref, ga_ref, ub_ref, h0_ref, cs_ref, pst_ref,
                         cw_ref, cb_ref, wr_ref, br_ref, wi_ref, bi_ref, lam_ref, wp_ref, ps_ref,
                         y_in_ref, y_ref, hn_ref, cn_ref, pn_ref):
    del y_in_ref
    c = pl.program_id(0)
    heads = SAMPLE_CB // LRU_HEAD
    rows = lambda t: slice(t * DEC_BATCH, (t + 1) * DEC_BATCH)

    ext = [cs_ref[k] for k in range(CONV_W - 1)] + [xa_ref[rows(t), :] for t in range(DEC_SEQ)]
    logsig = _log_sigmoid(lam_ref[...])
    h = h0_ref[...]
    ya = []
    for t in range(DEC_SEQ):
        xc = cb_ref[...] + cw_ref[0:1, :] * ext[t]
        for k in range(1, CONV_W):
            xc = xc + cw_ref[k:k + 1, :] * ext[t + k]
        xcb = xc.astype(BF16)
        r_parts, i_parts = [], []
        for hd in range(heads):
            sl = slice(hd * LRU_HEAD, (hd + 1) * LRU_HEAD)
            r_parts.append(_dot(xcb[:, sl], wr_ref[hd]))
            i_parts.append(_dot(xcb[:, sl], wi_ref[hd]))
        r = jax.nn.sigmoid(jnp.concatenate(r_parts, axis=1) + br_ref[...])
        i = jax.nn.sigmoid(jnp.concatenate(i_parts, axis=1) + bi_ref[...])
        a, mult, ix = _lru_coeffs(xc, r, i, logsig)
        if PAST_LEN + t == 0:
            a, mult = jnp.zeros_like(a), jnp.ones_like(mult)
        h = a * h + mult * ix
        ya.append((jax.nn.gelu(ga_ref[rows(t), :]) * h).astype(BF16))
    hn_ref[...] = h
    for k in range(CONV_W - 1):
        cn_ref[k] = ext[DEC_SEQ + k]

    for k in range(POOL_PAD):
        src = k + DEC_SEQ
        pn_ref[k] = pst_ref[src] if src < POOL_PAD else ub_ref[rows(src - POOL_PAD), :]

    for g, w in enumerate(POOL_WINDOWS):
        @pl.when(c == g)
        def _(g=g, w=w):
            pext = ([pst_ref[k] for k in range(POOL_PAD - w + 1, POOL_PAD)]
                    + [ub_ref[rows(t), :] for t in range(DEC_SEQ)])
            for t in range(DEC_SEQ):
                s = pext[t]
                for j in range(1, w):
                    s = s + pext[t + j]
                cnt = float(min(PAST_LEN + t + 1, w))
                d = s / cnt - pext[t + w - 1]
                yb = _dot(d.astype(BF16), wp_ref[0]) * ps_ref[...]
                y_ref[rows(t), g * SAMPLE_CB:(g + 1) * SAMPLE_CB] = ya[t]
                y_ref[rows(t), W_A + g * POOL_GC:W_A + (g + 1) * POOL_GC] = yb.astype(BF16)


def _sample_mixer(z, y, h0, cs, pst, cw, cb, wr, br, wi, bi, lam, wp, ps):
    cbw = SAMPLE_CB
    na = W_A // cbw
    heads = cbw // LRU_HEAD
    rb = N_PROMPT // N_SAMPLE
    col = lambda off: (lambda c: (0, 0, off + c))
    vec = pl.BlockSpec((1, cbw), lambda c: (0, c))
    return pl.pallas_call(
        _sample_mixer_kernel,
        grid=(na,),
        in_specs=[
            pl.BlockSpec((N_SAMPLE, cbw), lambda c: (rb, c)),
            pl.BlockSpec((N_SAMPLE, cbw), lambda c: (rb, na + c)),
            pl.BlockSpec((N_SAMPLE, cbw), lambda c: (rb, 2 * na + c)),
            pl.BlockSpec((DEC_BATCH, cbw), lambda c: (0, c)),
            pl.BlockSpec((CONV_W - 1, DEC_BATCH, cbw), col(0)),
            pl.BlockSpec((POOL_PAD, DEC_BATCH, cbw), col(0)),
            pl.BlockSpec((CONV_W, cbw), lambda c: (0, c)), vec,
            pl.BlockSpec((heads, LRU_HEAD, LRU_HEAD), lambda c: (c, 0, 0)), vec,
            pl.BlockSpec((heads, LRU_HEAD, LRU_HEAD), lambda c: (c, 0, 0)), vec,
            vec,
            pl.BlockSpec((1, POOL_GC, POOL_GC), lambda c: (c, 0, 0)), vec,
            pl.BlockSpec(memory_space=pl.ANY),
        ],
        out_specs=[
            pl.BlockSpec((N_SAMPLE, D_MODEL), lambda c: (rb, 0)),
            pl.BlockSpec((DEC_BATCH, cbw), lambda c: (0, c)),
            pl.BlockSpec((CONV_W - 1, DEC_BATCH, cbw), col(0)),
            pl.BlockSpec((POOL_PAD, DEC_BATCH, cbw), col(0)),
        ],
        out_shape=[
            jax.ShapeDtypeStruct((N_TOK, D_MODEL), BF16),
            jax.ShapeDtypeStruct((DEC_BATCH, W_A), F32),
            jax.ShapeDtypeStruct((CONV_W - 1, DEC_BATCH, W_A), F32),
            jax.ShapeDtypeStruct((POOL_PAD, DEC_BATCH, W_B), F32),
        ],
        input_output_aliases={15: 0},
        compiler_params=_params("arbitrary"),
        name="sample_mixer",
    )(z, z, z, h0, cs, pst, cw, cb, wr, br, wi, bi, lam, wp, ps, y)


def _out_proj_kernel(y_ref, w_ref, x_ref, o_ref):
    o_ref[...] = x_ref[...] + _dot(y_ref[...], w_ref[...].astype(BF16))


def _out_proj(y, w_all, l, x):
    m, k = y.shape
    n = w_all.shape[2]
    return pl.pallas_call(
        _out_proj_kernel,
        grid=(m // TM, n // TN),
        in_specs=[
            pl.BlockSpec((TM, k), lambda i, j: (i, 0)),
            pl.BlockSpec((None, k, TN), lambda i, j: (l, 0, j)),
            pl.BlockSpec((TM, TN), lambda i, j: (i, j)),
        ],
        out_specs=pl.BlockSpec((TM, TN), lambda i, j: (i, j)),
        out_shape=jax.ShapeDtypeStruct((m, n), F32),
        compiler_params=_params("parallel", "arbitrary"),
        name="out_proj",
    )(y, w_all, x)


def _ffn_up_kernel(xn_ref, wg_ref, wu_ref, wd_ref, h_ref, wdb_ref):
    xn = xn_ref[...]
    gate = _dot(xn, wg_ref[...].astype(BF16))
    up = _dot(xn, wu_ref[...].astype(BF16))
    h_ref[...] = (jax.nn.silu(gate) * up).astype(BF16)
    wdb_ref[...] = wd_ref[...].astype(BF16)


def _ffn_up(xn, wg_all, wu_all, wd_all, l):
    m, k = xn.shape
    nf = D_FF // TF
    steps = (m // TM) * nf
    wd_rows = D_FF // steps
    assert wd_rows * steps == D_FF and wd_rows % 16 == 0
    return pl.pallas_call(
        _ffn_up_kernel,
        grid=(m // TM, nf),
        in_specs=[
            pl.BlockSpec((TM, k), lambda i, j: (i, 0)),
            pl.BlockSpec((None, k, TF), lambda i, j: (l, 0, j)),
            pl.BlockSpec((None, k, TF), lambda i, j: (l, 0, j)),
            pl.BlockSpec((None, wd_rows, k), lambda i, j: (l, i * nf + j, 0)),
        ],
        out_specs=[
            pl.BlockSpec((TM, TF), lambda i, j: (i, j)),
            pl.BlockSpec((wd_rows, k), lambda i, j: (i * nf + j, 0)),
        ],
        out_shape=[
            jax.ShapeDtypeStruct((m, D_FF), BF16),
            jax.ShapeDtypeStruct((D_FF, k), BF16),
        ],
        compiler_params=_params("arbitrary", "arbitrary"),
        name="ffn_up",
    )(xn, wg_all, wu_all, wd_all)


def _ffn_down_kernel(h_ref, w_ref, x_ref, o_ref):
    o_ref[...] = x_ref[...] + _dot(h_ref[...], w_ref[...])


def _ffn_down(h, wdb, x):
    m, kf = h.shape
    n = wdb.shape[1]
    return pl.pallas_call(
        _ffn_down_kernel,
        grid=(m // TM_DOWN, n // TN),
        in_specs=[
            pl.BlockSpec((TM_DOWN, kf), lambda i, j: (i, 0)),
            pl.BlockSpec((kf, TN), lambda i, j: (0, j)),
            pl.BlockSpec((TM_DOWN, TN), lambda i, j: (i, j)),
        ],
        out_specs=pl.BlockSpec((TM_DOWN, TN), lambda i, j: (i, j)),
        out_shape=jax.ShapeDtypeStruct((m, n), F32),
        compiler_params=_params("parallel", "arbitrary"),
        name="ffn_down",
    )(h, wdb, x)


def _ple_kernel(xn_ref, wg_ref, p_ref, wp_ref, x_ref, o_ref):
    gate = jax.nn.sigmoid(_dot(xn_ref[...], wg_ref[...].astype(BF16)))
    proj = _dot(p_ref[...], wp_ref[...].astype(BF16))
    o_ref[...] = x_ref[...] + gate * proj


def _ple(xn, wg_all, p, wp_all, l, x):
    m, k = xn.shape
    return pl.pallas_call(
        _ple_kernel,
        grid=(m // TM, k // TN),
        in_specs=[
            pl.BlockSpec((TM, k), lambda i, j: (i, 0)),
            pl.BlockSpec((None, k, TN), lambda i, j: (l, 0, j)),
            pl.BlockSpec((TM, PLE_DIM), lambda i, j: (i, 0)),
            pl.BlockSpec((None, PLE_DIM, TN), lambda i, j: (l, 0, j)),
            pl.BlockSpec((TM, TN), lambda i, j: (i, j)),
        ],
        out_specs=pl.BlockSpec((TM, TN), lambda i, j: (i, j)),
        out_shape=jax.ShapeDtypeStruct((m, k), F32),
        compiler_params=_params("parallel", "arbitrary"),
        name="ple",
    )(xn, wg_all, p, wp_all, x)


def _time_major(a):
    return jnp.swapaxes(a, 0, 1)


def kernel(x_prompt, x_sample, state_h, state_conv, state_pool, p_prompt, p_sample, g_mix, w_in, conv_w, conv_b, w_rg, b_rg, w_ig, b_ig, lam, w_pool, pool_scale, w_out, g_ffn, w_gate, w_up, w_down, g_pe, w_pe_gate, w_pe_proj, g_final):
    row = lambda v: v.reshape(1, -1)
    x = jnp.concatenate([x_prompt.reshape(N_PROMPT, D_MODEL),
                         _time_major(x_sample).reshape(N_SAMPLE, D_MODEL)], axis=0)
    hp_l, cp_l, pp_l, hs_l, cs_l, ps_l = [], [], [], [], [], []
    for l in range(DEPTH):
        cw, cb = conv_w[l], row(conv_b[l])
        wr, br = w_rg[l].astype(BF16), row(b_rg[l])
        wi, bi = w_ig[l].astype(BF16), row(b_ig[l])
        lm, wp, ps = row(lam[l]), w_pool[l].astype(BF16), row(pool_scale[l])

        z = _in_proj(_rmsnorm(x, row(g_mix[l]), BF16), w_in, l)

        y, hp, ct, pt = _prompt_mixer(z, cw, cb, wr, br, wi, bi, lm, wp, ps)
        y, hs, cs, pst = _sample_mixer(
            z, y, state_h[l], _time_major(state_conv[l]), _time_major(state_pool[l]),
            cw, cb, wr, br, wi, bi, lm, wp, ps)

        x = _out_proj(y, w_out, l, x)
        h, wdb = _ffn_up(_rmsnorm(x, row(g_ffn[l]), BF16), w_gate, w_up, w_down, l)
        x = _ffn_down(h, wdb, x)
        p = jnp.concatenate([p_prompt[l].reshape(N_PROMPT, PLE_DIM),
                             _time_major(p_sample[l]).reshape(N_SAMPLE, PLE_DIM)], axis=0).astype(BF16)
        x = _ple(_rmsnorm(x, row(g_pe[l]), BF16), w_pe_gate, p, w_pe_proj, l, x)

        hp_l.append(hp.reshape(BATCH, W_A))
        cp_l.append(ct[:, CONV_CARRY - (CONV_W - 1):])
        pp_l.append(pt[:, POOL_CARRY - POOL_PAD:])
        hs_l.append(hs)
        cs_l.append(_time_major(cs))
        ps_l.append(_time_major(pst))

    gf = row(g_final)
    y_prompt = _rmsnorm(x, gf, F32, tm=N_SAMPLE, rows=N_PROMPT).reshape(BATCH, SEQ, D_MODEL)
    y_s = _rmsnorm(x, gf, F32, tm=N_SAMPLE, rows=N_SAMPLE, row0=N_PROMPT)
    y_sample = _time_major(y_s.reshape(DEC_SEQ, DEC_BATCH, D_MODEL))
    return (y_prompt, y_sample,
            jnp.stack(hp_l), jnp.stack(cp_l), jnp.stack(pp_l),
            jnp.stack(hs_l), jnp.stack(cs_l), jnp.stack(ps_l))
```

```python
import functools

import jax
import jax.numpy as jnp
from jax import lax
from jax.experimental import pallas as pl
from jax.experimental.pallas import tpu as pltpu

D_MODEL = 4096
BATCH = 4
SEQ = 2048
DEPTH = 2
DEC_BATCH = 128
DEC_SEQ = 4
PAST_LEN = 16384
W_A = D_MODEL // 2
W_B = D_MODEL - W_A
LRU_HEAD = 256
N_LRU_HEADS = W_A // LRU_HEAD
LRU_C = 8.0
CONV_W = 4
POOL_WINDOWS = (2, 4, 8, 16)
N_POOL_GROUPS = len(POOL_WINDOWS)
POOL_GC = W_B // N_POOL_GROUPS
POOL_PAD = max(POOL_WINDOWS) - 1
D_FF = 11008
PLE_DIM = 256
EPS = 1e-6

N_PROMPT = BATCH * SEQ
N_SAMPLE = DEC_BATCH * DEC_SEQ
N_TOK = N_PROMPT + N_SAMPLE
Z_W = 2 * W_A + W_B

F32 = jnp.float32
BF16 = jnp.bfloat16

VMEM_LIMIT_BYTES = 60 * 1024 * 1024

TM = 1088
TM_DOWN = 544
TM_EMBED = 256
TM_FINAL = 512
LANES = 128
TN = 512
TF = 256
TB = 256
CONV_CARRY = 8
POOL_CARRY = 16
SAMPLE_CB = 512


def _params(*sem):
    return pltpu.CompilerParams(dimension_semantics=sem, vmem_limit_bytes=VMEM_LIMIT_BYTES)


def _dot(a, b):
    return jnp.dot(a, b, preferred_element_type=F32)


def _log_sigmoid(x):
    return jnp.minimum(x, 0.0) - jnp.log1p(jnp.exp(-jnp.abs(x)))


def _lru_coeffs(xc, r, i, logsig):
    log_a = LRU_C * r * logsig
    a = jnp.exp(log_a)
    mult = jnp.sqrt(1.0 - a * a)
    return a, mult, i * xc


def _lane_partial_sumsq(x):
    sq = x * x
    part = sq[:, 0:LANES]
    for c in range(1, x.shape[1] // LANES):
        part = part + sq[:, c * LANES:(c + 1) * LANES]
    return part


def _inv_rms_lanes(partial_sumsq):
    tot = jnp.sum(partial_sumsq, axis=-1, keepdims=True)
    return jnp.broadcast_to(lax.rsqrt(tot * (1.0 / D_MODEL) + EPS), partial_sumsq.shape)


def _scale_rows(acc, inv_ref):
    inv = inv_ref[...]
    return jnp.concatenate(
        [acc[:, c * LANES:(c + 1) * LANES] * inv for c in range(acc.shape[1] // LANES)], axis=1)


def _store_residual(x_new, g_ref, o_ref, xg_ref, inv_ref):
    j, last = pl.program_id(1), pl.num_programs(1) - 1
    o_ref[...] = x_new
    xg_ref[...] = (x_new * g_ref[...]).astype(BF16)
    part = _lane_partial_sumsq(x_new)

    @pl.when(j == 0)
    def _():
        inv_ref[...] = part

    @pl.when(jnp.logical_and(j > 0, j < last))
    def _():
        inv_ref[...] += part

    @pl.when(j == last)
    def _():
        inv_ref[...] = _inv_rms_lanes(inv_ref[...] + part)


def _residual_out(m, n, tm):
    specs = [
        pl.BlockSpec((tm, TN), lambda i, j: (i, j)),
        pl.BlockSpec((tm, TN), lambda i, j: (i, j)),
        pl.BlockSpec((tm, LANES), lambda i, j: (i, 0)),
    ]
    shapes = [
        jax.ShapeDtypeStruct((m, n), F32),
        jax.ShapeDtypeStruct((m, n), BF16),
        jax.ShapeDtypeStruct((m, LANES), F32),
    ]
    return specs, shapes


def _embed_kernel(xp_ref, xs_ref, g_ref, x_ref, xg_ref, inv_ref):
    def emit(x):
        x_ref[...] = x
        xg_ref[...] = (x * g_ref[...]).astype(BF16)
        inv_ref[...] = _inv_rms_lanes(_lane_partial_sumsq(x))

    is_prompt = pl.program_id(0) < N_PROMPT // TM_EMBED

    @pl.when(is_prompt)
    def _():
        emit(xp_ref[...])

    @pl.when(jnp.logical_not(is_prompt))
    def _():
        emit(xs_ref[...])


def _embed(xp, xs, g):
    k = xp.shape[1]
    npb = N_PROMPT // TM_EMBED
    rows = lambda i: (i, 0)
    return pl.pallas_call(
        _embed_kernel,
        grid=(N_TOK // TM_EMBED,),
        in_specs=[
            pl.BlockSpec((TM_EMBED, k), lambda i: (jnp.minimum(i, npb - 1), 0)),
            pl.BlockSpec((TM_EMBED, k), lambda i: (jnp.maximum(i - npb, 0), 0)),
            pl.BlockSpec((1, k), lambda i: (0, 0)),
        ],
        out_specs=[
            pl.BlockSpec((TM_EMBED, k), rows),
            pl.BlockSpec((TM_EMBED, k), rows),
            pl.BlockSpec((TM_EMBED, LANES), rows),
        ],
        out_shape=[
            jax.ShapeDtypeStruct((N_TOK, k), F32),
            jax.ShapeDtypeStruct((N_TOK, k), BF16),
            jax.ShapeDtypeStruct((N_TOK, LANES), F32),
        ],
        compiler_params=_params("arbitrary"),
        name="embed",
    )(xp, xs, g)


def _final_norm_kernel(x_ref, inv_ref, g_ref, o_ref):
    o_ref[...] = _scale_rows(x_ref[...], inv_ref) * g_ref[...]


def _final_norm(x, inv, g, rows, row0):
    k = x.shape[1]
    tm = TM_FINAL
    assert rows % tm == 0 and row0 % tm == 0
    rb0 = row0 // tm
    return pl.pallas_call(
        _final_norm_kernel,
        grid=(rows // tm,),
        in_specs=[pl.BlockSpec((tm, k), lambda i: (rb0 + i, 0)),
                  pl.BlockSpec((tm, LANES), lambda i: (rb0 + i, 0)),
                  pl.BlockSpec((1, k), lambda i: (0, 0))],
        out_specs=pl.BlockSpec((tm, k), lambda i: (i, 0)),
        out_shape=jax.ShapeDtypeStruct((rows, k), F32),
        compiler_params=_params("parallel"),
        name="final_norm",
    )(x, inv, g)


def _in_proj_kernel(xg_ref, inv_ref, w_ref, o_ref):
    o_ref[...] = _scale_rows(_dot(xg_ref[...], w_ref[...].astype(BF16)), inv_ref)


def _in_proj(xg, inv, w_all, l):
    m, k = xg.shape
    n = w_all.shape[2]
    return pl.pallas_call(
        _in_proj_kernel,
        grid=(m // TM, n // TN),
        in_specs=[
            pl.BlockSpec((TM, k), lambda i, j: (i, 0)),
            pl.BlockSpec((TM, LANES), lambda i, j: (i, 0)),
            pl.BlockSpec((None, k, TN), lambda i, j: (l, 0, j)),
        ],
        out_specs=pl.BlockSpec((TM, TN), lambda i, j: (i, j)),
        out_shape=jax.ShapeDtypeStruct((m, n), F32),
        compiler_params=_params("parallel", "arbitrary"),
        name="in_proj",
    )(xg, inv, w_all)


def _prompt_mixer_kernel(xa_ref, ga_ref, ub_ref, cw_ref, cb_ref, wr_ref, br_ref, wi_ref, bi_ref,
                         lam_ref, wp_ref, ps_ref,
                         y_ref, hl_ref, ct_ref, pt_ref,
                         hc_ref, cc_ref, pc_ref, a_ref, b_ref):
    t = pl.program_id(1)

    @pl.when(t == 0)
    def _():
        hc_ref[...] = jnp.zeros_like(hc_ref)
        cc_ref[...] = jnp.zeros_like(cc_ref)
        pc_ref[...] = jnp.zeros_like(pc_ref)

    pos = lax.broadcasted_iota(jnp.int32, (TB, 1), 0) + t * TB

    xa = xa_ref[...]
    ext = jnp.concatenate([cc_ref[...], xa], axis=0)
    xc = cb_ref[...] + cw_ref[CONV_W - 1:CONV_W, :] * xa
    for s in range(1, CONV_W):
        xc = xc + cw_ref[CONV_W - 1 - s:CONV_W - s, :] * pltpu.roll(ext, s, axis=0)[CONV_CARRY:]
    xcb = xc.astype(BF16)
    r_parts, i_parts = [], []
    for hd in range(N_LRU_HEADS):
        sl = slice(hd * LRU_HEAD, (hd + 1) * LRU_HEAD)
        r_parts.append(_dot(xcb[:, sl], wr_ref[hd]))
        i_parts.append(_dot(xcb[:, sl], wi_ref[hd]))
    r = jax.nn.sigmoid(jnp.concatenate(r_parts, axis=1) + br_ref[...])
    i = jax.nn.sigmoid(jnp.concatenate(i_parts, axis=1) + bi_ref[...])
    a, mult, ix = _lru_coeffs(xc, r, i, _log_sigmoid(lam_ref[...]))
    a_ref[...] = a
    b_ref[...] = mult * ix

    @pl.when(t == 0)
    def _():
        a_ref[0:1, :] = jnp.zeros((1, W_A), F32)
        b_ref[0:1, :] = ix[0:1, :]

    def step(row, h):
        h = a_ref[pl.ds(row, 1), :] * h + b_ref[pl.ds(row, 1), :]
        b_ref[pl.ds(row, 1), :] = h
        return h

    h_last = lax.fori_loop(0, TB, step, hc_ref[0:1, :], unroll=8)
    hc_ref[0:1, :] = h_last
    hl_ref[0] = h_last
    y_ref[:, 0:W_A] = (jax.nn.gelu(ga_ref[...]) * b_ref[...]).astype(BF16)
    cc_ref[...] = xa[TB - CONV_CARRY:]
    ct_ref[0] = xa[TB - CONV_CARRY:]

    ub = ub_ref[...]
    pext = jnp.concatenate([pc_ref[...], ub], axis=0)
    for g, w in enumerate(POOL_WINDOWS):
        sl = slice(g * POOL_GC, (g + 1) * POOL_GC)
        s = pext[:, sl]
        shift = 1
        while shift < w:
            s = s + pltpu.roll(s, shift, axis=0)
            shift *= 2
        cnt = jnp.minimum(pos + 1, w).astype(F32)
        d = s[POOL_CARRY:] / cnt - ub[:, sl]
        yb = _dot(d.astype(BF16), wp_ref[g]) * ps_ref[:, sl]
        y_ref[:, W_A + g * POOL_GC:W_A + (g + 1) * POOL_GC] = yb.astype(BF16)
    pc_ref[...] = ub[TB - POOL_CARRY:]
    pt_ref[0] = ub[TB - POOL_CARRY:]


def _prompt_mixer(z, cw, cb, wr, br, wi, bi, lam, wp, ps):
    nt = SEQ // TB
    row = lambda b, t: b * nt + t
    full2 = lambda shape: pl.BlockSpec(shape, lambda b, t: (0, 0))
    full3 = lambda shape: pl.BlockSpec(shape, lambda b, t: (0, 0, 0))
    return pl.pallas_call(
        _prompt_mixer_kernel,
        grid=(BATCH, nt),
        in_specs=[
            pl.BlockSpec((TB, W_A), lambda b, t: (row(b, t), 0)),
            pl.BlockSpec((TB, W_A), lambda b, t: (row(b, t), 1)),
            pl.BlockSpec((TB, W_B), lambda b, t: (row(b, t), 2)),
            full2((CONV_W, W_A)), full2((1, W_A)),
            full3((N_LRU_HEADS, LRU_HEAD, LRU_HEAD)), full2((1, W_A)),
            full3((N_LRU_HEADS, LRU_HEAD, LRU_HEAD)), full2((1, W_A)),
            full2((1, W_A)),
            full3((N_POOL_GROUPS, POOL_GC, POOL_GC)), full2((1, W_B)),
        ],
        out_specs=[
            pl.BlockSpec((TB, D_MODEL), lambda b, t: (row(b, t), 0)),
            pl.BlockSpec((1, 1, W_A), lambda b, t: (b, 0, 0)),
            pl.BlockSpec((1, CONV_CARRY, W_A), lambda b, t: (b, 0, 0)),
            pl.BlockSpec((1, POOL_CARRY, W_B), lambda b, t: (b, 0, 0)),
        ],
        out_shape=[
            jax.ShapeDtypeStruct((N_TOK, D_MODEL), BF16),
            jax.ShapeDtypeStruct((BATCH, 1, W_A), F32),
            jax.ShapeDtypeStruct((BATCH, CONV_CARRY, W_A), F32),
            jax.ShapeDtypeStruct((BATCH, POOL_CARRY, W_B), F32),
        ],
        scratch_shapes=[
            pltpu.VMEM((8, W_A), F32),
            pltpu.VMEM((CONV_CARRY, W_A), F32),
            pltpu.VMEM((POOL_CARRY, W_B), F32),
            pltpu.VMEM((TB, W_A), F32),
            pltpu.VMEM((TB, W_A), F32),
        ],
        compiler_params=_params("parallel", "arbitrary"),
        name="prompt_mixer",
    )(z, z, z, cw, cb, wr, br, wi, bi, lam, wp, ps)


def _sample_mixer_kernel(xa_ref, ga_ref, ub_ref, h0_ref, cs_ref, pst_ref,
                         cw_ref, cb_ref, wr_ref, br_ref, wi_ref, bi_ref, lam_ref, wp_ref, ps_ref,
                         y_in_ref, y_ref, hn_ref, cn_ref, pn_ref):
    del y_in_ref
    c = pl.program_id(0)
    heads = SAMPLE_CB // LRU_HEAD
    rows = lambda t: slice(t * DEC_BATCH, (t + 1) * DEC_BATCH)

    ext = [cs_ref[k] for k in range(CONV_W - 1)] + [xa_ref[rows(t), :] for t in range(DEC_SEQ)]
    logsig = _log_sigmoid(lam_ref[...])
    h = h0_ref[...]
    ya = []
    for t in range(DEC_SEQ):
        xc = cb_ref[...] + cw_ref[0:1, :] * ext[t]
        for k in range(1, CONV_W):
            xc = xc + cw_ref[k:k + 1, :] * ext[t + k]
        xcb = xc.astype(BF16)
        r_parts, i_parts = [], []
        for hd in range(heads):
            sl = slice(hd * LRU_HEAD, (hd + 1) * LRU_HEAD)
            r_parts.append(_dot(xcb[:, sl], wr_ref[hd]))
            i_parts.append(_dot(xcb[:, sl], wi_ref[hd]))
        r = jax.nn.sigmoid(jnp.concatenate(r_parts, axis=1) + br_ref[...])
        i = jax.nn.sigmoid(jnp.concatenate(i_parts, axis=1) + bi_ref[...])
        a, mult, ix = _lru_coeffs(xc, r, i, logsig)
        if PAST_LEN + t == 0:
            a, mult = jnp.zeros_like(a), jnp.ones_like(mult)
        h = a * h + mult * ix
        ya.append((jax.nn.gelu(ga_ref[rows(t), :]) * h).astype(BF16))
    hn_ref[...] = h
    for k in range(CONV_W - 1):
        cn_ref[k] = ext[DEC_SEQ + k]

    for k in range(POOL_PAD):
        src = k + DEC_SEQ
        pn_ref[k] = pst_ref[src] if src < POOL_PAD else ub_ref[rows(src - POOL_PAD), :]

    for g, w in enumerate(POOL_WINDOWS):
        @pl.when(c == g)
        def _(g=g, w=w):
            pext = ([pst_ref[k] for k in range(POOL_PAD - w + 1, POOL_PAD)]
                    + [ub_ref[rows(t), :] for t in range(DEC_SEQ)])
            for t in range(DEC_SEQ):
                s = pext[t]
                for j in range(1, w):
                    s = s + pext[t + j]
                cnt = float(min(PAST_LEN + t + 1, w))
                d = s / cnt - pext[t + w - 1]
                yb = _dot(d.astype(BF16), wp_ref[0]) * ps_ref[...]
                y_ref[rows(t), g * SAMPLE_CB:(g + 1) * SAMPLE_CB] = ya[t]
                y_ref[rows(t), W_A + g * POOL_GC:W_A + (g + 1) * POOL_GC] = yb.astype(BF16)


def _sample_mixer(z, y, h0, cs, pst, cw, cb, wr, br, wi, bi, lam, wp, ps):
    cbw = SAMPLE_CB
    na = W_A // cbw
    heads = cbw // LRU_HEAD
    rb = N_PROMPT // N_SAMPLE
    col = lambda off: (lambda c: (0, 0, off + c))
    vec = pl.BlockSpec((1, cbw), lambda c: (0, c))
    return pl.pallas_call(
        _sample_mixer_kernel,
        grid=(na,),
        in_specs=[
            pl.BlockSpec((N_SAMPLE, cbw), lambda c: (rb, c)),
            pl.BlockSpec((N_SAMPLE, cbw), lambda c: (rb, na + c)),
            pl.BlockSpec((N_SAMPLE, cbw), lambda c: (rb, 2 * na + c)),
            pl.BlockSpec((DEC_BATCH, cbw), lambda c: (0, c)),
            pl.BlockSpec((CONV_W - 1, DEC_BATCH, cbw), col(0)),
            pl.BlockSpec((POOL_PAD, DEC_BATCH, cbw), col(0)),
            pl.BlockSpec((CONV_W, cbw), lambda c: (0, c)), vec,
            pl.BlockSpec((heads, LRU_HEAD, LRU_HEAD), lambda c: (c, 0, 0)), vec,
            pl.BlockSpec((heads, LRU_HEAD, LRU_HEAD), lambda c: (c, 0, 0)), vec,
            vec,
            pl.BlockSpec((1, POOL_GC, POOL_GC), lambda c: (c, 0, 0)), vec,
            pl.BlockSpec(memory_space=pl.ANY),
        ],
        out_specs=[
            pl.BlockSpec((N_SAMPLE, D_MODEL), lambda c: (rb, 0)),
            pl.BlockSpec((DEC_BATCH, cbw), lambda c: (0, c)),
            pl.BlockSpec((CONV_W - 1, DEC_BATCH, cbw), col(0)),
            pl.BlockSpec((POOL_PAD, DEC_BATCH, cbw), col(0)),
        ],
        out_shape=[
            jax.ShapeDtypeStruct((N_TOK, D_MODEL), BF16),
            jax.ShapeDtypeStruct((DEC_BATCH, W_A), F32),
            jax.ShapeDtypeStruct((CONV_W - 1, DEC_BATCH, W_A), F32),
            jax.ShapeDtypeStruct((POOL_PAD, DEC_BATCH, W_B), F32),
        ],
        input_output_aliases={15: 0},
        compiler_params=_params("arbitrary"),
        name="sample_mixer",
    )(z, z, z, h0, cs, pst, cw, cb, wr, br, wi, bi, lam, wp, ps, y)


def _out_proj_kernel(y_ref, w_ref, x_ref, g_ref, o_ref, xg_ref, inv_ref):
    x_new = x_ref[...] + _dot(y_ref[...], w_ref[...].astype(BF16))
    _store_residual(x_new, g_ref, o_ref, xg_ref, inv_ref)


def _out_proj(y, w_all, l, x, g_next):
    m, k = y.shape
    n = w_all.shape[2]
    out_specs, out_shape = _residual_out(m, n, TM)
    return pl.pallas_call(
        _out_proj_kernel,
        grid=(m // TM, n // TN),
        in_specs=[
            pl.BlockSpec((TM, k), lambda i, j: (i, 0)),
            pl.BlockSpec((None, k, TN), lambda i, j: (l, 0, j)),
            pl.BlockSpec((TM, TN), lambda i, j: (i, j)),
            pl.BlockSpec((1, TN), lambda i, j: (0, j)),
        ],
        out_specs=out_specs,
        out_shape=out_shape,
        compiler_params=_params("parallel", "arbitrary"),
        name="out_proj",
    )(y, w_all, x, g_next)


def _ffn_up_kernel(xg_ref, inv_ref, wg_ref, wu_ref, wd_ref, h_ref, wdb_ref):
    xg = xg_ref[...]
    gate = _scale_rows(_dot(xg, wg_ref[...].astype(BF16)), inv_ref)
    up = _scale_rows(_dot(xg, wu_ref[...].astype(BF16)), inv_ref)
    h_ref[...] = (jax.nn.silu(gate) * up).astype(BF16)
    wdb_ref[...] = wd_ref[...].astype(BF16)


def _ffn_up(xg, inv, wg_all, wu_all, wd_all, l):
    m, k = xg.shape
    nf = D_FF // TF
    steps = (m // TM) * nf
    wd_rows = D_FF // steps
    assert wd_rows * steps == D_FF and wd_rows % 16 == 0
    return pl.pallas_call(
        _ffn_up_kernel,
        grid=(m // TM, nf),
        in_specs=[
            pl.BlockSpec((TM, k), lambda i, j: (i, 0)),
            pl.BlockSpec((TM, LANES), lambda i, j: (i, 0)),
            pl.BlockSpec((None, k, TF), lambda i, j: (l, 0, j)),
            pl.BlockSpec((None, k, TF), lambda i, j: (l, 0, j)),
            pl.BlockSpec((None, wd_rows, k), lambda i, j: (l, i * nf + j, 0)),
        ],
        out_specs=[
            pl.BlockSpec((TM, TF), lambda i, j: (i, j)),
            pl.BlockSpec((wd_rows, k), lambda i, j: (i * nf + j, 0)),
        ],
        out_shape=[
            jax.ShapeDtypeStruct((m, D_FF), BF16),
            jax.ShapeDtypeStruct((D_FF, k), BF16),
        ],
        compiler_params=_params("arbitrary", "arbitrary"),
        name="ffn_up",
    )(xg, inv, wg_all, wu_all, wd_all)


def _ffn_down_kernel(h_ref, w_ref, x_ref, g_ref, o_ref, xg_ref, inv_ref):
    x_new = x_ref[...] + _dot(h_ref[...], w_ref[...])
    _store_residual(x_new, g_ref, o_ref, xg_ref, inv_ref)


def _ffn_down(h, wdb, x, g_next):
    m, kf = h.shape
    n = wdb.shape[1]
    out_specs, out_shape = _residual_out(m, n, TM_DOWN)
    return pl.pallas_call(
        _ffn_down_kernel,
        grid=(m // TM_DOWN, n // TN),
        in_specs=[
            pl.BlockSpec((TM_DOWN, kf), lambda i, j: (i, 0)),
            pl.BlockSpec((kf, TN), lambda i, j: (0, j)),
            pl.BlockSpec((TM_DOWN, TN), lambda i, j: (i, j)),
            pl.BlockSpec((1, TN), lambda i, j: (0, j)),
        ],
        out_specs=out_specs,
        out_shape=out_shape,
        compiler_params=_params("parallel", "arbitrary"),
        name="ffn_down",
    )(h, wdb, x, g_next)


def _ple_kernel(xg_ref, inv_ref, wg_ref, p_ref, wp_ref, x_ref, g_ref, o_ref, xgn_ref, invn_ref):
    gate = jax.nn.sigmoid(_scale_rows(_dot(xg_ref[...], wg_ref[...].astype(BF16)), inv_ref))
    proj = _dot(p_ref[...], wp_ref[...].astype(BF16))
    x_new = x_ref[...] + gate * proj
    _store_residual(x_new, g_ref, o_ref, xgn_ref, invn_ref)


def _ple(xg, inv, wg_all, p, wp_all, l, x, g_next):
    m, k = xg.shape
    out_specs, out_shape = _residual_out(m, k, TM)
    return pl.pallas_call(
        _ple_kernel,
        grid=(m // TM, k // TN),
        in_specs=[
            pl.BlockSpec((TM, k), lambda i, j: (i, 0)),
            pl.BlockSpec((TM, LANES), lambda i, j: (i, 0)),
            pl.BlockSpec((None, k, TN), lambda i, j: (l, 0, j)),
            pl.BlockSpec((TM, PLE_DIM), lambda i, j: (i, 0)),
            pl.BlockSpec((None, PLE_DIM, TN), lambda i, j: (l, 0, j)),
            pl.BlockSpec((TM, TN), lambda i, j: (i, j)),
            pl.BlockSpec((1, TN), lambda i, j: (0, j)),
        ],
        out_specs=out_specs,
        out_shape=out_shape,
        compiler_params=_params("parallel", "arbitrary"),
        name="ple",
    )(xg, inv, wg_all, p, wp_all, x, g_next)


def _time_major(a):
    return jnp.swapaxes(a, 0, 1)


def kernel(x_prompt, x_sample, state_h, state_conv, state_pool, p_prompt, p_sample, g_mix, w_in, conv_w, conv_b, w_rg, b_rg, w_ig, b_ig, lam, w_pool, pool_scale, w_out, g_ffn, w_gate, w_up, w_down, g_pe, w_pe_gate, w_pe_proj, g_final):
    row = lambda v: v.reshape(1, -1)
    x, xg, inv = _embed(x_prompt.reshape(N_PROMPT, D_MODEL),
                       _time_major(x_sample).reshape(N_SAMPLE, D_MODEL), row(g_mix[0]))
    hp_l, cp_l, pp_l, hs_l, cs_l, ps_l = [], [], [], [], [], []
    for l in range(DEPTH):
        cw, cb = conv_w[l], row(conv_b[l])
        wr, br = w_rg[l].astype(BF16), row(b_rg[l])
        wi, bi = w_ig[l].astype(BF16), row(b_ig[l])
        lm, wp, ps = row(lam[l]), w_pool[l].astype(BF16), row(pool_scale[l])

        g_after_ple = row(g_mix[l + 1]) if l + 1 < DEPTH else row(g_final)

        z = _in_proj(xg, inv, w_in, l)

        y, hp, ct, pt = _prompt_mixer(z, cw, cb, wr, br, wi, bi, lm, wp, ps)
        y, hs, cs, pst = _sample_mixer(
            z, y, state_h[l], _time_major(state_conv[l]), _time_major(state_pool[l]),
            cw, cb, wr, br, wi, bi, lm, wp, ps)

        x, xg, inv = _out_proj(y, w_out, l, x, row(g_ffn[l]))
        h, wdb = _ffn_up(xg, inv, w_gate, w_up, w_down, l)
        x, xg, inv = _ffn_down(h, wdb, x, row(g_pe[l]))
        p = jnp.concatenate([p_prompt[l].reshape(N_PROMPT, PLE_DIM),
                             _time_major(p_sample[l]).reshape(N_SAMPLE, PLE_DIM)], axis=0).astype(BF16)
        x, xg, inv = _ple(xg, inv, w_pe_gate, p, w_pe_proj, l, x, g_after_ple)

        hp_l.append(hp.reshape(BATCH, W_A))
        cp_l.append(ct[:, CONV_CARRY - (CONV_W - 1):])
        pp_l.append(pt[:, POOL_CARRY - POOL_PAD:])
        hs_l.append(hs)
        cs_l.append(_time_major(cs))
        ps_l.append(_time_major(pst))

    gf = row(g_final)
    y_prompt = _final_norm(x, inv, gf, N_PROMPT, 0).reshape(BATCH, SEQ, D_MODEL)
    y_s = _final_norm(x, inv, gf, N_SAMPLE, N_PROMPT)
    y_sample = _time_major(y_s.reshape(DEC_SEQ, DEC_BATCH, D_MODEL))
    return (y_prompt, y_sample,
            jnp.stack(hp_l), jnp.stack(cp_l), jnp.stack(pp_l),
            jnp.stack(hs_l), jnp.stack(cs_l), jnp.stack(ps_l))
```

```python
import functools

import jax
import jax.numpy as jnp
from jax import lax
from jax.experimental import pallas as pl
from jax.experimental.pallas import tpu as pltpu

D_MODEL = 4096
BATCH = 4
SEQ = 2048
DEPTH = 2
DEC_BATCH = 128
DEC_SEQ = 4
PAST_LEN = 16384
W_A = D_MODEL // 2
W_B = D_MODEL - W_A
LRU_HEAD = 256
N_LRU_HEADS = W_A // LRU_HEAD
LRU_C = 8.0
CONV_W = 4
POOL_WINDOWS = (2, 4, 8, 16)
N_POOL_GROUPS = len(POOL_WINDOWS)
POOL_GC = W_B // N_POOL_GROUPS
POOL_PAD = max(POOL_WINDOWS) - 1
D_FF = 11008
PLE_DIM = 256
EPS = 1e-6

N_PROMPT = BATCH * SEQ
N_SAMPLE = DEC_BATCH * DEC_SEQ
N_TOK = N_PROMPT + N_SAMPLE
Z_W = 2 * W_A + W_B

F32 = jnp.float32
BF16 = jnp.bfloat16

VMEM_LIMIT_BYTES = 60 * 1024 * 1024

TM = 1088
TM_UP = 2176
TM_DOWN = 544
TM_EMBED = 256
TM_FINAL = 512
LANES = 128
TN = 512
TF = 256
TB = 256
CONV_CARRY = 8
POOL_CARRY = 16
SAMPLE_CB = 512


def _params(*sem):
    return pltpu.CompilerParams(dimension_semantics=sem, vmem_limit_bytes=VMEM_LIMIT_BYTES)


def _dot(a, b):
    return jnp.dot(a, b, preferred_element_type=F32)


def _snake(i, j, nj):
    return jnp.where(i % 2 == 0, j, nj - 1 - j)


def _row_map(i, j):
    return (i, 0)


def _tile_map(nj):
    return lambda i, j: (i, _snake(i, j, nj))


def _col_map(nj):
    return lambda i, j: (0, _snake(i, j, nj))


def _layer_col_map(l, nj):
    return lambda i, j: (l, 0, _snake(i, j, nj))


def _log_sigmoid(x):
    return jnp.minimum(x, 0.0) - jnp.log1p(jnp.exp(-jnp.abs(x)))


def _lru_coeffs(xc, r, i, logsig):
    log_a = r * (LRU_C * logsig)
    a = jnp.exp(log_a)
    mult = jnp.sqrt(1.0 - a * a)
    return a, mult, i * xc


def _lane_partial_sumsq(x):
    sq = x * x
    part = sq[:, 0:LANES]
    for c in range(1, x.shape[1] // LANES):
        part = part + sq[:, c * LANES:(c + 1) * LANES]
    return part


def _inv_rms_lanes(partial_sumsq):
    tot = jnp.sum(partial_sumsq, axis=-1, keepdims=True)
    return jnp.broadcast_to(lax.rsqrt(tot * (1.0 / D_MODEL) + EPS), partial_sumsq.shape)


def _scale_rows(acc, inv_ref):
    inv = inv_ref[...]
    return jnp.concatenate(
        [acc[:, c * LANES:(c + 1) * LANES] * inv for c in range(acc.shape[1] // LANES)], axis=1)


def _store_residual(x_new, g_ref, o_ref, xg_ref, inv_ref):
    j, last = pl.program_id(1), pl.num_programs(1) - 1
    o_ref[...] = x_new
    xg_ref[...] = (x_new * g_ref[...]).astype(BF16)
    part = _lane_partial_sumsq(x_new)

    @pl.when(j == 0)
    def _():
        inv_ref[...] = part

    @pl.when(jnp.logical_and(j > 0, j < last))
    def _():
        inv_ref[...] += part

    @pl.when(j == last)
    def _():
        inv_ref[...] = _inv_rms_lanes(inv_ref[...] + part)


def _residual_out(m, n, tm):
    specs = [
        pl.BlockSpec((tm, TN), _tile_map(n // TN)),
        pl.BlockSpec((tm, TN), _tile_map(n // TN)),
        pl.BlockSpec((tm, LANES), _row_map),
    ]
    shapes = [
        jax.ShapeDtypeStruct((m, n), F32),
        jax.ShapeDtypeStruct((m, n), BF16),
        jax.ShapeDtypeStruct((m, LANES), F32),
    ]
    return specs, shapes


def _embed_kernel(xp_ref, xs_ref, g_ref, x_ref, xg_ref, inv_ref):
    def emit(x):
        x_ref[...] = x
        xg_ref[...] = (x * g_ref[...]).astype(BF16)
        inv_ref[...] = _inv_rms_lanes(_lane_partial_sumsq(x))

    is_prompt = pl.program_id(0) < N_PROMPT // TM_EMBED

    @pl.when(is_prompt)
    def _():
        emit(xp_ref[...])

    @pl.when(jnp.logical_not(is_prompt))
    def _():
        emit(xs_ref[...])


def _embed(xp, xs, g):
    k = xp.shape[1]
    npb = N_PROMPT // TM_EMBED
    rows = lambda i: (i, 0)
    return pl.pallas_call(
        _embed_kernel,
        grid=(N_TOK // TM_EMBED,),
        in_specs=[
            pl.BlockSpec((TM_EMBED, k), lambda i: (jnp.minimum(i, npb - 1), 0)),
            pl.BlockSpec((TM_EMBED, k), lambda i: (jnp.maximum(i - npb, 0), 0)),
            pl.BlockSpec((1, k), lambda i: (0, 0)),
        ],
        out_specs=[
            pl.BlockSpec((TM_EMBED, k), rows),
            pl.BlockSpec((TM_EMBED, k), rows),
            pl.BlockSpec((TM_EMBED, LANES), rows),
        ],
        out_shape=[
            jax.ShapeDtypeStruct((N_TOK, k), F32),
            jax.ShapeDtypeStruct((N_TOK, k), BF16),
            jax.ShapeDtypeStruct((N_TOK, LANES), F32),
        ],
        compiler_params=_params("arbitrary"),
        name="embed",
    )(xp, xs, g)


def _final_norm_kernel(x_ref, inv_ref, g_ref, o_ref):
    o_ref[...] = _scale_rows(x_ref[...], inv_ref) * g_ref[...]


def _final_norm(x, inv, g, rows, row0):
    k = x.shape[1]
    tm = TM_FINAL
    assert rows % tm == 0 and row0 % tm == 0
    rb0 = row0 // tm
    return pl.pallas_call(
        _final_norm_kernel,
        grid=(rows // tm,),
        in_specs=[pl.BlockSpec((tm, k), lambda i: (rb0 + i, 0)),
                  pl.BlockSpec((tm, LANES), lambda i: (rb0 + i, 0)),
                  pl.BlockSpec((1, k), lambda i: (0, 0))],
        out_specs=pl.BlockSpec((tm, k), lambda i: (i, 0)),
        out_shape=jax.ShapeDtypeStruct((rows, k), F32),
        compiler_params=_params("parallel"),
        name="final_norm",
    )(x, inv, g)


def _in_proj_kernel(xg_ref, inv_ref, w_ref, o_ref):
    o_ref[...] = _scale_rows(_dot(xg_ref[...], w_ref[...].astype(BF16)), inv_ref)


def _in_proj(xg, inv, w_all, l):
    m, k = xg.shape
    n = w_all.shape[2]
    return pl.pallas_call(
        _in_proj_kernel,
        grid=(m // TM, n // TN),
        in_specs=[
            pl.BlockSpec((TM, k), _row_map),
            pl.BlockSpec((TM, LANES), _row_map),
            pl.BlockSpec((None, k, TN), _layer_col_map(l, n // TN)),
        ],
        out_specs=pl.BlockSpec((TM, TN), _tile_map(n // TN)),
        out_shape=jax.ShapeDtypeStruct((m, n), F32),
        compiler_params=_params("parallel", "arbitrary"),
        name="in_proj",
    )(xg, inv, w_all)


def _prompt_mixer_kernel(xa_ref, ga_ref, ub_ref, cw_ref, cb_ref, wr_ref, br_ref, wi_ref, bi_ref,
                         lam_ref, wp_ref, ps_ref,
                         y_ref, hl_ref, ct_ref, pt_ref,
                         hc_ref, cc_ref, pc_ref, a_ref, b_ref):
    t = pl.program_id(1)

    @pl.when(t == 0)
    def _():
        hc_ref[...] = jnp.zeros_like(hc_ref)
        cc_ref[...] = jnp.zeros_like(cc_ref)
        pc_ref[...] = jnp.zeros_like(pc_ref)

    pos = lax.broadcasted_iota(jnp.int32, (TB, 1), 0) + t * TB

    xa = xa_ref[...]
    ext = jnp.concatenate([cc_ref[...], xa], axis=0)
    xc = cb_ref[...] + cw_ref[CONV_W - 1:CONV_W, :] * xa
    for s in range(1, CONV_W):
        xc = xc + cw_ref[CONV_W - 1 - s:CONV_W - s, :] * pltpu.roll(ext, s, axis=0)[CONV_CARRY:]
    xcb = xc.astype(BF16)
    r_parts, i_parts = [], []
    for hd in range(N_LRU_HEADS):
        sl = slice(hd * LRU_HEAD, (hd + 1) * LRU_HEAD)
        r_parts.append(_dot(xcb[:, sl], wr_ref[hd]))
        i_parts.append(_dot(xcb[:, sl], wi_ref[hd]))
    r = jax.nn.sigmoid(jnp.concatenate(r_parts, axis=1) + br_ref[...])
    i = jax.nn.sigmoid(jnp.concatenate(i_parts, axis=1) + bi_ref[...])
    a, mult, ix = _lru_coeffs(xc, r, i, _log_sigmoid(lam_ref[...]))
    a_ref[...] = a
    b_ref[...] = mult * ix

    @pl.when(t == 0)
    def _():
        a_ref[0:1, :] = jnp.zeros((1, W_A), F32)
        b_ref[0:1, :] = ix[0:1, :]

    def step(row, h):
        h = a_ref[pl.ds(row, 1), :] * h + b_ref[pl.ds(row, 1), :]
        b_ref[pl.ds(row, 1), :] = h
        return h

    h_last = lax.fori_loop(0, TB, step, hc_ref[0:1, :], unroll=8)
    hc_ref[0:1, :] = h_last
    hl_ref[0] = h_last
    y_ref[:, 0:W_A] = (jax.nn.gelu(ga_ref[...]) * b_ref[...]).astype(BF16)
    cc_ref[...] = xa[TB - CONV_CARRY:]
    ct_ref[0] = xa[TB - CONV_CARRY:]

    ub = ub_ref[...]
    pext = jnp.concatenate([pc_ref[...], ub], axis=0)
    for g, w in enumerate(POOL_WINDOWS):
        sl = slice(g * POOL_GC, (g + 1) * POOL_GC)
        s = pext[:, sl]
        shift = 1
        while shift < w:
            s = s + pltpu.roll(s, shift, axis=0)
            shift *= 2
        cnt = jnp.minimum(pos + 1, w).astype(F32)
        d = s[POOL_CARRY:] / cnt - ub[:, sl]
        yb = _dot(d.astype(BF16), wp_ref[g]) * ps_ref[:, sl]
        y_ref[:, W_A + g * POOL_GC:W_A + (g + 1) * POOL_GC] = yb.astype(BF16)
    pc_ref[...] = ub[TB - POOL_CARRY:]
    pt_ref[0] = ub[TB - POOL_CARRY:]


def _prompt_mixer(z, cw, cb, wr, br, wi, bi, lam, wp, ps):
    nt = SEQ // TB
    row = lambda b, t: b * nt + t
    full2 = lambda shape: pl.BlockSpec(shape, lambda b, t: (0, 0))
    full3 = lambda shape: pl.BlockSpec(shape, lambda b, t: (0, 0, 0))
    return pl.pallas_call(
        _prompt_mixer_kernel,
        grid=(BATCH, nt),
        in_specs=[
            pl.BlockSpec((TB, W_A), lambda b, t: (row(b, t), 0)),
            pl.BlockSpec((TB, W_A), lambda b, t: (row(b, t), 1)),
            pl.BlockSpec((TB, W_B), lambda b, t: (row(b, t), 2)),
            full2((CONV_W, W_A)), full2((1, W_A)),
            full3((N_LRU_HEADS, LRU_HEAD, LRU_HEAD)), full2((1, W_A)),
            full3((N_LRU_HEADS, LRU_HEAD, LRU_HEAD)), full2((1, W_A)),
            full2((1, W_A)),
            full3((N_POOL_GROUPS, POOL_GC, POOL_GC)), full2((1, W_B)),
        ],
        out_specs=[
            pl.BlockSpec((TB, D_MODEL), lambda b, t: (row(b, t), 0)),
            pl.BlockSpec((1, 1, W_A), lambda b, t: (b, 0, 0)),
            pl.BlockSpec((1, CONV_CARRY, W_A), lambda b, t: (b, 0, 0)),
            pl.BlockSpec((1, POOL_CARRY, W_B), lambda b, t: (b, 0, 0)),
        ],
        out_shape=[
            jax.ShapeDtypeStruct((N_TOK, D_MODEL), BF16),
            jax.ShapeDtypeStruct((BATCH, 1, W_A), F32),
            jax.ShapeDtypeStruct((BATCH, CONV_CARRY, W_A), F32),
            jax.ShapeDtypeStruct((BATCH, POOL_CARRY, W_B), F32),
        ],
        scratch_shapes=[
            pltpu.VMEM((8, W_A), F32),
            pltpu.VMEM((CONV_CARRY, W_A), F32),
            pltpu.VMEM((POOL_CARRY, W_B), F32),
            pltpu.VMEM((TB, W_A), F32),
            pltpu.VMEM((TB, W_A), F32),
        ],
        compiler_params=_params("parallel", "arbitrary"),
        name="prompt_mixer",
    )(z, z, z, cw, cb, wr, br, wi, bi, lam, wp, ps)


def _sample_mixer_kernel(xa_ref, ga_ref, ub_ref, h0_ref, cs_ref, pst_ref,
                         cw_ref, cb_ref, wr_ref, br_ref, wi_ref, bi_ref, lam_ref, wp_ref, ps_ref,
                         y_in_ref, y_ref, hn_ref, cn_ref, pn_ref):
    del y_in_ref
    c = pl.program_id(0)
    heads = SAMPLE_CB // LRU_HEAD
    rows = lambda t: slice(t * DEC_BATCH, (t + 1) * DEC_BATCH)

    ext = [cs_ref[k] for k in range(CONV_W - 1)] + [xa_ref[rows(t), :] for t in range(DEC_SEQ)]
    logsig = _log_sigmoid(lam_ref[...])
    h = h0_ref[...]
    ya = []
    for t in range(DEC_SEQ):
        xc = cb_ref[...] + cw_ref[0:1, :] * ext[t]
        for k in range(1, CONV_W):
            xc = xc + cw_ref[k:k + 1, :] * ext[t + k]
        xcb = xc.astype(BF16)
        r_parts, i_parts = [], []
        for hd in range(heads):
            sl = slice(hd * LRU_HEAD, (hd + 1) * LRU_HEAD)
            r_parts.append(_dot(xcb[:, sl], wr_ref[hd]))
            i_parts.append(_dot(xcb[:, sl], wi_ref[hd]))
        r = jax.nn.sigmoid(jnp.concatenate(r_parts, axis=1) + br_ref[...])
        i = jax.nn.sigmoid(jnp.concatenate(i_parts, axis=1) + bi_ref[...])
        a, mult, ix = _lru_coeffs(xc, r, i, logsig)
        if PAST_LEN + t == 0:
            a, mult = jnp.zeros_like(a), jnp.ones_like(mult)
        h = a * h + mult * ix
        ya.append((jax.nn.gelu(ga_ref[rows(t), :]) * h).astype(BF16))
    hn_ref[...] = h
    for k in range(CONV_W - 1):
        cn_ref[k] = ext[DEC_SEQ + k]

    for k in range(POOL_PAD):
        src = k + DEC_SEQ
        pn_ref[k] = pst_ref[src] if src < POOL_PAD else ub_ref[rows(src - POOL_PAD), :]

    for g, w in enumerate(POOL_WINDOWS):
        @pl.when(c == g)
        def _(g=g, w=w):
            pext = ([pst_ref[k] for k in range(POOL_PAD - w + 1, POOL_PAD)]
                    + [ub_ref[rows(t), :] for t in range(DEC_SEQ)])
            for t in range(DEC_SEQ):
                s = pext[t]
                for j in range(1, w):
                    s = s + pext[t + j]
                cnt = float(min(PAST_LEN + t + 1, w))
                d = s / cnt - pext[t + w - 1]
                yb = _dot(d.astype(BF16), wp_ref[0]) * ps_ref[...]
                y_ref[rows(t), g * SAMPLE_CB:(g + 1) * SAMPLE_CB] = ya[t]
                y_ref[rows(t), W_A + g * POOL_GC:W_A + (g + 1) * POOL_GC] = yb.astype(BF16)


def _sample_mixer(z, y, h0, cs, pst, cw, cb, wr, br, wi, bi, lam, wp, ps):
    cbw = SAMPLE_CB
    na = W_A // cbw
    heads = cbw // LRU_HEAD
    rb = N_PROMPT // N_SAMPLE
    col = lambda off: (lambda c: (0, 0, off + c))
    vec = pl.BlockSpec((1, cbw), lambda c: (0, c))
    return pl.pallas_call(
        _sample_mixer_kernel,
        grid=(na,),
        in_specs=[
            pl.BlockSpec((N_SAMPLE, cbw), lambda c: (rb, c)),
            pl.BlockSpec((N_SAMPLE, cbw), lambda c: (rb, na + c)),
            pl.BlockSpec((N_SAMPLE, cbw), lambda c: (rb, 2 * na + c)),
            pl.BlockSpec((DEC_BATCH, cbw), lambda c: (0, c)),
            pl.BlockSpec((CONV_W - 1, DEC_BATCH, cbw), col(0)),
            pl.BlockSpec((POOL_PAD, DEC_BATCH, cbw), col(0)),
            pl.BlockSpec((CONV_W, cbw), lambda c: (0, c)), vec,
            pl.BlockSpec((heads, LRU_HEAD, LRU_HEAD), lambda c: (c, 0, 0)), vec,
            pl.BlockSpec((heads, LRU_HEAD, LRU_HEAD), lambda c: (c, 0, 0)), vec,
            vec,
            pl.BlockSpec((1, POOL_GC, POOL_GC), lambda c: (c, 0, 0)), vec,
            pl.BlockSpec(memory_space=pl.ANY),
        ],
        out_specs=[
            pl.BlockSpec((N_SAMPLE, D_MODEL), lambda c: (rb, 0)),
            pl.BlockSpec((DEC_BATCH, cbw), lambda c: (0, c)),
            pl.BlockSpec((CONV_W - 1, DEC_BATCH, cbw), col(0)),
            pl.BlockSpec((POOL_PAD, DEC_BATCH, cbw), col(0)),
        ],
        out_shape=[
            jax.ShapeDtypeStruct((N_TOK, D_MODEL), BF16),
            jax.ShapeDtypeStruct((DEC_BATCH, W_A), F32),
            jax.ShapeDtypeStruct((CONV_W - 1, DEC_BATCH, W_A), F32),
            jax.ShapeDtypeStruct((POOL_PAD, DEC_BATCH, W_B), F32),
        ],
        input_output_aliases={15: 0},
        compiler_params=_params("arbitrary"),
        name="sample_mixer",
    )(z, z, z, h0, cs, pst, cw, cb, wr, br, wi, bi, lam, wp, ps, y)


def _out_proj_kernel(y_ref, w_ref, x_ref, g_ref, o_ref, xg_ref, inv_ref):
    x_new = x_ref[...] + _dot(y_ref[...], w_ref[...].astype(BF16))
    _store_residual(x_new, g_ref, o_ref, xg_ref, inv_ref)


def _out_proj(y, w_all, l, x, g_next):
    m, k = y.shape
    n = w_all.shape[2]
    out_specs, out_shape = _residual_out(m, n, TM)
    return pl.pallas_call(
        _out_proj_kernel,
        grid=(m // TM, n // TN),
        in_specs=[
            pl.BlockSpec((TM, k), _row_map),
            pl.BlockSpec((None, k, TN), _layer_col_map(l, n // TN)),
            pl.BlockSpec((TM, TN), _tile_map(n // TN)),
            pl.BlockSpec((1, TN), _col_map(n // TN)),
        ],
        out_specs=out_specs,
        out_shape=out_shape,
        compiler_params=_params("parallel", "arbitrary"),
        name="out_proj",
    )(y, w_all, x, g_next)


def _ffn_up_kernel(xg_ref, inv_ref, wg_ref, wu_ref, wd_ref, h_ref, wdb_ref):
    xg = xg_ref[...]
    gate = _scale_rows(_dot(xg, wg_ref[...].astype(BF16)), inv_ref)
    up = _scale_rows(_dot(xg, wu_ref[...].astype(BF16)), inv_ref)
    h_ref[...] = (jax.nn.silu(gate) * up).astype(BF16)
    wdb_ref[...] = wd_ref[...].astype(BF16)


def _ffn_up(xg, inv, wg_all, wu_all, wd_all, l):
    m, k = xg.shape
    nf = D_FF // TF
    steps = (m // TM_UP) * nf
    wd_rows = D_FF // steps
    assert wd_rows * steps == D_FF and wd_rows % 16 == 0
    return pl.pallas_call(
        _ffn_up_kernel,
        grid=(m // TM_UP, nf),
        in_specs=[
            pl.BlockSpec((TM_UP, k), _row_map, pipeline_mode=pl.Buffered(1)),
            pl.BlockSpec((TM_UP, LANES), _row_map),
            pl.BlockSpec((None, k, TF), _layer_col_map(l, nf)),
            pl.BlockSpec((None, k, TF), _layer_col_map(l, nf)),
            pl.BlockSpec((None, wd_rows, k), lambda i, j: (l, i * nf + j, 0)),
        ],
        out_specs=[
            pl.BlockSpec((TM_UP, TF), _tile_map(nf)),
            pl.BlockSpec((wd_rows, k), lambda i, j: (i * nf + j, 0)),
        ],
        out_shape=[
            jax.ShapeDtypeStruct((m, D_FF), BF16),
            jax.ShapeDtypeStruct((D_FF, k), BF16),
        ],
        compiler_params=_params("arbitrary", "arbitrary"),
        name="ffn_up",
    )(xg, inv, wg_all, wu_all, wd_all)


def _ffn_down_kernel(h_ref, w_ref, x_ref, g_ref, o_ref, xg_ref, inv_ref):
    x_new = x_ref[...] + _dot(h_ref[...], w_ref[...])
    _store_residual(x_new, g_ref, o_ref, xg_ref, inv_ref)


def _ffn_down(h, wdb, x, g_next):
    m, kf = h.shape
    n = wdb.shape[1]
    out_specs, out_shape = _residual_out(m, n, TM_DOWN)
    return pl.pallas_call(
        _ffn_down_kernel,
        grid=(m // TM_DOWN, n // TN),
        in_specs=[
            pl.BlockSpec((TM_DOWN, kf), _row_map),
            pl.BlockSpec((kf, TN), _col_map(n // TN)),
            pl.BlockSpec((TM_DOWN, TN), _tile_map(n // TN)),
            pl.BlockSpec((1, TN), _col_map(n // TN)),
        ],
        out_specs=out_specs,
        out_shape=out_shape,
        compiler_params=_params("parallel", "arbitrary"),
        name="ffn_down",
    )(h, wdb, x, g_next)


def _ple_kernel(xg_ref, inv_ref, wg_ref, p_ref, wp_ref, x_ref, g_ref, o_ref, xgn_ref, invn_ref):
    gate = jax.nn.sigmoid(_scale_rows(_dot(xg_ref[...], wg_ref[...].astype(BF16)), inv_ref))
    proj = _dot(p_ref[...], wp_ref[...].astype(BF16))
    x_new = x_ref[...] + gate * proj
    _store_residual(x_new, g_ref, o_ref, xgn_ref, invn_ref)


def _ple(xg, inv, wg_all, p, wp_all, l, x, g_next):
    m, k = xg.shape
    out_specs, out_shape = _residual_out(m, k, TM)
    return pl.pallas_call(
        _ple_kernel,
        grid=(m // TM, k // TN),
        in_specs=[
            pl.BlockSpec((TM, k), _row_map),
            pl.BlockSpec((TM, LANES), _row_map),
            pl.BlockSpec((None, k, TN), _layer_col_map(l, k // TN)),
            pl.BlockSpec((TM, PLE_DIM), _row_map),
            pl.BlockSpec((None, PLE_DIM, TN), _layer_col_map(l, k // TN)),
            pl.BlockSpec((TM, TN), _tile_map(k // TN)),
            pl.BlockSpec((1, TN), _col_map(k // TN)),
        ],
        out_specs=out_specs,
        out_shape=out_shape,
        compiler_params=_params("parallel", "arbitrary"),
        name="ple",
    )(xg, inv, wg_all, p, wp_all, x, g_next)


def _time_major(a):
    return jnp.swapaxes(a, 0, 1)


def kernel(x_prompt, x_sample, state_h, state_conv, state_pool, p_prompt, p_sample, g_mix, w_in, conv_w, conv_b, w_rg, b_rg, w_ig, b_ig, lam, w_pool, pool_scale, w_out, g_ffn, w_gate, w_up, w_down, g_pe, w_pe_gate, w_pe_proj, g_final):
    row = lambda v: v.reshape(1, -1)
    x, xg, inv = _embed(x_prompt.reshape(N_PROMPT, D_MODEL),
                       _time_major(x_sample).reshape(N_SAMPLE, D_MODEL), row(g_mix[0]))
    hp_l, cp_l, pp_l, hs_l, cs_l, ps_l = [], [], [], [], [], []
    for l in range(DEPTH):
        cw, cb = conv_w[l], row(conv_b[l])
        wr, br = w_rg[l].astype(BF16), row(b_rg[l])
        wi, bi = w_ig[l].astype(BF16), row(b_ig[l])
        lm, wp, ps = row(lam[l]), w_pool[l].astype(BF16), row(pool_scale[l])

        g_after_ple = row(g_mix[l + 1]) if l + 1 < DEPTH else row(g_final)

        z = _in_proj(xg, inv, w_in, l)

        y, hp, ct, pt = _prompt_mixer(z, cw, cb, wr, br, wi, bi, lm, wp, ps)
        y, hs, cs, pst = _sample_mixer(
            z, y, state_h[l], _time_major(state_conv[l]), _time_major(state_pool[l]),
            cw, cb, wr, br, wi, bi, lm, wp, ps)

        x, xg, inv = _out_proj(y, w_out, l, x, row(g_ffn[l]))
        h, wdb = _ffn_up(xg, inv, w_gate, w_up, w_down, l)
        x, xg, inv = _ffn_down(h, wdb, x, row(g_pe[l]))
        p = jnp.concatenate([p_prompt[l].reshape(N_PROMPT, PLE_DIM),
                             _time_major(p_sample[l]).reshape(N_SAMPLE, PLE_DIM)], axis=0).astype(BF16)
        x, xg, inv = _ple(xg, inv, w_pe_gate, p, w_pe_proj, l, x, g_after_ple)

        hp_l.append(hp.reshape(BATCH, W_A))
        cp_l.append(ct[:, CONV_CARRY - (CONV_W - 1):])
        pp_l.append(pt[:, POOL_CARRY - POOL_PAD:])
        hs_l.append(hs)
        cs_l.append(_time_major(cs))
        ps_l.append(_time_major(pst))

    gf = row(g_final)
    y_prompt = _final_norm(x, inv, gf, N_PROMPT, 0).reshape(BATCH, SEQ, D_MODEL)
    y_s = _final_norm(x, inv, gf, N_SAMPLE, N_PROMPT)
    y_sample = _time_major(y_s.reshape(DEC_SEQ, DEC_BATCH, D_MODEL))
    return (y_prompt, y_sample,
            jnp.stack(hp_l), jnp.stack(cp_l), jnp.stack(pp_l),
            jnp.stack(hs_l), jnp.stack(cs_l), jnp.stack(ps_l))
```

```python
import functools

import jax
import jax.numpy as jnp
from jax import lax
from jax.experimental import pallas as pl
from jax.experimental.pallas import tpu as pltpu

D_MODEL = 4096
BATCH = 4
SEQ = 2048
DEPTH = 2
DEC_BATCH = 128
DEC_SEQ = 4
PAST_LEN = 16384
W_A = D_MODEL // 2
W_B = D_MODEL - W_A
LRU_HEAD = 256
N_LRU_HEADS = W_A // LRU_HEAD
LRU_C = 8.0
CONV_W = 4
POOL_WINDOWS = (2, 4, 8, 16)
N_POOL_GROUPS = len(POOL_WINDOWS)
POOL_GC = W_B // N_POOL_GROUPS
POOL_PAD = max(POOL_WINDOWS) - 1
D_FF = 11008
PLE_DIM = 256
EPS = 1e-6

N_PROMPT = BATCH * SEQ
N_SAMPLE = DEC_BATCH * DEC_SEQ
N_TOK = N_PROMPT + N_SAMPLE
Z_W = 2 * W_A + W_B

F32 = jnp.float32
BF16 = jnp.bfloat16

VMEM_LIMIT_BYTES = 60 * 1024 * 1024

TM = 1088
TM_UP = 2176
UP_CHUNK = 544
TM_DOWN = 544
TM_EMBED = 256
TM_FINAL = 512
LANES = 128
TN = 512
TF = 256
TB = 256
CONV_CARRY = 8
POOL_CARRY = 16
SAMPLE_CB = 512


def _params(*sem):
    return pltpu.CompilerParams(dimension_semantics=sem, vmem_limit_bytes=VMEM_LIMIT_BYTES)


def _dot(a, b):
    return jnp.dot(a, b, preferred_element_type=F32)


def _snake(i, j, nj):
    return jnp.where(i % 2 == 0, j, nj - 1 - j)


def _row_map(i, j):
    return (i, 0)


def _tile_map(nj):
    return lambda i, j: (i, _snake(i, j, nj))


def _col_map(nj):
    return lambda i, j: (0, _snake(i, j, nj))


def _layer_col_map(l, nj):
    return lambda i, j: (l, 0, _snake(i, j, nj))


def _log_sigmoid(x):
    return jnp.minimum(x, 0.0) - jnp.log1p(jnp.exp(-jnp.abs(x)))


def _lru_coeffs(xc, r, i, logsig):
    log_a = r * (LRU_C * logsig)
    a = jnp.exp(log_a)
    mult = jnp.sqrt(1.0 - a * a)
    return a, mult, i * xc


def _lane_partial_sumsq(x):
    sq = x * x
    part = sq[:, 0:LANES]
    for c in range(1, x.shape[1] // LANES):
        part = part + sq[:, c * LANES:(c + 1) * LANES]
    return part


def _inv_rms_lanes(partial_sumsq):
    tot = jnp.sum(partial_sumsq, axis=-1, keepdims=True)
    return jnp.broadcast_to(lax.rsqrt(tot * (1.0 / D_MODEL) + EPS), partial_sumsq.shape)


def _scale_rows(acc, inv_ref):
    inv = inv_ref[...]
    return jnp.concatenate(
        [acc[:, c * LANES:(c + 1) * LANES] * inv for c in range(acc.shape[1] // LANES)], axis=1)


def _store_residual(x_new, g_ref, o_ref, xg_ref, inv_ref):
    j, last = pl.program_id(1), pl.num_programs(1) - 1
    o_ref[...] = x_new
    xg_ref[...] = (x_new * g_ref[...]).astype(BF16)
    part = _lane_partial_sumsq(x_new)

    @pl.when(j == 0)
    def _():
        inv_ref[...] = part

    @pl.when(jnp.logical_and(j > 0, j < last))
    def _():
        inv_ref[...] += part

    @pl.when(j == last)
    def _():
        inv_ref[...] = _inv_rms_lanes(inv_ref[...] + part)


def _residual_out(m, n, tm):
    specs = [
        pl.BlockSpec((tm, TN), _tile_map(n // TN)),
        pl.BlockSpec((tm, TN), _tile_map(n // TN)),
        pl.BlockSpec((tm, LANES), _row_map),
    ]
    shapes = [
        jax.ShapeDtypeStruct((m, n), F32),
        jax.ShapeDtypeStruct((m, n), BF16),
        jax.ShapeDtypeStruct((m, LANES), F32),
    ]
    return specs, shapes


def _embed_kernel(xp_ref, xs_ref, g_ref, x_ref, xg_ref, inv_ref):
    def emit(x):
        x_ref[...] = x
        xg_ref[...] = (x * g_ref[...]).astype(BF16)
        inv_ref[...] = _inv_rms_lanes(_lane_partial_sumsq(x))

    is_prompt = pl.program_id(0) < N_PROMPT // TM_EMBED

    @pl.when(is_prompt)
    def _():
        emit(xp_ref[...])

    @pl.when(jnp.logical_not(is_prompt))
    def _():
        emit(xs_ref[...])


def _embed(xp, xs, g):
    k = xp.shape[1]
    npb = N_PROMPT // TM_EMBED
    rows = lambda i: (i, 0)
    return pl.pallas_call(
        _embed_kernel,
        grid=(N_TOK // TM_EMBED,),
        in_specs=[
            pl.BlockSpec((TM_EMBED, k), lambda i: (jnp.minimum(i, npb - 1), 0)),
            pl.BlockSpec((TM_EMBED, k), lambda i: (jnp.maximum(i - npb, 0), 0)),
            pl.BlockSpec((1, k), lambda i: (0, 0)),
        ],
        out_specs=[
            pl.BlockSpec((TM_EMBED, k), rows),
            pl.BlockSpec((TM_EMBED, k), rows),
            pl.BlockSpec((TM_EMBED, LANES), rows),
        ],
        out_shape=[
            jax.ShapeDtypeStruct((N_TOK, k), F32),
            jax.ShapeDtypeStruct((N_TOK, k), BF16),
            jax.ShapeDtypeStruct((N_TOK, LANES), F32),
        ],
        compiler_params=_params("arbitrary"),
        name="embed",
    )(xp, xs, g)


def _final_norm_kernel(x_ref, inv_ref, g_ref, o_ref):
    o_ref[...] = _scale_rows(x_ref[...], inv_ref) * g_ref[...]


def _final_norm(x, inv, g, rows, row0):
    k = x.shape[1]
    tm = TM_FINAL
    assert rows % tm == 0 and row0 % tm == 0
    rb0 = row0 // tm
    return pl.pallas_call(
        _final_norm_kernel,
        grid=(rows // tm,),
        in_specs=[pl.BlockSpec((tm, k), lambda i: (rb0 + i, 0)),
                  pl.BlockSpec((tm, LANES), lambda i: (rb0 + i, 0)),
                  pl.BlockSpec((1, k), lambda i: (0, 0))],
        out_specs=pl.BlockSpec((tm, k), lambda i: (i, 0)),
        out_shape=jax.ShapeDtypeStruct((rows, k), F32),
        compiler_params=_params("parallel"),
        name="final_norm",
    )(x, inv, g)


def _in_proj_kernel(xg_ref, inv_ref, w_ref, o_ref):
    o_ref[...] = _scale_rows(_dot(xg_ref[...], w_ref[...].astype(BF16)), inv_ref)


def _in_proj(xg, inv, w_all, l):
    m, k = xg.shape
    n = w_all.shape[2]
    return pl.pallas_call(
        _in_proj_kernel,
        grid=(m // TM, n // TN),
        in_specs=[
            pl.BlockSpec((TM, k), _row_map),
            pl.BlockSpec((TM, LANES), _row_map),
            pl.BlockSpec((None, k, TN), _layer_col_map(l, n // TN)),
        ],
        out_specs=pl.BlockSpec((TM, TN), _tile_map(n // TN)),
        out_shape=jax.ShapeDtypeStruct((m, n), F32),
        compiler_params=_params("parallel", "arbitrary"),
        name="in_proj",
    )(xg, inv, w_all)


def _prompt_mixer_kernel(xa_ref, ga_ref, ub_ref, cw_ref, cb_ref, wr_ref, br_ref, wi_ref, bi_ref,
                         lam_ref, wp_ref, ps_ref,
                         y_ref, hl_ref, ct_ref, pt_ref,
                         hc_ref, cc_ref, pc_ref, a_ref, b_ref):
    t = pl.program_id(1)

    @pl.when(t == 0)
    def _():
        hc_ref[...] = jnp.zeros_like(hc_ref)
        cc_ref[...] = jnp.zeros_like(cc_ref)
        pc_ref[...] = jnp.zeros_like(pc_ref)

    pos = lax.broadcasted_iota(jnp.int32, (TB, 1), 0) + t * TB

    xa = xa_ref[...]
    ext = jnp.concatenate([cc_ref[...], xa], axis=0)
    xc = cb_ref[...] + cw_ref[CONV_W - 1:CONV_W, :] * xa
    for s in range(1, CONV_W):
        xc = xc + cw_ref[CONV_W - 1 - s:CONV_W - s, :] * pltpu.roll(ext, s, axis=0)[CONV_CARRY:]
    xcb = xc.astype(BF16)
    r_parts, i_parts = [], []
    for hd in range(N_LRU_HEADS):
        sl = slice(hd * LRU_HEAD, (hd + 1) * LRU_HEAD)
        r_parts.append(_dot(xcb[:, sl], wr_ref[hd]))
        i_parts.append(_dot(xcb[:, sl], wi_ref[hd]))
    r = jax.nn.sigmoid(jnp.concatenate(r_parts, axis=1) + br_ref[...])
    i = jax.nn.sigmoid(jnp.concatenate(i_parts, axis=1) + bi_ref[...])
    a, mult, ix = _lru_coeffs(xc, r, i, _log_sigmoid(lam_ref[...]))
    a_ref[...] = a
    b_ref[...] = mult * ix

    @pl.when(t == 0)
    def _():
        a_ref[0:1, :] = jnp.zeros((1, W_A), F32)
        b_ref[0:1, :] = ix[0:1, :]

    def step(row, h):
        h = a_ref[pl.ds(row, 1), :] * h + b_ref[pl.ds(row, 1), :]
        b_ref[pl.ds(row, 1), :] = h
        return h

    h_last = lax.fori_loop(0, TB, step, hc_ref[0:1, :], unroll=8)
    hc_ref[0:1, :] = h_last
    hl_ref[0] = h_last
    y_ref[:, 0:W_A] = (jax.nn.gelu(ga_ref[...]) * b_ref[...]).astype(BF16)
    cc_ref[...] = xa[TB - CONV_CARRY:]
    ct_ref[0] = xa[TB - CONV_CARRY:]

    ub = ub_ref[...]
    pext = jnp.concatenate([pc_ref[...], ub], axis=0)
    for g, w in enumerate(POOL_WINDOWS):
        sl = slice(g * POOL_GC, (g + 1) * POOL_GC)
        s = pext[:, sl]
        shift = 1
        while shift < w:
            s = s + pltpu.roll(s, shift, axis=0)
            shift *= 2
        cnt = jnp.minimum(pos + 1, w).astype(F32)
        d = s[POOL_CARRY:] / cnt - ub[:, sl]
        yb = _dot(d.astype(BF16), wp_ref[g]) * ps_ref[:, sl]
        y_ref[:, W_A + g * POOL_GC:W_A + (g + 1) * POOL_GC] = yb.astype(BF16)
    pc_ref[...] = ub[TB - POOL_CARRY:]
    pt_ref[0] = ub[TB - POOL_CARRY:]


def _prompt_mixer(z, cw, cb, wr, br, wi, bi, lam, wp, ps):
    nt = SEQ // TB
    row = lambda b, t: b * nt + t
    full2 = lambda shape: pl.BlockSpec(shape, lambda b, t: (0, 0))
    full3 = lambda shape: pl.BlockSpec(shape, lambda b, t: (0, 0, 0))
    return pl.pallas_call(
        _prompt_mixer_kernel,
        grid=(BATCH, nt),
        in_specs=[
            pl.BlockSpec((TB, W_A), lambda b, t: (row(b, t), 0)),
            pl.BlockSpec((TB, W_A), lambda b, t: (row(b, t), 1)),
            pl.BlockSpec((TB, W_B), lambda b, t: (row(b, t), 2)),
            full2((CONV_W, W_A)), full2((1, W_A)),
            full3((N_LRU_HEADS, LRU_HEAD, LRU_HEAD)), full2((1, W_A)),
            full3((N_LRU_HEADS, LRU_HEAD, LRU_HEAD)), full2((1, W_A)),
            full2((1, W_A)),
            full3((N_POOL_GROUPS, POOL_GC, POOL_GC)), full2((1, W_B)),
        ],
        out_specs=[
            pl.BlockSpec((TB, D_MODEL), lambda b, t: (row(b, t), 0)),
            pl.BlockSpec((1, 1, W_A), lambda b, t: (b, 0, 0)),
            pl.BlockSpec((1, CONV_CARRY, W_A), lambda b, t: (b, 0, 0)),
            pl.BlockSpec((1, POOL_CARRY, W_B), lambda b, t: (b, 0, 0)),
        ],
        out_shape=[
            jax.ShapeDtypeStruct((N_TOK, D_MODEL), BF16),
            jax.ShapeDtypeStruct((BATCH, 1, W_A), F32),
            jax.ShapeDtypeStruct((BATCH, CONV_CARRY, W_A), F32),
            jax.ShapeDtypeStruct((BATCH, POOL_CARRY, W_B), F32),
        ],
        scratch_shapes=[
            pltpu.VMEM((8, W_A), F32),
            pltpu.VMEM((CONV_CARRY, W_A), F32),
            pltpu.VMEM((POOL_CARRY, W_B), F32),
            pltpu.VMEM((TB, W_A), F32),
            pltpu.VMEM((TB, W_A), F32),
        ],
        compiler_params=_params("parallel", "arbitrary"),
        name="prompt_mixer",
    )(z, z, z, cw, cb, wr, br, wi, bi, lam, wp, ps)


def _sample_mixer_kernel(xa_ref, ga_ref, ub_ref, h0_ref, cs_ref, pst_ref,
                         cw_ref, cb_ref, wr_ref, br_ref, wi_ref, bi_ref, lam_ref, wp_ref, ps_ref,
                         y_in_ref, y_ref, hn_ref, cn_ref, pn_ref):
    del y_in_ref
    c = pl.program_id(0)
    heads = SAMPLE_CB // LRU_HEAD
    rows = lambda t: slice(t * DEC_BATCH, (t + 1) * DEC_BATCH)

    ext = [cs_ref[k] for k in range(CONV_W - 1)] + [xa_ref[rows(t), :] for t in range(DEC_SEQ)]
    logsig = _log_sigmoid(lam_ref[...])
    h = h0_ref[...]
    ya = []
    for t in range(DEC_SEQ):
        xc = cb_ref[...] + cw_ref[0:1, :] * ext[t]
        for k in range(1, CONV_W):
            xc = xc + cw_ref[k:k + 1, :] * ext[t + k]
        xcb = xc.astype(BF16)
        r_parts, i_parts = [], []
        for hd in range(heads):
            sl = slice(hd * LRU_HEAD, (hd + 1) * LRU_HEAD)
            r_parts.append(_dot(xcb[:, sl], wr_ref[hd]))
            i_parts.append(_dot(xcb[:, sl], wi_ref[hd]))
        r = jax.nn.sigmoid(jnp.concatenate(r_parts, axis=1) + br_ref[...])
        i = jax.nn.sigmoid(jnp.concatenate(i_parts, axis=1) + bi_ref[...])
        a, mult, ix = _lru_coeffs(xc, r, i, logsig)
        if PAST_LEN + t == 0:
            a, mult = jnp.zeros_like(a), jnp.ones_like(mult)
        h = a * h + mult * ix
        ya.append((jax.nn.gelu(ga_ref[rows(t), :]) * h).astype(BF16))
    hn_ref[...] = h
    for k in range(CONV_W - 1):
        cn_ref[k] = ext[DEC_SEQ + k]

    for k in range(POOL_PAD):
        src = k + DEC_SEQ
        pn_ref[k] = pst_ref[src] if src < POOL_PAD else ub_ref[rows(src - POOL_PAD), :]

    for g, w in enumerate(POOL_WINDOWS):
        @pl.when(c == g)
        def _(g=g, w=w):
            pext = ([pst_ref[k] for k in range(POOL_PAD - w + 1, POOL_PAD)]
                    + [ub_ref[rows(t), :] for t in range(DEC_SEQ)])
            for t in range(DEC_SEQ):
                s = pext[t]
                for j in range(1, w):
                    s = s + pext[t + j]
                cnt = float(min(PAST_LEN + t + 1, w))
                d = s / cnt - pext[t + w - 1]
                yb = _dot(d.astype(BF16), wp_ref[0]) * ps_ref[...]
                y_ref[rows(t), g * SAMPLE_CB:(g + 1) * SAMPLE_CB] = ya[t]
                y_ref[rows(t), W_A + g * POOL_GC:W_A + (g + 1) * POOL_GC] = yb.astype(BF16)


def _sample_mixer(z, y, h0, cs, pst, cw, cb, wr, br, wi, bi, lam, wp, ps):
    cbw = SAMPLE_CB
    na = W_A // cbw
    heads = cbw // LRU_HEAD
    rb = N_PROMPT // N_SAMPLE
    col = lambda off: (lambda c: (0, 0, off + c))
    vec = pl.BlockSpec((1, cbw), lambda c: (0, c))
    return pl.pallas_call(
        _sample_mixer_kernel,
        grid=(na,),
        in_specs=[
            pl.BlockSpec((N_SAMPLE, cbw), lambda c: (rb, c)),
            pl.BlockSpec((N_SAMPLE, cbw), lambda c: (rb, na + c)),
            pl.BlockSpec((N_SAMPLE, cbw), lambda c: (rb, 2 * na + c)),
            pl.BlockSpec((DEC_BATCH, cbw), lambda c: (0, c)),
            pl.BlockSpec((CONV_W - 1, DEC_BATCH, cbw), col(0)),
            pl.BlockSpec((POOL_PAD, DEC_BATCH, cbw), col(0)),
            pl.BlockSpec((CONV_W, cbw), lambda c: (0, c)), vec,
            pl.BlockSpec((heads, LRU_HEAD, LRU_HEAD), lambda c: (c, 0, 0)), vec,
            pl.BlockSpec((heads, LRU_HEAD, LRU_HEAD), lambda c: (c, 0, 0)), vec,
            vec,
            pl.BlockSpec((1, POOL_GC, POOL_GC), lambda c: (c, 0, 0)), vec,
            pl.BlockSpec(memory_space=pl.ANY),
        ],
        out_specs=[
            pl.BlockSpec((N_SAMPLE, D_MODEL), lambda c: (rb, 0)),
            pl.BlockSpec((DEC_BATCH, cbw), lambda c: (0, c)),
            pl.BlockSpec((CONV_W - 1, DEC_BATCH, cbw), col(0)),
            pl.BlockSpec((POOL_PAD, DEC_BATCH, cbw), col(0)),
        ],
        out_shape=[
            jax.ShapeDtypeStruct((N_TOK, D_MODEL), BF16),
            jax.ShapeDtypeStruct((DEC_BATCH, W_A), F32),
            jax.ShapeDtypeStruct((CONV_W - 1, DEC_BATCH, W_A), F32),
            jax.ShapeDtypeStruct((POOL_PAD, DEC_BATCH, W_B), F32),
        ],
        input_output_aliases={15: 0},
        compiler_params=_params("arbitrary"),
        name="sample_mixer",
    )(z, z, z, h0, cs, pst, cw, cb, wr, br, wi, bi, lam, wp, ps, y)


def _out_proj_kernel(y_ref, w_ref, x_ref, g_ref, o_ref, xg_ref, inv_ref):
    x_new = x_ref[...] + _dot(y_ref[...], w_ref[...].astype(BF16))
    _store_residual(x_new, g_ref, o_ref, xg_ref, inv_ref)


def _out_proj(y, w_all, l, x, g_next):
    m, k = y.shape
    n = w_all.shape[2]
    out_specs, out_shape = _residual_out(m, n, TM)
    return pl.pallas_call(
        _out_proj_kernel,
        grid=(m // TM, n // TN),
        in_specs=[
            pl.BlockSpec((TM, k), _row_map),
            pl.BlockSpec((None, k, TN), _layer_col_map(l, n // TN)),
            pl.BlockSpec((TM, TN), _tile_map(n // TN)),
            pl.BlockSpec((1, TN), _col_map(n // TN)),
        ],
        out_specs=out_specs,
        out_shape=out_shape,
        compiler_params=_params("parallel", "arbitrary"),
        name="out_proj",
    )(y, w_all, x, g_next)


def _ffn_up_kernel(xg_ref, inv_ref, wg_ref, wu_ref, wd_ref, h_ref, wdb_ref):
    wg = wg_ref[...].astype(BF16)
    wu = wu_ref[...].astype(BF16)
    for r0 in range(0, TM_UP, UP_CHUNK):
        rows = slice(r0, r0 + UP_CHUNK)
        xg = xg_ref[rows, :]
        inv = inv_ref.at[rows, :]
        gate = _scale_rows(_dot(xg, wg), inv)
        up = _scale_rows(_dot(xg, wu), inv)
        h_ref[rows, :] = (jax.nn.silu(gate) * up).astype(BF16)
    wdb_ref[...] = wd_ref[...].astype(BF16)


def _ffn_up(xg, inv, wg_all, wu_all, wd_all, l):
    m, k = xg.shape
    nf = D_FF // TF
    steps = (m // TM_UP) * nf
    wd_rows = D_FF // steps
    assert wd_rows * steps == D_FF and wd_rows % 16 == 0
    return pl.pallas_call(
        _ffn_up_kernel,
        grid=(m // TM_UP, nf),
        in_specs=[
            pl.BlockSpec((TM_UP, k), _row_map, pipeline_mode=pl.Buffered(1)),
            pl.BlockSpec((TM_UP, LANES), _row_map),
            pl.BlockSpec((None, k, TF), _layer_col_map(l, nf)),
            pl.BlockSpec((None, k, TF), _layer_col_map(l, nf)),
            pl.BlockSpec((None, wd_rows, k), lambda i, j: (l, i * nf + j, 0)),
        ],
        out_specs=[
            pl.BlockSpec((TM_UP, TF), _tile_map(nf)),
            pl.BlockSpec((wd_rows, k), lambda i, j: (i * nf + j, 0)),
        ],
        out_shape=[
            jax.ShapeDtypeStruct((m, D_FF), BF16),
            jax.ShapeDtypeStruct((D_FF, k), BF16),
        ],
        compiler_params=_params("arbitrary", "arbitrary"),
        name="ffn_up",
    )(xg, inv, wg_all, wu_all, wd_all)


def _ffn_down_kernel(h_ref, w_ref, x_ref, g_ref, o_ref, xg_ref, inv_ref):
    x_new = x_ref[...] + _dot(h_ref[...], w_ref[...])
    _store_residual(x_new, g_ref, o_ref, xg_ref, inv_ref)


def _ffn_down(h, wdb, x, g_next):
    m, kf = h.shape
    n = wdb.shape[1]
    out_specs, out_shape = _residual_out(m, n, TM_DOWN)
    return pl.pallas_call(
        _ffn_down_kernel,
        grid=(m // TM_DOWN, n // TN),
        in_specs=[
            pl.BlockSpec((TM_DOWN, kf), _row_map),
            pl.BlockSpec((kf, TN), _col_map(n // TN)),
            pl.BlockSpec((TM_DOWN, TN), _tile_map(n // TN)),
            pl.BlockSpec((1, TN), _col_map(n // TN)),
        ],
        out_specs=out_specs,
        out_shape=out_shape,
        compiler_params=_params("parallel", "arbitrary"),
        name="ffn_down",
    )(h, wdb, x, g_next)


def _ple_kernel(xg_ref, inv_ref, wg_ref, p_ref, wp_ref, x_ref, g_ref, o_ref, xgn_ref, invn_ref):
    gate = jax.nn.sigmoid(_scale_rows(_dot(xg_ref[...], wg_ref[...].astype(BF16)), inv_ref))
    proj = _dot(p_ref[...], wp_ref[...].astype(BF16))
    x_new = x_ref[...] + gate * proj
    _store_residual(x_new, g_ref, o_ref, xgn_ref, invn_ref)


def _ple(xg, inv, wg_all, p, wp_all, l, x, g_next):
    m, k = xg.shape
    out_specs, out_shape = _residual_out(m, k, TM)
    return pl.pallas_call(
        _ple_kernel,
        grid=(m // TM, k // TN),
        in_specs=[
            pl.BlockSpec((TM, k), _row_map),
            pl.BlockSpec((TM, LANES), _row_map),
            pl.BlockSpec((None, k, TN), _layer_col_map(l, k // TN)),
            pl.BlockSpec((TM, PLE_DIM), _row_map),
            pl.BlockSpec((None, PLE_DIM, TN), _layer_col_map(l, k // TN)),
            pl.BlockSpec((TM, TN), _tile_map(k // TN)),
            pl.BlockSpec((1, TN), _col_map(k // TN)),
        ],
        out_specs=out_specs,
        out_shape=out_shape,
        compiler_params=_params("parallel", "arbitrary"),
        name="ple",
    )(xg, inv, wg_all, p, wp_all, x, g_next)


def _time_major(a):
    return jnp.swapaxes(a, 0, 1)


def kernel(x_prompt, x_sample, state_h, state_conv, state_pool, p_prompt, p_sample, g_mix, w_in, conv_w, conv_b, w_rg, b_rg, w_ig, b_ig, lam, w_pool, pool_scale, w_out, g_ffn, w_gate, w_up, w_down, g_pe, w_pe_gate, w_pe_proj, g_final):
    row = lambda v: v.reshape(1, -1)
    x, xg, inv = _embed(x_prompt.reshape(N_PROMPT, D_MODEL),
                       _time_major(x_sample).reshape(N_SAMPLE, D_MODEL), row(g_mix[0]))
    hp_l, cp_l, pp_l, hs_l, cs_l, ps_l = [], [], [], [], [], []
    for l in range(DEPTH):
        cw, cb = conv_w[l], row(conv_b[l])
        wr, br = w_rg[l].astype(BF16), row(b_rg[l])
        wi, bi = w_ig[l].astype(BF16), row(b_ig[l])
        lm, wp, ps = row(lam[l]), w_pool[l].astype(BF16), row(pool_scale[l])

        g_after_ple = row(g_mix[l + 1]) if l + 1 < DEPTH else row(g_final)

        z = _in_proj(xg, inv, w_in, l)

        y, hp, ct, pt = _prompt_mixer(z, cw, cb, wr, br, wi, bi, lm, wp, ps)
        y, hs, cs, pst = _sample_mixer(
            z, y, state_h[l], _time_major(state_conv[l]), _time_major(state_pool[l]),
            cw, cb, wr, br, wi, bi, lm, wp, ps)

        x, xg, inv = _out_proj(y, w_out, l, x, row(g_ffn[l]))
        h, wdb = _ffn_up(xg, inv, w_gate, w_up, w_down, l)
        x, xg, inv = _ffn_down(h, wdb, x, row(g_pe[l]))
        p = jnp.concatenate([p_prompt[l].reshape(N_PROMPT, PLE_DIM),
                             _time_major(p_sample[l]).reshape(N_SAMPLE, PLE_DIM)], axis=0).astype(BF16)
        x, xg, inv = _ple(xg, inv, w_pe_gate, p, w_pe_proj, l, x, g_after_ple)

        hp_l.append(hp.reshape(BATCH, W_A))
        cp_l.append(ct[:, CONV_CARRY - (CONV_W - 1):])
        pp_l.append(pt[:, POOL_CARRY - POOL_PAD:])
        hs_l.append(hs)
        cs_l.append(_time_major(cs))
        ps_l.append(_time_major(pst))

    gf = row(g_final)
    y_prompt = _final_norm(x, inv, gf, N_PROMPT, 0).reshape(BATCH, SEQ, D_MODEL)
    y_s = _final_norm(x, inv, gf, N_SAMPLE, N_PROMPT)
    y_sample = _time_major(y_s.reshape(DEC_SEQ, DEC_BATCH, D_MODEL))
    return (y_prompt, y_sample,
            jnp.stack(hp_l), jnp.stack(cp_l), jnp.stack(pp_l),
            jnp.stack(hs_l), jnp.stack(cs_l), jnp.stack(ps_l))
```

```python
import functools

import jax
import jax.numpy as jnp
from jax import lax
from jax.experimental import pallas as pl
from jax.experimental.pallas import tpu as pltpu

D_MODEL = 4096
BATCH = 4
SEQ = 2048
DEPTH = 2
DEC_BATCH = 128
DEC_SEQ = 4
PAST_LEN = 16384
W_A = D_MODEL // 2
W_B = D_MODEL - W_A
LRU_HEAD = 256
N_LRU_HEADS = W_A // LRU_HEAD
LRU_C = 8.0
CONV_W = 4
POOL_WINDOWS = (2, 4, 8, 16)
N_POOL_GROUPS = len(POOL_WINDOWS)
POOL_GC = W_B // N_POOL_GROUPS
POOL_PAD = max(POOL_WINDOWS) - 1
D_FF = 11008
PLE_DIM = 256
EPS = 1e-6

N_PROMPT = BATCH * SEQ
N_SAMPLE = DEC_BATCH * DEC_SEQ
N_TOK = N_PROMPT + N_SAMPLE
Z_W = 2 * W_A + W_B

F32 = jnp.float32
BF16 = jnp.bfloat16

VMEM_LIMIT_BYTES = 60 * 1024 * 1024

TM = 1088
TM_UP = 2176
UP_CHUNK = 272
ROW_CHUNK = 272
TM_DOWN = 544
TM_EMBED = 256
TM_FINAL = 512
LANES = 128
TN = 512
TF = 256
TB = 256
CONV_CARRY = 8
POOL_CARRY = 16
SAMPLE_CB = 512


def _params(*sem):
    return pltpu.CompilerParams(dimension_semantics=sem, vmem_limit_bytes=VMEM_LIMIT_BYTES)


def _dot(a, b):
    return jnp.dot(a, b, preferred_element_type=F32)


def _snake(i, j, nj):
    return jnp.where(i % 2 == 0, j, nj - 1 - j)


def _row_map(i, j):
    return (i, 0)


def _tile_map(nj):
    return lambda i, j: (i, _snake(i, j, nj))


def _col_map(nj):
    return lambda i, j: (0, _snake(i, j, nj))


def _layer_col_map(l, nj):
    return lambda i, j: (l, 0, _snake(i, j, nj))


def _log_sigmoid(x):
    return jnp.minimum(x, 0.0) - jnp.log1p(jnp.exp(-jnp.abs(x)))


def _lru_coeffs(xc, r, i, logsig):
    log_a = r * (LRU_C * logsig)
    a = jnp.exp(log_a)
    u = 1.0 - a * a
    mult = jnp.where(u > 0.0, u * lax.rsqrt(u), 0.0)
    return a, mult, i * xc


def _lane_partial_sumsq(x):
    sq = x * x
    part = sq[:, 0:LANES]
    for c in range(1, x.shape[1] // LANES):
        part = part + sq[:, c * LANES:(c + 1) * LANES]
    return part


def _inv_rms_lanes(partial_sumsq):
    tot = jnp.sum(partial_sumsq, axis=-1, keepdims=True)
    return jnp.broadcast_to(lax.rsqrt(tot * (1.0 / D_MODEL) + EPS), partial_sumsq.shape)


def _scale_rows(acc, inv_ref):
    inv = inv_ref[...]
    return jnp.concatenate(
        [acc[:, c * LANES:(c + 1) * LANES] * inv for c in range(acc.shape[1] // LANES)], axis=1)


def _store_residual(x_new, g_ref, o_ref, xg_ref, inv_ref):
    j, last = pl.program_id(1), pl.num_programs(1) - 1
    parts = []
    for r0 in range(0, o_ref.shape[0], ROW_CHUNK):
        rows = slice(r0, r0 + ROW_CHUNK)
        xc = x_new(rows)
        o_ref[rows, :] = xc
        xg_ref[rows, :] = (xc * g_ref[...]).astype(BF16)
        parts.append(_lane_partial_sumsq(xc))
    part = jnp.concatenate(parts, axis=0)

    @pl.when(j == 0)
    def _():
        inv_ref[...] = part

    @pl.when(jnp.logical_and(j > 0, j < last))
    def _():
        inv_ref[...] += part

    @pl.when(j == last)
    def _():
        inv_ref[...] = _inv_rms_lanes(inv_ref[...] + part)


def _residual_out(m, n, tm):
    specs = [
        pl.BlockSpec((tm, TN), _tile_map(n // TN)),
        pl.BlockSpec((tm, TN), _tile_map(n // TN)),
        pl.BlockSpec((tm, LANES), _row_map),
    ]
    shapes = [
        jax.ShapeDtypeStruct((m, n), F32),
        jax.ShapeDtypeStruct((m, n), BF16),
        jax.ShapeDtypeStruct((m, LANES), F32),
    ]
    return specs, shapes


def _embed_kernel(xp_ref, xs_ref, g_ref, x_ref, xg_ref, inv_ref):
    def emit(x):
        x_ref[...] = x
        xg_ref[...] = (x * g_ref[...]).astype(BF16)
        inv_ref[...] = _inv_rms_lanes(_lane_partial_sumsq(x))

    is_prompt = pl.program_id(0) < N_PROMPT // TM_EMBED

    @pl.when(is_prompt)
    def _():
        emit(xp_ref[...])

    @pl.when(jnp.logical_not(is_prompt))
    def _():
        emit(xs_ref[...])


def _embed(xp, xs, g):
    k = xp.shape[1]
    npb = N_PROMPT // TM_EMBED
    rows = lambda i: (i, 0)
    return pl.pallas_call(
        _embed_kernel,
        grid=(N_TOK // TM_EMBED,),
        in_specs=[
            pl.BlockSpec((TM_EMBED, k), lambda i: (jnp.minimum(i, npb - 1), 0)),
            pl.BlockSpec((TM_EMBED, k), lambda i: (jnp.maximum(i - npb, 0), 0)),
            pl.BlockSpec((1, k), lambda i: (0, 0)),
        ],
        out_specs=[
            pl.BlockSpec((TM_EMBED, k), rows),
            pl.BlockSpec((TM_EMBED, k), rows),
            pl.BlockSpec((TM_EMBED, LANES), rows),
        ],
        out_shape=[
            jax.ShapeDtypeStruct((N_TOK, k), F32),
            jax.ShapeDtypeStruct((N_TOK, k), BF16),
            jax.ShapeDtypeStruct((N_TOK, LANES), F32),
        ],
        compiler_params=_params("arbitrary"),
        name="embed",
    )(xp, xs, g)


def _final_norm_kernel(x_ref, inv_ref, g_ref, o_ref):
    o_ref[...] = _scale_rows(x_ref[...], inv_ref) * g_ref[...]


def _final_norm(x, inv, g, rows, row0):
    k = x.shape[1]
    tm = TM_FINAL
    assert rows % tm == 0 and row0 % tm == 0
    rb0 = row0 // tm
    return pl.pallas_call(
        _final_norm_kernel,
        grid=(rows // tm,),
        in_specs=[pl.BlockSpec((tm, k), lambda i: (rb0 + i, 0)),
                  pl.BlockSpec((tm, LANES), lambda i: (rb0 + i, 0)),
                  pl.BlockSpec((1, k), lambda i: (0, 0))],
        out_specs=pl.BlockSpec((tm, k), lambda i: (i, 0)),
        out_shape=jax.ShapeDtypeStruct((rows, k), F32),
        compiler_params=_params("parallel"),
        name="final_norm",
    )(x, inv, g)


def _in_proj_kernel(xg_ref, inv_ref, w_ref, o_ref):
    o_ref[...] = _scale_rows(_dot(xg_ref[...], w_ref[...].astype(BF16)), inv_ref)


def _in_proj(xg, inv, w_all, l):
    m, k = xg.shape
    n = w_all.shape[2]
    return pl.pallas_call(
        _in_proj_kernel,
        grid=(m // TM, n // TN),
        in_specs=[
            pl.BlockSpec((TM, k), _row_map),
            pl.BlockSpec((TM, LANES), _row_map),
            pl.BlockSpec((None, k, TN), _layer_col_map(l, n // TN)),
        ],
        out_specs=pl.BlockSpec((TM, TN), _tile_map(n // TN)),
        out_shape=jax.ShapeDtypeStruct((m, n), F32),
        compiler_params=_params("parallel", "arbitrary"),
        name="in_proj",
    )(xg, inv, w_all)


def _prompt_mixer_kernel(xa_ref, ga_ref, ub_ref, cw_ref, cb_ref, wr_ref, br_ref, wi_ref, bi_ref,
                         lam_ref, wp_ref, ps_ref,
                         y_ref, hl_ref, ct_ref, pt_ref,
                         hc_ref, cc_ref, pc_ref, a_ref, b_ref):
    t = pl.program_id(1)

    @pl.when(t == 0)
    def _():
        hc_ref[...] = jnp.zeros_like(hc_ref)
        cc_ref[...] = jnp.zeros_like(cc_ref)
        pc_ref[...] = jnp.zeros_like(pc_ref)

    pos = lax.broadcasted_iota(jnp.int32, (TB, 1), 0) + t * TB

    xa = xa_ref[...]
    ext = jnp.concatenate([cc_ref[...], xa], axis=0)
    xc = cb_ref[...] + cw_ref[CONV_W - 1:CONV_W, :] * xa
    for s in range(1, CONV_W):
        xc = xc + cw_ref[CONV_W - 1 - s:CONV_W - s, :] * pltpu.roll(ext, s, axis=0)[CONV_CARRY:]
    xcb = xc.astype(BF16)
    r_parts, i_parts = [], []
    for hd in range(N_LRU_HEADS):
        sl = slice(hd * LRU_HEAD, (hd + 1) * LRU_HEAD)
        r_parts.append(_dot(xcb[:, sl], wr_ref[hd]))
        i_parts.append(_dot(xcb[:, sl], wi_ref[hd]))
    r = jax.nn.sigmoid(jnp.concatenate(r_parts, axis=1) + br_ref[...])
    i = jax.nn.sigmoid(jnp.concatenate(i_parts, axis=1) + bi_ref[...])
    a, mult, ix = _lru_coeffs(xc, r, i, _log_sigmoid(lam_ref[...]))
    a_ref[...] = a
    b_ref[...] = mult * ix

    @pl.when(t == 0)
    def _():
        a_ref[0:1, :] = jnp.zeros((1, W_A), F32)
        b_ref[0:1, :] = ix[0:1, :]

    def step(row, h):
        h = a_ref[pl.ds(row, 1), :] * h + b_ref[pl.ds(row, 1), :]
        b_ref[pl.ds(row, 1), :] = h
        return h

    h_last = lax.fori_loop(0, TB, step, hc_ref[0:1, :], unroll=8)
    hc_ref[0:1, :] = h_last
    hl_ref[0] = h_last
    y_ref[:, 0:W_A] = (jax.nn.gelu(ga_ref[...]) * b_ref[...]).astype(BF16)
    cc_ref[...] = xa[TB - CONV_CARRY:]
    ct_ref[0] = xa[TB - CONV_CARRY:]

    ub = ub_ref[...]
    pext = jnp.concatenate([pc_ref[...], ub], axis=0)
    for g, w in enumerate(POOL_WINDOWS):
        sl = slice(g * POOL_GC, (g + 1) * POOL_GC)
        s = pext[:, sl]
        shift = 1
        while shift < w:
            s = s + pltpu.roll(s, shift, axis=0)
            shift *= 2
        cnt = jnp.minimum(pos + 1, w).astype(F32)
        d = s[POOL_CARRY:] / cnt - ub[:, sl]
        yb = _dot(d.astype(BF16), wp_ref[g]) * ps_ref[:, sl]
        y_ref[:, W_A + g * POOL_GC:W_A + (g + 1) * POOL_GC] = yb.astype(BF16)
    pc_ref[...] = ub[TB - POOL_CARRY:]
    pt_ref[0] = ub[TB - POOL_CARRY:]


def _prompt_mixer(z, cw, cb, wr, br, wi, bi, lam, wp, ps):
    nt = SEQ // TB
    row = lambda b, t: b * nt + t
    full2 = lambda shape: pl.BlockSpec(shape, lambda b, t: (0, 0))
    full3 = lambda shape: pl.BlockSpec(shape, lambda b, t: (0, 0, 0))
    return pl.pallas_call(
        _prompt_mixer_kernel,
        grid=(BATCH, nt),
        in_specs=[
            pl.BlockSpec((TB, W_A), lambda b, t: (row(b, t), 0)),
            pl.BlockSpec((TB, W_A), lambda b, t: (row(b, t), 1)),
            pl.BlockSpec((TB, W_B), lambda b, t: (row(b, t), 2)),
            full2((CONV_W, W_A)), full2((1, W_A)),
            full3((N_LRU_HEADS, LRU_HEAD, LRU_HEAD)), full2((1, W_A)),
            full3((N_LRU_HEADS, LRU_HEAD, LRU_HEAD)), full2((1, W_A)),
            full2((1, W_A)),
            full3((N_POOL_GROUPS, POOL_GC, POOL_GC)), full2((1, W_B)),
        ],
        out_specs=[
            pl.BlockSpec((TB, D_MODEL), lambda b, t: (row(b, t), 0)),
            pl.BlockSpec((1, 1, W_A), lambda b, t: (b, 0, 0)),
            pl.BlockSpec((1, CONV_CARRY, W_A), lambda b, t: (b, 0, 0)),
            pl.BlockSpec((1, POOL_CARRY, W_B), lambda b, t: (b, 0, 0)),
        ],
        out_shape=[
            jax.ShapeDtypeStruct((N_TOK, D_MODEL), BF16),
            jax.ShapeDtypeStruct((BATCH, 1, W_A), F32),
            jax.ShapeDtypeStruct((BATCH, CONV_CARRY, W_A), F32),
            jax.ShapeDtypeStruct((BATCH, POOL_CARRY, W_B), F32),
        ],
        scratch_shapes=[
            pltpu.VMEM((8, W_A), F32),
            pltpu.VMEM((CONV_CARRY, W_A), F32),
            pltpu.VMEM((POOL_CARRY, W_B), F32),
            pltpu.VMEM((TB, W_A), F32),
            pltpu.VMEM((TB, W_A), F32),
        ],
        compiler_params=_params("parallel", "arbitrary"),
        name="prompt_mixer",
    )(z, z, z, cw, cb, wr, br, wi, bi, lam, wp, ps)


def _sample_mixer_kernel(xa_ref, ga_ref, ub_ref, h0_ref, cs_ref, pst_ref,
                         cw_ref, cb_ref, wr_ref, br_ref, wi_ref, bi_ref, lam_ref, wp_ref, ps_ref,
                         y_in_ref, y_ref, hn_ref, cn_ref, pn_ref):
    del y_in_ref
    c = pl.program_id(0)
    heads = SAMPLE_CB // LRU_HEAD
    rows = lambda t: slice(t * DEC_BATCH, (t + 1) * DEC_BATCH)

    ext = [cs_ref[k] for k in range(CONV_W - 1)] + [xa_ref[rows(t), :] for t in range(DEC_SEQ)]
    logsig = _log_sigmoid(lam_ref[...])
    h = h0_ref[...]
    ya = []
    for t in range(DEC_SEQ):
        xc = cb_ref[...] + cw_ref[0:1, :] * ext[t]
        for k in range(1, CONV_W):
            xc = xc + cw_ref[k:k + 1, :] * ext[t + k]
        xcb = xc.astype(BF16)
        r_parts, i_parts = [], []
        for hd in range(heads):
            sl = slice(hd * LRU_HEAD, (hd + 1) * LRU_HEAD)
            r_parts.append(_dot(xcb[:, sl], wr_ref[hd]))
            i_parts.append(_dot(xcb[:, sl], wi_ref[hd]))
        r = jax.nn.sigmoid(jnp.concatenate(r_parts, axis=1) + br_ref[...])
        i = jax.nn.sigmoid(jnp.concatenate(i_parts, axis=1) + bi_ref[...])
        a, mult, ix = _lru_coeffs(xc, r, i, logsig)
        if PAST_LEN + t == 0:
            a, mult = jnp.zeros_like(a), jnp.ones_like(mult)
        h = a * h + mult * ix
        ya.append((jax.nn.gelu(ga_ref[rows(t), :]) * h).astype(BF16))
    hn_ref[...] = h
    for k in range(CONV_W - 1):
        cn_ref[k] = ext[DEC_SEQ + k]

    for k in range(POOL_PAD):
        src = k + DEC_SEQ
        pn_ref[k] = pst_ref[src] if src < POOL_PAD else ub_ref[rows(src - POOL_PAD), :]

    for g, w in enumerate(POOL_WINDOWS):
        @pl.when(c == g)
        def _(g=g, w=w):
            pext = ([pst_ref[k] for k in range(POOL_PAD - w + 1, POOL_PAD)]
                    + [ub_ref[rows(t), :] for t in range(DEC_SEQ)])
            for t in range(DEC_SEQ):
                s = pext[t]
                for j in range(1, w):
                    s = s + pext[t + j]
                cnt = float(min(PAST_LEN + t + 1, w))
                d = s / cnt - pext[t + w - 1]
                yb = _dot(d.astype(BF16), wp_ref[0]) * ps_ref[...]
                y_ref[rows(t), g * SAMPLE_CB:(g + 1) * SAMPLE_CB] = ya[t]
                y_ref[rows(t), W_A + g * POOL_GC:W_A + (g + 1) * POOL_GC] = yb.astype(BF16)


def _sample_mixer(z, y, h0, cs, pst, cw, cb, wr, br, wi, bi, lam, wp, ps):
    cbw = SAMPLE_CB
    na = W_A // cbw
    heads = cbw // LRU_HEAD
    rb = N_PROMPT // N_SAMPLE
    col = lambda off: (lambda c: (0, 0, off + c))
    vec = pl.BlockSpec((1, cbw), lambda c: (0, c))
    return pl.pallas_call(
        _sample_mixer_kernel,
        grid=(na,),
        in_specs=[
            pl.BlockSpec((N_SAMPLE, cbw), lambda c: (rb, c)),
            pl.BlockSpec((N_SAMPLE, cbw), lambda c: (rb, na + c)),
            pl.BlockSpec((N_SAMPLE, cbw), lambda c: (rb, 2 * na + c)),
            pl.BlockSpec((DEC_BATCH, cbw), lambda c: (0, c)),
            pl.BlockSpec((CONV_W - 1, DEC_BATCH, cbw), col(0)),
            pl.BlockSpec((POOL_PAD, DEC_BATCH, cbw), col(0)),
            pl.BlockSpec((CONV_W, cbw), lambda c: (0, c)), vec,
            pl.BlockSpec((heads, LRU_HEAD, LRU_HEAD), lambda c: (c, 0, 0)), vec,
            pl.BlockSpec((heads, LRU_HEAD, LRU_HEAD), lambda c: (c, 0, 0)), vec,
            vec,
            pl.BlockSpec((1, POOL_GC, POOL_GC), lambda c: (c, 0, 0)), vec,
            pl.BlockSpec(memory_space=pl.ANY),
        ],
        out_specs=[
            pl.BlockSpec((N_SAMPLE, D_MODEL), lambda c: (rb, 0)),
            pl.BlockSpec((DEC_BATCH, cbw), lambda c: (0, c)),
            pl.BlockSpec((CONV_W - 1, DEC_BATCH, cbw), col(0)),
            pl.BlockSpec((POOL_PAD, DEC_BATCH, cbw), col(0)),
        ],
        out_shape=[
            jax.ShapeDtypeStruct((N_TOK, D_MODEL), BF16),
            jax.ShapeDtypeStruct((DEC_BATCH, W_A), F32),
            jax.ShapeDtypeStruct((CONV_W - 1, DEC_BATCH, W_A), F32),
            jax.ShapeDtypeStruct((POOL_PAD, DEC_BATCH, W_B), F32),
        ],
        input_output_aliases={15: 0},
        compiler_params=_params("arbitrary"),
        name="sample_mixer",
    )(z, z, z, h0, cs, pst, cw, cb, wr, br, wi, bi, lam, wp, ps, y)


def _out_proj_kernel(y_ref, w_ref, x_ref, g_ref, o_ref, xg_ref, inv_ref):
    w = w_ref[...].astype(BF16)
    x_new = lambda rows: x_ref[rows, :] + _dot(y_ref[rows, :], w)
    _store_residual(x_new, g_ref, o_ref, xg_ref, inv_ref)


def _out_proj(y, w_all, l, x, g_next):
    m, k = y.shape
    n = w_all.shape[2]
    out_specs, out_shape = _residual_out(m, n, TM)
    return pl.pallas_call(
        _out_proj_kernel,
        grid=(m // TM, n // TN),
        in_specs=[
            pl.BlockSpec((TM, k), _row_map),
            pl.BlockSpec((None, k, TN), _layer_col_map(l, n // TN)),
            pl.BlockSpec((TM, TN), _tile_map(n // TN)),
            pl.BlockSpec((1, TN), _col_map(n // TN)),
        ],
        out_specs=out_specs,
        out_shape=out_shape,
        compiler_params=_params("parallel", "arbitrary"),
        name="out_proj",
    )(y, w_all, x, g_next)


def _ffn_up_kernel(xg_ref, inv_ref, wg_ref, wu_ref, wd_ref, h_ref, wdb_ref):
    wg = wg_ref[...].astype(BF16)
    wu = wu_ref[...].astype(BF16)
    for r0 in range(0, TM_UP, UP_CHUNK):
        rows = slice(r0, r0 + UP_CHUNK)
        xg = xg_ref[rows, :]
        inv = inv_ref.at[rows, :]
        gate = _scale_rows(_dot(xg, wg), inv)
        up = _scale_rows(_dot(xg, wu), inv)
        h_ref[rows, :] = (jax.nn.silu(gate) * up).astype(BF16)
    wdb_ref[...] = wd_ref[...].astype(BF16)


def _ffn_up(xg, inv, wg_all, wu_all, wd_all, l):
    m, k = xg.shape
    nf = D_FF // TF
    steps = (m // TM_UP) * nf
    wd_rows = D_FF // steps
    assert wd_rows * steps == D_FF and wd_rows % 16 == 0
    return pl.pallas_call(
        _ffn_up_kernel,
        grid=(m // TM_UP, nf),
        in_specs=[
            pl.BlockSpec((TM_UP, k), _row_map, pipeline_mode=pl.Buffered(1)),
            pl.BlockSpec((TM_UP, LANES), _row_map),
            pl.BlockSpec((None, k, TF), _layer_col_map(l, nf)),
            pl.BlockSpec((None, k, TF), _layer_col_map(l, nf)),
            pl.BlockSpec((None, wd_rows, k), lambda i, j: (l, i * nf + j, 0)),
        ],
        out_specs=[
            pl.BlockSpec((TM_UP, TF), _tile_map(nf)),
            pl.BlockSpec((wd_rows, k), lambda i, j: (i * nf + j, 0)),
        ],
        out_shape=[
            jax.ShapeDtypeStruct((m, D_FF), BF16),
            jax.ShapeDtypeStruct((D_FF, k), BF16),
        ],
        compiler_params=_params("arbitrary", "arbitrary"),
        name="ffn_up",
    )(xg, inv, wg_all, wu_all, wd_all)


def _ffn_down_kernel(h_ref, w_ref, x_ref, g_ref, o_ref, xg_ref, inv_ref):
    x_new = lambda rows: x_ref[rows, :] + _dot(h_ref[rows, :], w_ref[...])
    _store_residual(x_new, g_ref, o_ref, xg_ref, inv_ref)


def _ffn_down(h, wdb, x, g_next):
    m, kf = h.shape
    n = wdb.shape[1]
    out_specs, out_shape = _residual_out(m, n, TM_DOWN)
    return pl.pallas_call(
        _ffn_down_kernel,
        grid=(m // TM_DOWN, n // TN),
        in_specs=[
            pl.BlockSpec((TM_DOWN, kf), _row_map),
            pl.BlockSpec((kf, TN), _col_map(n // TN)),
            pl.BlockSpec((TM_DOWN, TN), _tile_map(n // TN)),
            pl.BlockSpec((1, TN), _col_map(n // TN)),
        ],
        out_specs=out_specs,
        out_shape=out_shape,
        compiler_params=_params("parallel", "arbitrary"),
        name="ffn_down",
    )(h, wdb, x, g_next)


def _ple_kernel(xg_ref, inv_ref, wg_ref, p_ref, wp_ref, x_ref, g_ref, o_ref, xgn_ref, invn_ref):
    wg = wg_ref[...].astype(BF16)
    wp = wp_ref[...].astype(BF16)

    def x_new(rows):
        gate = jax.nn.sigmoid(_scale_rows(_dot(xg_ref[rows, :], wg), inv_ref.at[rows, :]))
        return x_ref[rows, :] + gate * _dot(p_ref[rows, :], wp)

    _store_residual(x_new, g_ref, o_ref, xgn_ref, invn_ref)


def _ple(xg, inv, wg_all, p, wp_all, l, x, g_next):
    m, k = xg.shape
    out_specs, out_shape = _residual_out(m, k, TM)
    return pl.pallas_call(
        _ple_kernel,
        grid=(m // TM, k // TN),
        in_specs=[
            pl.BlockSpec((TM, k), _row_map),
            pl.BlockSpec((TM, LANES), _row_map),
            pl.BlockSpec((None, k, TN), _layer_col_map(l, k // TN)),
            pl.BlockSpec((TM, PLE_DIM), _row_map),
            pl.BlockSpec((None, PLE_DIM, TN), _layer_col_map(l, k // TN)),
            pl.BlockSpec((TM, TN), _tile_map(k // TN)),
            pl.BlockSpec((1, TN), _col_map(k // TN)),
        ],
        out_specs=out_specs,
        out_shape=out_shape,
        compiler_params=_params("parallel", "arbitrary"),
        name="ple",
    )(xg, inv, wg_all, p, wp_all, x, g_next)


def _time_major(a):
    return jnp.swapaxes(a, 0, 1)


def kernel(x_prompt, x_sample, state_h, state_conv, state_pool, p_prompt, p_sample, g_mix, w_in, conv_w, conv_b, w_rg, b_rg, w_ig, b_ig, lam, w_pool, pool_scale, w_out, g_ffn, w_gate, w_up, w_down, g_pe, w_pe_gate, w_pe_proj, g_final):
    row = lambda v: v.reshape(1, -1)
    x, xg, inv = _embed(x_prompt.reshape(N_PROMPT, D_MODEL),
                       _time_major(x_sample).reshape(N_SAMPLE, D_MODEL), row(g_mix[0]))
    hp_l, cp_l, pp_l, hs_l, cs_l, ps_l = [], [], [], [], [], []
    for l in range(DEPTH):
        cw, cb = conv_w[l], row(conv_b[l])
        wr, br = w_rg[l].astype(BF16), row(b_rg[l])
        wi, bi = w_ig[l].astype(BF16), row(b_ig[l])
        lm, wp, ps = row(lam[l]), w_pool[l].astype(BF16), row(pool_scale[l])

        g_after_ple = row(g_mix[l + 1]) if l + 1 < DEPTH else row(g_final)

        z = _in_proj(xg, inv, w_in, l)

        y, hp, ct, pt = _prompt_mixer(z, cw, cb, wr, br, wi, bi, lm, wp, ps)
        y, hs, cs, pst = _sample_mixer(
            z, y, state_h[l], _time_major(state_conv[l]), _time_major(state_pool[l]),
            cw, cb, wr, br, wi, bi, lm, wp, ps)

        x, xg, inv = _out_proj(y, w_out, l, x, row(g_ffn[l]))
        h, wdb = _ffn_up(xg, inv, w_gate, w_up, w_down, l)
        x, xg, inv = _ffn_down(h, wdb, x, row(g_pe[l]))
        p = jnp.concatenate([p_prompt[l].reshape(N_PROMPT, PLE_DIM),
                             _time_major(p_sample[l]).reshape(N_SAMPLE, PLE_DIM)], axis=0).astype(BF16)
        x, xg, inv = _ple(xg, inv, w_pe_gate, p, w_pe_proj, l, x, g_after_ple)

        hp_l.append(hp.reshape(BATCH, W_A))
        cp_l.append(ct[:, CONV_CARRY - (CONV_W - 1):])
        pp_l.append(pt[:, POOL_CARRY - POOL_PAD:])
        hs_l.append(hs)
        cs_l.append(_time_major(cs))
        ps_l.append(_time_major(pst))

    gf = row(g_final)
    y_prompt = _final_norm(x, inv, gf, N_PROMPT, 0).reshape(BATCH, SEQ, D_MODEL)
    y_s = _final_norm(x, inv, gf, N_SAMPLE, N_PROMPT)
    y_sample = _time_major(y_s.reshape(DEC_SEQ, DEC_BATCH, D_MODEL))
    return (y_prompt, y_sample,
            jnp.stack(hp_l), jnp.stack(cp_l), jnp.stack(pp_l),
            jnp.stack(hs_l), jnp.stack(cs_l), jnp.stack(ps_l))
```

```python
import functools

import jax
import jax.numpy as jnp
from jax import lax
from jax.experimental import pallas as pl
from jax.experimental.pallas import tpu as pltpu

D_MODEL = 4096
BATCH = 4
SEQ = 2048
DEPTH = 2
DEC_BATCH = 128
DEC_SEQ = 4
PAST_LEN = 16384
W_A = D_MODEL // 2
W_B = D_MODEL - W_A
LRU_HEAD = 256
N_LRU_HEADS = W_A // LRU_HEAD
LRU_C = 8.0
CONV_W = 4
POOL_WINDOWS = (2, 4, 8, 16)
N_POOL_GROUPS = len(POOL_WINDOWS)
POOL_GC = W_B // N_POOL_GROUPS
POOL_PAD = max(POOL_WINDOWS) - 1
D_FF = 11008
PLE_DIM = 256
EPS = 1e-6

N_PROMPT = BATCH * SEQ
N_SAMPLE = DEC_BATCH * DEC_SEQ
N_TOK = N_PROMPT + N_SAMPLE
Z_W = 2 * W_A + W_B

F32 = jnp.float32
BF16 = jnp.bfloat16

VMEM_LIMIT_BYTES = 60 * 1024 * 1024

TM = 1088
TM_UP = 2176
UP_CHUNK = 272
ROW_CHUNK = 272
TM_DOWN = 544
TM_EMBED = 256
TM_FINAL = 512
LANES = 128
TN = 512
TF = 256
TB = 256
CONV_CARRY = 8
POOL_CARRY = 16
SAMPLE_CB = 512


def _params(*sem):
    return pltpu.CompilerParams(dimension_semantics=sem, vmem_limit_bytes=VMEM_LIMIT_BYTES)


def _dot(a, b):
    return jnp.dot(a, b, preferred_element_type=F32)


def _snake(i, j, nj):
    return jnp.where(i % 2 == 0, j, nj - 1 - j)


def _row_map(i, j):
    return (i, 0)


def _tile_map(nj):
    return lambda i, j: (i, _snake(i, j, nj))


def _col_map(nj):
    return lambda i, j: (0, _snake(i, j, nj))


def _layer_col_map(l, nj):
    return lambda i, j: (l, 0, _snake(i, j, nj))


def _log_sigmoid(x):
    return jnp.minimum(x, 0.0) - jnp.log1p(jnp.exp(-jnp.abs(x)))


def _lru_coeffs(xc, r, i, logsig):
    log_a = r * (LRU_C * logsig)
    a = jnp.exp(log_a)
    u = 1.0 - a * a
    mult = jnp.where(u > 0.0, u * lax.rsqrt(u), 0.0)
    return a, mult, i * xc


def _lane_partial_sumsq(x):
    sq = x * x
    part = sq[:, 0:LANES]
    for c in range(1, x.shape[1] // LANES):
        part = part + sq[:, c * LANES:(c + 1) * LANES]
    return part


def _inv_rms_lanes(partial_sumsq):
    tot = jnp.sum(partial_sumsq, axis=-1, keepdims=True)
    return jnp.broadcast_to(lax.rsqrt(tot * (1.0 / D_MODEL) + EPS), partial_sumsq.shape)


def _scale_rows(acc, inv_ref):
    inv = inv_ref[...]
    return jnp.concatenate(
        [acc[:, c * LANES:(c + 1) * LANES] * inv for c in range(acc.shape[1] // LANES)], axis=1)


def _store_residual(x_new, g_ref, o_ref, xg_ref, inv_ref):
    j, last = pl.program_id(1), pl.num_programs(1) - 1
    parts = []
    for r0 in range(0, o_ref.shape[0], ROW_CHUNK):
        rows = slice(r0, r0 + ROW_CHUNK)
        xc = x_new(rows)
        o_ref[rows, :] = xc
        xg_ref[rows, :] = (xc * g_ref[...]).astype(BF16)
        parts.append(_lane_partial_sumsq(xc))
    part = jnp.concatenate(parts, axis=0)

    @pl.when(j == 0)
    def _():
        inv_ref[...] = part

    @pl.when(jnp.logical_and(j > 0, j < last))
    def _():
        inv_ref[...] += part

    @pl.when(j == last)
    def _():
        inv_ref[...] = _inv_rms_lanes(inv_ref[...] + part)


def _residual_out(m, n, tm):
    specs = [
        pl.BlockSpec((tm, TN), _tile_map(n // TN)),
        pl.BlockSpec((tm, TN), _tile_map(n // TN)),
        pl.BlockSpec((tm, LANES), _row_map),
    ]
    shapes = [
        jax.ShapeDtypeStruct((m, n), F32),
        jax.ShapeDtypeStruct((m, n), BF16),
        jax.ShapeDtypeStruct((m, LANES), F32),
    ]
    return specs, shapes


def _embed_kernel(xp_ref, xs_ref, g_ref, x_ref, xg_ref, inv_ref):
    def emit(x):
        x_ref[...] = x
        xg_ref[...] = (x * g_ref[...]).astype(BF16)
        inv_ref[...] = _inv_rms_lanes(_lane_partial_sumsq(x))

    is_prompt = pl.program_id(0) < N_PROMPT // TM_EMBED

    @pl.when(is_prompt)
    def _():
        emit(xp_ref[...])

    @pl.when(jnp.logical_not(is_prompt))
    def _():
        emit(xs_ref[...])


def _embed(xp, xs, g):
    k = xp.shape[1]
    npb = N_PROMPT // TM_EMBED
    rows = lambda i: (i, 0)
    return pl.pallas_call(
        _embed_kernel,
        grid=(N_TOK // TM_EMBED,),
        in_specs=[
            pl.BlockSpec((TM_EMBED, k), lambda i: (jnp.minimum(i, npb - 1), 0)),
            pl.BlockSpec((TM_EMBED, k), lambda i: (jnp.maximum(i - npb, 0), 0)),
            pl.BlockSpec((1, k), lambda i: (0, 0)),
        ],
        out_specs=[
            pl.BlockSpec((TM_EMBED, k), rows),
            pl.BlockSpec((TM_EMBED, k), rows),
            pl.BlockSpec((TM_EMBED, LANES), rows),
        ],
        out_shape=[
            jax.ShapeDtypeStruct((N_TOK, k), F32),
            jax.ShapeDtypeStruct((N_TOK, k), BF16),
            jax.ShapeDtypeStruct((N_TOK, LANES), F32),
        ],
        compiler_params=_params("arbitrary"),
        name="embed",
    )(xp, xs, g)


def _final_norm_kernel(x_ref, inv_ref, g_ref, o_ref):
    o_ref[...] = _scale_rows(x_ref[...], inv_ref) * g_ref[...]


def _final_norm(x, inv, g, rows, row0):
    k = x.shape[1]
    tm = TM_FINAL
    assert rows % tm == 0 and row0 % tm == 0
    rb0 = row0 // tm
    return pl.pallas_call(
        _final_norm_kernel,
        grid=(rows // tm,),
        in_specs=[pl.BlockSpec((tm, k), lambda i: (rb0 + i, 0)),
                  pl.BlockSpec((tm, LANES), lambda i: (rb0 + i, 0)),
                  pl.BlockSpec((1, k), lambda i: (0, 0))],
        out_specs=pl.BlockSpec((tm, k), lambda i: (i, 0)),
        out_shape=jax.ShapeDtypeStruct((rows, k), F32),
        compiler_params=_params("parallel"),
        name="final_norm",
    )(x, inv, g)


def _in_proj_kernel(xg_ref, inv_ref, w_ref, o_ref):
    o_ref[...] = _scale_rows(_dot(xg_ref[...], w_ref[...].astype(BF16)), inv_ref)


def _in_proj(xg, inv, w_all, l):
    m, k = xg.shape
    n = w_all.shape[2]
    return pl.pallas_call(
        _in_proj_kernel,
        grid=(m // TM, n // TN),
        in_specs=[
            pl.BlockSpec((TM, k), _row_map),
            pl.BlockSpec((TM, LANES), _row_map),
            pl.BlockSpec((None, k, TN), _layer_col_map(l, n // TN)),
        ],
        out_specs=pl.BlockSpec((TM, TN), _tile_map(n // TN)),
        out_shape=jax.ShapeDtypeStruct((m, n), F32),
        compiler_params=_params("parallel", "arbitrary"),
        name="in_proj",
    )(xg, inv, w_all)


def _prompt_mixer_kernel(xa_ref, ga_ref, ub_ref, cw_ref, cb_ref, wr_ref, br_ref, wi_ref, bi_ref,
                         lam_ref, wp_ref, ps_ref,
                         y_ref, hl_ref, ct_ref, pt_ref,
                         hc_ref, cc_ref, pc_ref, a_ref, b_ref):
    t = pl.program_id(1)

    @pl.when(t == 0)
    def _():
        hc_ref[...] = jnp.zeros_like(hc_ref)
        cc_ref[...] = jnp.zeros_like(cc_ref)
        pc_ref[...] = jnp.zeros_like(pc_ref)

    pos = lax.broadcasted_iota(jnp.int32, (TB, 1), 0) + t * TB

    xa = xa_ref[...]
    ext = jnp.concatenate([cc_ref[...], xa], axis=0)
    xc = cb_ref[...] + cw_ref[CONV_W - 1:CONV_W, :] * xa
    for s in range(1, CONV_W):
        xc = xc + cw_ref[CONV_W - 1 - s:CONV_W - s, :] * pltpu.roll(ext, s, axis=0)[CONV_CARRY:]
    xcb = xc.astype(BF16)
    r_parts, i_parts = [], []
    for hd in range(N_LRU_HEADS):
        sl = slice(hd * LRU_HEAD, (hd + 1) * LRU_HEAD)
        r_parts.append(_dot(xcb[:, sl], wr_ref[hd]))
        i_parts.append(_dot(xcb[:, sl], wi_ref[hd]))
    r = jax.nn.sigmoid(jnp.concatenate(r_parts, axis=1) + br_ref[...])
    i = jax.nn.sigmoid(jnp.concatenate(i_parts, axis=1) + bi_ref[...])
    a, mult, ix = _lru_coeffs(xc, r, i, _log_sigmoid(lam_ref[...]))
    a_ref[...] = a
    b_ref[...] = mult * ix

    @pl.when(t == 0)
    def _():
        a_ref[0:1, :] = jnp.zeros((1, W_A), F32)
        b_ref[0:1, :] = ix[0:1, :]

    def step(row, h):
        h = a_ref[pl.ds(row, 1), :] * h + b_ref[pl.ds(row, 1), :]
        b_ref[pl.ds(row, 1), :] = h
        return h

    h_last = lax.fori_loop(0, TB, step, hc_ref[0:1, :], unroll=8)
    hc_ref[0:1, :] = h_last
    hl_ref[0] = h_last
    y_ref[:, 0:W_A] = (jax.nn.gelu(ga_ref[...]) * b_ref[...]).astype(BF16)
    cc_ref[...] = xa[TB - CONV_CARRY:]
    ct_ref[0] = xa[TB - CONV_CARRY:]

    ub = ub_ref[...]
    pext = jnp.concatenate([pc_ref[...], ub], axis=0)
    for g, w in enumerate(POOL_WINDOWS):
        sl = slice(g * POOL_GC, (g + 1) * POOL_GC)
        s = pext[:, sl]
        shift = 1
        while shift < w:
            s = s + pltpu.roll(s, shift, axis=0)
            shift *= 2
        cnt = jnp.minimum(pos + 1, w).astype(F32)
        d = s[POOL_CARRY:] / cnt - ub[:, sl]
        yb = _dot(d.astype(BF16), wp_ref[g]) * ps_ref[:, sl]
        y_ref[:, W_A + g * POOL_GC:W_A + (g + 1) * POOL_GC] = yb.astype(BF16)
    pc_ref[...] = ub[TB - POOL_CARRY:]
    pt_ref[0] = ub[TB - POOL_CARRY:]


def _prompt_mixer(z, l, cw, cb, wr, br, wi, bi, lam, wp, ps):
    nt = SEQ // TB
    row = lambda b, t: b * nt + t
    full2 = lambda shape: pl.BlockSpec((None,) + shape, lambda b, t: (l, 0, 0))
    full3 = lambda shape: pl.BlockSpec((None,) + shape, lambda b, t: (l, 0, 0, 0))
    return pl.pallas_call(
        _prompt_mixer_kernel,
        grid=(BATCH, nt),
        in_specs=[
            pl.BlockSpec((TB, W_A), lambda b, t: (row(b, t), 0)),
            pl.BlockSpec((TB, W_A), lambda b, t: (row(b, t), 1)),
            pl.BlockSpec((TB, W_B), lambda b, t: (row(b, t), 2)),
            full2((CONV_W, W_A)), full2((1, W_A)),
            full3((N_LRU_HEADS, LRU_HEAD, LRU_HEAD)), full2((1, W_A)),
            full3((N_LRU_HEADS, LRU_HEAD, LRU_HEAD)), full2((1, W_A)),
            full2((1, W_A)),
            full3((N_POOL_GROUPS, POOL_GC, POOL_GC)), full2((1, W_B)),
        ],
        out_specs=[
            pl.BlockSpec((TB, D_MODEL), lambda b, t: (row(b, t), 0)),
            pl.BlockSpec((1, 1, W_A), lambda b, t: (b, 0, 0)),
            pl.BlockSpec((1, CONV_CARRY, W_A), lambda b, t: (b, 0, 0)),
            pl.BlockSpec((1, POOL_CARRY, W_B), lambda b, t: (b, 0, 0)),
        ],
        out_shape=[
            jax.ShapeDtypeStruct((N_TOK, D_MODEL), BF16),
            jax.ShapeDtypeStruct((BATCH, 1, W_A), F32),
            jax.ShapeDtypeStruct((BATCH, CONV_CARRY, W_A), F32),
            jax.ShapeDtypeStruct((BATCH, POOL_CARRY, W_B), F32),
        ],
        scratch_shapes=[
            pltpu.VMEM((8, W_A), F32),
            pltpu.VMEM((CONV_CARRY, W_A), F32),
            pltpu.VMEM((POOL_CARRY, W_B), F32),
            pltpu.VMEM((TB, W_A), F32),
            pltpu.VMEM((TB, W_A), F32),
        ],
        compiler_params=_params("parallel", "arbitrary"),
        name="prompt_mixer",
    )(z, z, z, cw, cb, wr, br, wi, bi, lam, wp, ps)


def _sample_mixer_kernel(xa_ref, ga_ref, ub_ref, h0_ref, cs_ref, pst_ref,
                         cw_ref, cb_ref, wr_ref, br_ref, wi_ref, bi_ref, lam_ref, wp_ref, ps_ref,
                         *aliased_and_out_refs):
    y_ref, hn_ref, cn_ref, pn_ref = aliased_and_out_refs[-4:]
    c = pl.program_id(0)
    heads = SAMPLE_CB // LRU_HEAD
    rows = lambda t: slice(t * DEC_BATCH, (t + 1) * DEC_BATCH)

    ext = [cs_ref[:, k, :] for k in range(CONV_W - 1)] + [xa_ref[rows(t), :] for t in range(DEC_SEQ)]
    logsig = _log_sigmoid(lam_ref[...])
    h = h0_ref[...]
    ya = []
    for t in range(DEC_SEQ):
        xc = cb_ref[...] + cw_ref[0:1, :] * ext[t]
        for k in range(1, CONV_W):
            xc = xc + cw_ref[k:k + 1, :] * ext[t + k]
        xcb = xc.astype(BF16)
        r_parts, i_parts = [], []
        for hd in range(heads):
            sl = slice(hd * LRU_HEAD, (hd + 1) * LRU_HEAD)
            r_parts.append(_dot(xcb[:, sl], wr_ref[hd]))
            i_parts.append(_dot(xcb[:, sl], wi_ref[hd]))
        r = jax.nn.sigmoid(jnp.concatenate(r_parts, axis=1) + br_ref[...])
        i = jax.nn.sigmoid(jnp.concatenate(i_parts, axis=1) + bi_ref[...])
        a, mult, ix = _lru_coeffs(xc, r, i, logsig)
        if PAST_LEN + t == 0:
            a, mult = jnp.zeros_like(a), jnp.ones_like(mult)
        h = a * h + mult * ix
        ya.append((jax.nn.gelu(ga_ref[rows(t), :]) * h).astype(BF16))
    hn_ref[...] = h
    for k in range(CONV_W - 1):
        cn_ref[:, k, :] = ext[DEC_SEQ + k]

    for k in range(POOL_PAD):
        src = k + DEC_SEQ
        pn_ref[:, k, :] = pst_ref[:, src, :] if src < POOL_PAD else ub_ref[rows(src - POOL_PAD), :]

    for g, w in enumerate(POOL_WINDOWS):
        @pl.when(c == g)
        def _(g=g, w=w):
            pext = ([pst_ref[:, k, :] for k in range(POOL_PAD - w + 1, POOL_PAD)]
                    + [ub_ref[rows(t), :] for t in range(DEC_SEQ)])
            for t in range(DEC_SEQ):
                s = pext[t]
                for j in range(1, w):
                    s = s + pext[t + j]
                cnt = float(min(PAST_LEN + t + 1, w))
                d = s / cnt - pext[t + w - 1]
                yb = _dot(d.astype(BF16), wp_ref[0]) * ps_ref[...]
                y_ref[rows(t), g * SAMPLE_CB:(g + 1) * SAMPLE_CB] = ya[t]
                y_ref[rows(t), W_A + g * POOL_GC:W_A + (g + 1) * POOL_GC] = yb.astype(BF16)


def _sample_mixer(z, y, state_h, state_conv, state_pool, prev_states, l,
                  cw, cb, wr, br, wi, bi, lam, wp, ps):
    cbw = SAMPLE_CB
    na = W_A // cbw
    heads = cbw // LRU_HEAD
    rb = N_PROMPT // N_SAMPLE
    vec = pl.BlockSpec((None, 1, cbw), lambda c: (l, 0, c))
    state3 = lambda steps: pl.BlockSpec((None, DEC_BATCH, steps, cbw), lambda c: (l, 0, 0, c))
    state2 = pl.BlockSpec((None, DEC_BATCH, cbw), lambda c: (l, 0, c))
    aliased = (y,) + (() if prev_states is None else tuple(prev_states))
    n_in = 15
    return pl.pallas_call(
        _sample_mixer_kernel,
        grid=(na,),
        in_specs=[
            pl.BlockSpec((N_SAMPLE, cbw), lambda c: (rb, c)),
            pl.BlockSpec((N_SAMPLE, cbw), lambda c: (rb, na + c)),
            pl.BlockSpec((N_SAMPLE, cbw), lambda c: (rb, 2 * na + c)),
            state2,
            state3(CONV_W - 1),
            state3(POOL_PAD),
            pl.BlockSpec((None, CONV_W, cbw), lambda c: (l, 0, c)), vec,
            pl.BlockSpec((None, heads, LRU_HEAD, LRU_HEAD), lambda c: (l, c, 0, 0)), vec,
            pl.BlockSpec((None, heads, LRU_HEAD, LRU_HEAD), lambda c: (l, c, 0, 0)), vec,
            vec,
            pl.BlockSpec((None, 1, POOL_GC, POOL_GC), lambda c: (l, c, 0, 0)), vec,
        ] + [pl.BlockSpec(memory_space=pl.ANY)] * len(aliased),
        out_specs=[
            pl.BlockSpec((N_SAMPLE, D_MODEL), lambda c: (rb, 0)),
            state2,
            state3(CONV_W - 1),
            state3(POOL_PAD),
        ],
        out_shape=[
            jax.ShapeDtypeStruct((N_TOK, D_MODEL), BF16),
            jax.ShapeDtypeStruct((DEPTH, DEC_BATCH, W_A), F32),
            jax.ShapeDtypeStruct((DEPTH, DEC_BATCH, CONV_W - 1, W_A), F32),
            jax.ShapeDtypeStruct((DEPTH, DEC_BATCH, POOL_PAD, W_B), F32),
        ],
        input_output_aliases={n_in + a: a for a in range(len(aliased))},
        compiler_params=_params("arbitrary"),
        name="sample_mixer",
    )(z, z, z, state_h, state_conv, state_pool, cw, cb, wr, br, wi, bi, lam, wp, ps, *aliased)


def _out_proj_kernel(y_ref, w_ref, x_ref, g_ref, o_ref, xg_ref, inv_ref):
    w = w_ref[...].astype(BF16)
    x_new = lambda rows: x_ref[rows, :] + _dot(y_ref[rows, :], w)
    _store_residual(x_new, g_ref, o_ref, xg_ref, inv_ref)


def _out_proj(y, w_all, l, x, g_next):
    m, k = y.shape
    n = w_all.shape[2]
    out_specs, out_shape = _residual_out(m, n, TM)
    return pl.pallas_call(
        _out_proj_kernel,
        grid=(m // TM, n // TN),
        in_specs=[
            pl.BlockSpec((TM, k), _row_map),
            pl.BlockSpec((None, k, TN), _layer_col_map(l, n // TN)),
            pl.BlockSpec((TM, TN), _tile_map(n // TN)),
            pl.BlockSpec((1, TN), _col_map(n // TN)),
        ],
        out_specs=out_specs,
        out_shape=out_shape,
        compiler_params=_params("parallel", "arbitrary"),
        name="out_proj",
    )(y, w_all, x, g_next)


def _ffn_up_kernel(xg_ref, inv_ref, wg_ref, wu_ref, wd_ref, h_ref, wdb_ref):
    wg = wg_ref[...].astype(BF16)
    wu = wu_ref[...].astype(BF16)
    for r0 in range(0, TM_UP, UP_CHUNK):
        rows = slice(r0, r0 + UP_CHUNK)
        xg = xg_ref[rows, :]
        inv = inv_ref.at[rows, :]
        gate = _scale_rows(_dot(xg, wg), inv)
        up = _scale_rows(_dot(xg, wu), inv)
        h_ref[rows, :] = (jax.nn.silu(gate) * up).astype(BF16)
    wdb_ref[...] = wd_ref[...].astype(BF16)


def _ffn_up(xg, inv, wg_all, wu_all, wd_all, l):
    m, k = xg.shape
    nf = D_FF // TF
    steps = (m // TM_UP) * nf
    wd_rows = D_FF // steps
    assert wd_rows * steps == D_FF and wd_rows % 16 == 0
    return pl.pallas_call(
        _ffn_up_kernel,
        grid=(m // TM_UP, nf),
        in_specs=[
            pl.BlockSpec((TM_UP, k), _row_map, pipeline_mode=pl.Buffered(1)),
            pl.BlockSpec((TM_UP, LANES), _row_map),
            pl.BlockSpec((None, k, TF), _layer_col_map(l, nf)),
            pl.BlockSpec((None, k, TF), _layer_col_map(l, nf)),
            pl.BlockSpec((None, wd_rows, k), lambda i, j: (l, i * nf + j, 0)),
        ],
        out_specs=[
            pl.BlockSpec((TM_UP, TF), _tile_map(nf)),
            pl.BlockSpec((wd_rows, k), lambda i, j: (i * nf + j, 0)),
        ],
        out_shape=[
            jax.ShapeDtypeStruct((m, D_FF), BF16),
            jax.ShapeDtypeStruct((D_FF, k), BF16),
        ],
        compiler_params=_params("arbitrary", "arbitrary"),
        name="ffn_up",
    )(xg, inv, wg_all, wu_all, wd_all)


def _ffn_down_kernel(h_ref, w_ref, x_ref, g_ref, o_ref, xg_ref, inv_ref):
    x_new = lambda rows: x_ref[rows, :] + _dot(h_ref[rows, :], w_ref[...])
    _store_residual(x_new, g_ref, o_ref, xg_ref, inv_ref)


def _ffn_down(h, wdb, x, g_next):
    m, kf = h.shape
    n = wdb.shape[1]
    out_specs, out_shape = _residual_out(m, n, TM_DOWN)
    return pl.pallas_call(
        _ffn_down_kernel,
        grid=(m // TM_DOWN, n // TN),
        in_specs=[
            pl.BlockSpec((TM_DOWN, kf), _row_map),
            pl.BlockSpec((kf, TN), _col_map(n // TN)),
            pl.BlockSpec((TM_DOWN, TN), _tile_map(n // TN)),
            pl.BlockSpec((1, TN), _col_map(n // TN)),
        ],
        out_specs=out_specs,
        out_shape=out_shape,
        compiler_params=_params("parallel", "arbitrary"),
        name="ffn_down",
    )(h, wdb, x, g_next)


def _ple_kernel(xg_ref, inv_ref, wg_ref, p_ref, wp_ref, x_ref, g_ref, o_ref, xgn_ref, invn_ref):
    wg = wg_ref[...].astype(BF16)
    wp = wp_ref[...].astype(BF16)

    def x_new(rows):
        gate = jax.nn.sigmoid(_scale_rows(_dot(xg_ref[rows, :], wg), inv_ref.at[rows, :]))
        return x_ref[rows, :] + gate * _dot(p_ref[rows, :], wp)

    _store_residual(x_new, g_ref, o_ref, xgn_ref, invn_ref)


def _ple(xg, inv, wg_all, p, wp_all, l, x, g_next):
    m, k = xg.shape
    out_specs, out_shape = _residual_out(m, k, TM)
    return pl.pallas_call(
        _ple_kernel,
        grid=(m // TM, k // TN),
        in_specs=[
            pl.BlockSpec((TM, k), _row_map),
            pl.BlockSpec((TM, LANES), _row_map),
            pl.BlockSpec((None, k, TN), _layer_col_map(l, k // TN)),
            pl.BlockSpec((None, TM, PLE_DIM), lambda i, j: (l, i, 0)),
            pl.BlockSpec((None, PLE_DIM, TN), _layer_col_map(l, k // TN)),
            pl.BlockSpec((TM, TN), _tile_map(k // TN)),
            pl.BlockSpec((1, TN), _col_map(k // TN)),
        ],
        out_specs=out_specs,
        out_shape=out_shape,
        compiler_params=_params("parallel", "arbitrary"),
        name="ple",
    )(xg, inv, wg_all, p, wp_all, x, g_next)


def _time_major(a):
    return jnp.swapaxes(a, 0, 1)


def kernel(x_prompt, x_sample, state_h, state_conv, state_pool, p_prompt, p_sample, g_mix, w_in, conv_w, conv_b, w_rg, b_rg, w_ig, b_ig, lam, w_pool, pool_scale, w_out, g_ffn, w_gate, w_up, w_down, g_pe, w_pe_gate, w_pe_proj, g_final):
    row = lambda v: v.reshape(1, -1)
    layer_rows = lambda v: v.reshape(DEPTH, 1, -1)
    x, xg, inv = _embed(x_prompt.reshape(N_PROMPT, D_MODEL),
                       _time_major(x_sample).reshape(N_SAMPLE, D_MODEL), row(g_mix[0]))
    mixer_params = (conv_w, layer_rows(conv_b),
                    w_rg.astype(BF16), layer_rows(b_rg), w_ig.astype(BF16), layer_rows(b_ig),
                    layer_rows(lam), w_pool.astype(BF16), layer_rows(pool_scale))
    p_all = jnp.concatenate(
        [p_prompt.reshape(DEPTH, N_PROMPT, PLE_DIM),
         jnp.swapaxes(p_sample, 1, 2).reshape(DEPTH, N_SAMPLE, PLE_DIM)], axis=1).astype(BF16)
    hp_l, cp_l, pp_l = [], [], []
    sample_states = None
    for l in range(DEPTH):
        g_after_ple = row(g_mix[l + 1]) if l + 1 < DEPTH else row(g_final)

        z = _in_proj(xg, inv, w_in, l)

        y, hp, ct, pt = _prompt_mixer(z, l, *mixer_params)
        y, *sample_states = _sample_mixer(z, y, state_h, state_conv, state_pool, sample_states, l,
                                          *mixer_params)

        x, xg, inv = _out_proj(y, w_out, l, x, row(g_ffn[l]))
        h, wdb = _ffn_up(xg, inv, w_gate, w_up, w_down, l)
        x, xg, inv = _ffn_down(h, wdb, x, row(g_pe[l]))
        x, xg, inv = _ple(xg, inv, w_pe_gate, p_all, w_pe_proj, l, x, g_after_ple)

        hp_l.append(hp.reshape(BATCH, W_A))
        cp_l.append(ct[:, CONV_CARRY - (CONV_W - 1):])
        pp_l.append(pt[:, POOL_CARRY - POOL_PAD:])

    gf = row(g_final)
    y_prompt = _final_norm(x, inv, gf, N_PROMPT, 0).reshape(BATCH, SEQ, D_MODEL)
    y_s = _final_norm(x, inv, gf, N_SAMPLE, N_PROMPT)
    y_sample = _time_major(y_s.reshape(DEC_SEQ, DEC_BATCH, D_MODEL))
    return (y_prompt, y_sample,
            jnp.stack(hp_l), jnp.stack(cp_l), jnp.stack(pp_l),
            *sample_states)
```

```python
import functools

import jax
import jax.numpy as jnp
from jax import lax
from jax.experimental import pallas as pl
from jax.experimental.pallas import tpu as pltpu

D_MODEL = 4096
BATCH = 4
SEQ = 2048
DEPTH = 2
DEC_BATCH = 128
DEC_SEQ = 4
PAST_LEN = 16384
W_A = D_MODEL // 2
W_B = D_MODEL - W_A
LRU_HEAD = 256
N_LRU_HEADS = W_A // LRU_HEAD
LRU_C = 8.0
CONV_W = 4
POOL_WINDOWS = (2, 4, 8, 16)
N_POOL_GROUPS = len(POOL_WINDOWS)
POOL_GC = W_B // N_POOL_GROUPS
POOL_PAD = max(POOL_WINDOWS) - 1
D_FF = 11008
PLE_DIM = 256
EPS = 1e-6

N_PROMPT = BATCH * SEQ
N_SAMPLE = DEC_BATCH * DEC_SEQ
N_TOK = N_PROMPT + N_SAMPLE
Z_W = 2 * W_A + W_B

F32 = jnp.float32
BF16 = jnp.bfloat16

VMEM_LIMIT_BYTES = 60 * 1024 * 1024

TM = 1088
TM_UP = 2176
UP_CHUNK = 272
ROW_CHUNK = 272
TM_DOWN = 544
TM_EMBED = 256
TM_FINAL = 512
LANES = 128
TN = 512
TF = 256
TB = 256
CONV_CARRY = 8
POOL_CARRY = 16
SAMPLE_CB = 512


def _params(*sem):
    return pltpu.CompilerParams(dimension_semantics=sem, vmem_limit_bytes=VMEM_LIMIT_BYTES)


def _dot(a, b):
    return jnp.dot(a, b, preferred_element_type=F32)


def _snake(i, j, nj):
    return jnp.where(i % 2 == 0, j, nj - 1 - j)


def _row_map(i, j):
    return (i, 0)


def _tile_map(nj):
    return lambda i, j: (i, _snake(i, j, nj))


def _col_map(nj):
    return lambda i, j: (0, _snake(i, j, nj))


def _layer_col_map(l, nj):
    return lambda i, j: (l, 0, _snake(i, j, nj))


def _log_sigmoid(x):
    return jnp.minimum(x, 0.0) - jnp.log1p(jnp.exp(-jnp.abs(x)))


def _lru_coeffs(xc, r, i, logsig):
    log_a = r * (LRU_C * logsig)
    a = jnp.exp(log_a)
    u = 1.0 - a * a
    mult = jnp.where(u > 0.0, u * lax.rsqrt(u), 0.0)
    return a, mult, i * xc


def _lane_partial_sumsq(x):
    sq = x * x
    part = sq[:, 0:LANES]
    for c in range(1, x.shape[1] // LANES):
        part = part + sq[:, c * LANES:(c + 1) * LANES]
    return part


def _inv_rms_lanes(partial_sumsq):
    tot = jnp.sum(partial_sumsq, axis=-1, keepdims=True)
    return jnp.broadcast_to(lax.rsqrt(tot * (1.0 / D_MODEL) + EPS), partial_sumsq.shape)


def _scale_rows(acc, inv_ref):
    inv = inv_ref[...]
    return jnp.concatenate(
        [acc[:, c * LANES:(c + 1) * LANES] * inv for c in range(acc.shape[1] // LANES)], axis=1)


def _store_residual(x_new, g_ref, o_ref, xg_ref, inv_ref):
    j, last = pl.program_id(1), pl.num_programs(1) - 1
    parts = []
    for r0 in range(0, o_ref.shape[0], ROW_CHUNK):
        rows = slice(r0, r0 + ROW_CHUNK)
        xc = x_new(rows)
        o_ref[rows, :] = xc
        xg_ref[rows, :] = (xc * g_ref[...]).astype(BF16)
        parts.append(_lane_partial_sumsq(xc))
    part = jnp.concatenate(parts, axis=0)

    @pl.when(j == 0)
    def _():
        inv_ref[...] = part

    @pl.when(jnp.logical_and(j > 0, j < last))
    def _():
        inv_ref[...] += part

    @pl.when(j == last)
    def _():
        inv_ref[...] = _inv_rms_lanes(inv_ref[...] + part)


def _residual_out(m, n, tm):
    specs = [
        pl.BlockSpec((tm, TN), _tile_map(n // TN)),
        pl.BlockSpec((tm, TN), _tile_map(n // TN)),
        pl.BlockSpec((tm, LANES), _row_map),
    ]
    shapes = [
        jax.ShapeDtypeStruct((m, n), F32),
        jax.ShapeDtypeStruct((m, n), BF16),
        jax.ShapeDtypeStruct((m, LANES), F32),
    ]
    return specs, shapes


def _embed_kernel(xp_ref, xs_ref, g_ref, x_ref, xg_ref, inv_ref):
    def emit(x):
        x_ref[...] = x
        xg_ref[...] = (x * g_ref[...]).astype(BF16)
        inv_ref[...] = _inv_rms_lanes(_lane_partial_sumsq(x))

    is_prompt = pl.program_id(0) < N_PROMPT // TM_EMBED

    @pl.when(is_prompt)
    def _():
        emit(xp_ref[...])

    @pl.when(jnp.logical_not(is_prompt))
    def _():
        emit(xs_ref[...])


def _embed(xp, xs, g):
    k = xp.shape[1]
    npb = N_PROMPT // TM_EMBED
    rows = lambda i: (i, 0)
    return pl.pallas_call(
        _embed_kernel,
        grid=(N_TOK // TM_EMBED,),
        in_specs=[
            pl.BlockSpec((TM_EMBED, k), lambda i: (jnp.minimum(i, npb - 1), 0)),
            pl.BlockSpec((TM_EMBED, k), lambda i: (jnp.maximum(i - npb, 0), 0)),
            pl.BlockSpec((1, k), lambda i: (0, 0)),
        ],
        out_specs=[
            pl.BlockSpec((TM_EMBED, k), rows),
            pl.BlockSpec((TM_EMBED, k), rows),
            pl.BlockSpec((TM_EMBED, LANES), rows),
        ],
        out_shape=[
            jax.ShapeDtypeStruct((N_TOK, k), F32),
            jax.ShapeDtypeStruct((N_TOK, k), BF16),
            jax.ShapeDtypeStruct((N_TOK, LANES), F32),
        ],
        compiler_params=_params("arbitrary"),
        name="embed",
    )(xp, xs, g)


def _final_norm_kernel(x_ref, inv_ref, g_ref, o_ref):
    o_ref[...] = _scale_rows(x_ref[...], inv_ref) * g_ref[...]


def _final_norm(x, inv, g, rows, row0):
    k = x.shape[1]
    tm = TM_FINAL
    assert rows % tm == 0 and row0 % tm == 0
    rb0 = row0 // tm
    return pl.pallas_call(
        _final_norm_kernel,
        grid=(rows // tm,),
        in_specs=[pl.BlockSpec((tm, k), lambda i: (rb0 + i, 0)),
                  pl.BlockSpec((tm, LANES), lambda i: (rb0 + i, 0)),
                  pl.BlockSpec((1, k), lambda i: (0, 0))],
        out_specs=pl.BlockSpec((tm, k), lambda i: (i, 0)),
        out_shape=jax.ShapeDtypeStruct((rows, k), F32),
        compiler_params=_params("parallel"),
        name="final_norm",
    )(x, inv, g)


def _in_proj_kernel(xg_ref, inv_ref, w_ref, o_ref):
    o_ref[...] = _scale_rows(_dot(xg_ref[...], w_ref[...].astype(BF16)), inv_ref)


def _in_proj(xg, inv, w_all, l):
    m, k = xg.shape
    n = w_all.shape[2]
    return pl.pallas_call(
        _in_proj_kernel,
        grid=(m // TM, n // TN),
        in_specs=[
            pl.BlockSpec((TM, k), _row_map),
            pl.BlockSpec((TM, LANES), _row_map),
            pl.BlockSpec((None, k, TN), _layer_col_map(l, n // TN)),
        ],
        out_specs=pl.BlockSpec((TM, TN), _tile_map(n // TN)),
        out_shape=jax.ShapeDtypeStruct((m, n), F32),
        compiler_params=_params("parallel", "arbitrary"),
        name="in_proj",
    )(xg, inv, w_all)


def _prompt_mixer_kernel(xa_ref, ga_ref, ub_ref, cw_ref, cb_ref, wr_ref, br_ref, wi_ref, bi_ref,
                         lam_ref, wp_ref, ps_ref,
                         y_ref, hl_ref, ct_ref, pt_ref,
                         hc_ref, cc_ref, pc_ref, a_ref, b_ref):
    t = pl.program_id(1)

    @pl.when(t == 0)
    def _():
        hc_ref[...] = jnp.zeros_like(hc_ref)
        cc_ref[...] = jnp.zeros_like(cc_ref)
        pc_ref[...] = jnp.zeros_like(pc_ref)

    pos = lax.broadcasted_iota(jnp.int32, (TB, 1), 0) + t * TB

    xa = xa_ref[...]
    ext = jnp.concatenate([cc_ref[...], xa], axis=0)
    xc = cb_ref[...] + cw_ref[CONV_W - 1:CONV_W, :] * xa
    for s in range(1, CONV_W):
        xc = xc + cw_ref[CONV_W - 1 - s:CONV_W - s, :] * pltpu.roll(ext, s, axis=0)[CONV_CARRY:]
    xcb = xc.astype(BF16)
    r_parts, i_parts = [], []
    for hd in range(N_LRU_HEADS):
        sl = slice(hd * LRU_HEAD, (hd + 1) * LRU_HEAD)
        r_parts.append(_dot(xcb[:, sl], wr_ref[hd]))
        i_parts.append(_dot(xcb[:, sl], wi_ref[hd]))
    r = jax.nn.sigmoid(jnp.concatenate(r_parts, axis=1) + br_ref[...])
    i = jax.nn.sigmoid(jnp.concatenate(i_parts, axis=1) + bi_ref[...])
    a, mult, ix = _lru_coeffs(xc, r, i, _log_sigmoid(lam_ref[...]))
    a_ref[...] = a
    b_ref[...] = mult * ix

    @pl.when(t == 0)
    def _():
        a_ref[0:1, :] = jnp.zeros((1, W_A), F32)
        b_ref[0:1, :] = ix[0:1, :]

    def step(row, h):
        h = a_ref[pl.ds(row, 1), :] * h + b_ref[pl.ds(row, 1), :]
        b_ref[pl.ds(row, 1), :] = h
        return h

    h_last = lax.fori_loop(0, TB, step, hc_ref[0:1, :], unroll=8)
    hc_ref[0:1, :] = h_last
    hl_ref[0] = h_last
    y_ref[:, 0:W_A] = (jax.nn.gelu(ga_ref[...]) * b_ref[...]).astype(BF16)
    cc_ref[...] = xa[TB - CONV_CARRY:]
    ct_ref[0] = xa[TB - CONV_CARRY:]

    ub = ub_ref[...]
    pext = jnp.concatenate([pc_ref[...], ub], axis=0)
    for g, w in enumerate(POOL_WINDOWS):
        sl = slice(g * POOL_GC, (g + 1) * POOL_GC)
        s = pext[:, sl]
        shift = 1
        while shift < w:
            s = s + pltpu.roll(s, shift, axis=0)
            shift *= 2
        cnt = jnp.minimum(pos + 1, w).astype(F32)
        d = s[POOL_CARRY:] / cnt - ub[:, sl]
        yb = _dot(d.astype(BF16), wp_ref[g]) * ps_ref[:, sl]
        y_ref[:, W_A + g * POOL_GC:W_A + (g + 1) * POOL_GC] = yb.astype(BF16)
    pc_ref[...] = ub[TB - POOL_CARRY:]
    pt_ref[0] = ub[TB - POOL_CARRY:]


def _prompt_mixer(z, l, cw, cb, wr, br, wi, bi, lam, wp, ps):
    nt = SEQ // TB
    row = lambda b, t: b * nt + t
    full2 = lambda shape: pl.BlockSpec((None,) + shape, lambda b, t: (l, 0, 0))
    full3 = lambda shape: pl.BlockSpec((None,) + shape, lambda b, t: (l, 0, 0, 0))
    return pl.pallas_call(
        _prompt_mixer_kernel,
        grid=(BATCH, nt),
        in_specs=[
            pl.BlockSpec((TB, W_A), lambda b, t: (row(b, t), 0)),
            pl.BlockSpec((TB, W_A), lambda b, t: (row(b, t), 1)),
            pl.BlockSpec((TB, W_B), lambda b, t: (row(b, t), 2)),
            full2((CONV_W, W_A)), full2((1, W_A)),
            full3((N_LRU_HEADS, LRU_HEAD, LRU_HEAD)), full2((1, W_A)),
            full3((N_LRU_HEADS, LRU_HEAD, LRU_HEAD)), full2((1, W_A)),
            full2((1, W_A)),
            full3((N_POOL_GROUPS, POOL_GC, POOL_GC)), full2((1, W_B)),
        ],
        out_specs=[
            pl.BlockSpec((TB, D_MODEL), lambda b, t: (row(b, t), 0)),
            pl.BlockSpec((1, 1, W_A), lambda b, t: (b, 0, 0)),
            pl.BlockSpec((1, CONV_CARRY, W_A), lambda b, t: (b, 0, 0)),
            pl.BlockSpec((1, POOL_CARRY, W_B), lambda b, t: (b, 0, 0)),
        ],
        out_shape=[
            jax.ShapeDtypeStruct((N_TOK, D_MODEL), BF16),
            jax.ShapeDtypeStruct((BATCH, 1, W_A), F32),
            jax.ShapeDtypeStruct((BATCH, CONV_CARRY, W_A), F32),
            jax.ShapeDtypeStruct((BATCH, POOL_CARRY, W_B), F32),
        ],
        scratch_shapes=[
            pltpu.VMEM((8, W_A), F32),
            pltpu.VMEM((CONV_CARRY, W_A), F32),
            pltpu.VMEM((POOL_CARRY, W_B), F32),
            pltpu.VMEM((TB, W_A), F32),
            pltpu.VMEM((TB, W_A), F32),
        ],
        compiler_params=_params("parallel", "arbitrary"),
        name="prompt_mixer",
    )(z, z, z, cw, cb, wr, br, wi, bi, lam, wp, ps)


def _sample_mixer_kernel(xa_ref, ga_ref, ub_ref, h0_ref, cs_ref, pst_ref,
                         cw_ref, cb_ref, wr_ref, br_ref, wi_ref, bi_ref, lam_ref, wp_ref, ps_ref,
                         *aliased_and_out_refs):
    y_ref, hn_ref, cn_ref, pn_ref = aliased_and_out_refs[-4:]
    c = pl.program_id(0)
    heads = SAMPLE_CB // LRU_HEAD
    rows = lambda t: slice(t * DEC_BATCH, (t + 1) * DEC_BATCH)

    ext = [cs_ref[k] for k in range(CONV_W - 1)] + [xa_ref[rows(t), :] for t in range(DEC_SEQ)]
    logsig = _log_sigmoid(lam_ref[...])
    h = h0_ref[...]
    ya = []
    for t in range(DEC_SEQ):
        xc = cb_ref[...] + cw_ref[0:1, :] * ext[t]
        for k in range(1, CONV_W):
            xc = xc + cw_ref[k:k + 1, :] * ext[t + k]
        xcb = xc.astype(BF16)
        r_parts, i_parts = [], []
        for hd in range(heads):
            sl = slice(hd * LRU_HEAD, (hd + 1) * LRU_HEAD)
            r_parts.append(_dot(xcb[:, sl], wr_ref[hd]))
            i_parts.append(_dot(xcb[:, sl], wi_ref[hd]))
        r = jax.nn.sigmoid(jnp.concatenate(r_parts, axis=1) + br_ref[...])
        i = jax.nn.sigmoid(jnp.concatenate(i_parts, axis=1) + bi_ref[...])
        a, mult, ix = _lru_coeffs(xc, r, i, logsig)
        if PAST_LEN + t == 0:
            a, mult = jnp.zeros_like(a), jnp.ones_like(mult)
        h = a * h + mult * ix
        ya.append((jax.nn.gelu(ga_ref[rows(t), :]) * h).astype(BF16))
    hn_ref[...] = h
    for k in range(CONV_W - 1):
        cn_ref[k] = ext[DEC_SEQ + k]

    for k in range(POOL_PAD):
        src = k + DEC_SEQ
        pn_ref[k] = pst_ref[src] if src < POOL_PAD else ub_ref[rows(src - POOL_PAD), :]

    for g, w in enumerate(POOL_WINDOWS):
        @pl.when(c == g)
        def _(g=g, w=w):
            pext = ([pst_ref[k] for k in range(POOL_PAD - w + 1, POOL_PAD)]
                    + [ub_ref[rows(t), :] for t in range(DEC_SEQ)])
            for t in range(DEC_SEQ):
                s = pext[t]
                for j in range(1, w):
                    s = s + pext[t + j]
                cnt = float(min(PAST_LEN + t + 1, w))
                d = s / cnt - pext[t + w - 1]
                yb = _dot(d.astype(BF16), wp_ref[0]) * ps_ref[...]
                y_ref[rows(t), g * SAMPLE_CB:(g + 1) * SAMPLE_CB] = ya[t]
                y_ref[rows(t), W_A + g * POOL_GC:W_A + (g + 1) * POOL_GC] = yb.astype(BF16)


def _sample_mixer(z, y, state_h, state_conv, state_pool, prev_states, l,
                  cw, cb, wr, br, wi, bi, lam, wp, ps):
    cbw = SAMPLE_CB
    na = W_A // cbw
    heads = cbw // LRU_HEAD
    rb = N_PROMPT // N_SAMPLE
    vec = pl.BlockSpec((None, 1, cbw), lambda c: (l, 0, c))
    state3 = lambda steps: pl.BlockSpec((None, steps, DEC_BATCH, cbw), lambda c: (l, 0, 0, c))
    state2 = pl.BlockSpec((None, DEC_BATCH, cbw), lambda c: (l, 0, c))
    aliased = (y,) + (() if prev_states is None else tuple(prev_states))
    n_in = 15
    return pl.pallas_call(
        _sample_mixer_kernel,
        grid=(na,),
        in_specs=[
            pl.BlockSpec((N_SAMPLE, cbw), lambda c: (rb, c)),
            pl.BlockSpec((N_SAMPLE, cbw), lambda c: (rb, na + c)),
            pl.BlockSpec((N_SAMPLE, cbw), lambda c: (rb, 2 * na + c)),
            state2,
            state3(CONV_W - 1),
            state3(POOL_PAD),
            pl.BlockSpec((None, CONV_W, cbw), lambda c: (l, 0, c)), vec,
            pl.BlockSpec((None, heads, LRU_HEAD, LRU_HEAD), lambda c: (l, c, 0, 0)), vec,
            pl.BlockSpec((None, heads, LRU_HEAD, LRU_HEAD), lambda c: (l, c, 0, 0)), vec,
            vec,
            pl.BlockSpec((None, 1, POOL_GC, POOL_GC), lambda c: (l, c, 0, 0)), vec,
        ] + [pl.BlockSpec(memory_space=pl.ANY)] * len(aliased),
        out_specs=[
            pl.BlockSpec((N_SAMPLE, D_MODEL), lambda c: (rb, 0)),
            state2,
            state3(CONV_W - 1),
            state3(POOL_PAD),
        ],
        out_shape=[
            jax.ShapeDtypeStruct((N_TOK, D_MODEL), BF16),
            jax.ShapeDtypeStruct((DEPTH, DEC_BATCH, W_A), F32),
            jax.ShapeDtypeStruct((DEPTH, CONV_W - 1, DEC_BATCH, W_A), F32),
            jax.ShapeDtypeStruct((DEPTH, POOL_PAD, DEC_BATCH, W_B), F32),
        ],
        input_output_aliases={n_in + a: a for a in range(len(aliased))},
        compiler_params=_params("arbitrary"),
        name="sample_mixer",
    )(z, z, z, state_h, state_conv, state_pool, cw, cb, wr, br, wi, bi, lam, wp, ps, *aliased)


def _out_proj_kernel(y_ref, w_ref, x_ref, g_ref, o_ref, xg_ref, inv_ref):
    w = w_ref[...].astype(BF16)
    x_new = lambda rows: x_ref[rows, :] + _dot(y_ref[rows, :], w)
    _store_residual(x_new, g_ref, o_ref, xg_ref, inv_ref)


def _out_proj(y, w_all, l, x, g_next):
    m, k = y.shape
    n = w_all.shape[2]
    out_specs, out_shape = _residual_out(m, n, TM)
    return pl.pallas_call(
        _out_proj_kernel,
        grid=(m // TM, n // TN),
        in_specs=[
            pl.BlockSpec((TM, k), _row_map),
            pl.BlockSpec((None, k, TN), _layer_col_map(l, n // TN)),
            pl.BlockSpec((TM, TN), _tile_map(n // TN)),
            pl.BlockSpec((1, TN), _col_map(n // TN)),
        ],
        out_specs=out_specs,
        out_shape=out_shape,
        compiler_params=_params("parallel", "arbitrary"),
        name="out_proj",
    )(y, w_all, x, g_next)


def _ffn_up_kernel(xg_ref, inv_ref, wg_ref, wu_ref, wd_ref, h_ref, wdb_ref):
    wg = wg_ref[...].astype(BF16)
    wu = wu_ref[...].astype(BF16)
    for r0 in range(0, TM_UP, UP_CHUNK):
        rows = slice(r0, r0 + UP_CHUNK)
        xg = xg_ref[rows, :]
        inv = inv_ref.at[rows, :]
        gate = _scale_rows(_dot(xg, wg), inv)
        up = _scale_rows(_dot(xg, wu), inv)
        h_ref[rows, :] = (jax.nn.silu(gate) * up).astype(BF16)
    wdb_ref[...] = wd_ref[...].astype(BF16)


def _ffn_up(xg, inv, wg_all, wu_all, wd_all, l):
    m, k = xg.shape
    nf = D_FF // TF
    steps = (m // TM_UP) * nf
    wd_rows = D_FF // steps
    assert wd_rows * steps == D_FF and wd_rows % 16 == 0
    return pl.pallas_call(
        _ffn_up_kernel,
        grid=(m // TM_UP, nf),
        in_specs=[
            pl.BlockSpec((TM_UP, k), _row_map, pipeline_mode=pl.Buffered(1)),
            pl.BlockSpec((TM_UP, LANES), _row_map),
            pl.BlockSpec((None, k, TF), _layer_col_map(l, nf)),
            pl.BlockSpec((None, k, TF), _layer_col_map(l, nf)),
            pl.BlockSpec((None, wd_rows, k), lambda i, j: (l, i * nf + j, 0)),
        ],
        out_specs=[
            pl.BlockSpec((TM_UP, TF), _tile_map(nf)),
            pl.BlockSpec((wd_rows, k), lambda i, j: (i * nf + j, 0)),
        ],
        out_shape=[
            jax.ShapeDtypeStruct((m, D_FF), BF16),
            jax.ShapeDtypeStruct((D_FF, k), BF16),
        ],
        compiler_params=_params("arbitrary", "arbitrary"),
        name="ffn_up",
    )(xg, inv, wg_all, wu_all, wd_all)


def _ffn_down_kernel(h_ref, w_ref, x_ref, g_ref, o_ref, xg_ref, inv_ref):
    x_new = lambda rows: x_ref[rows, :] + _dot(h_ref[rows, :], w_ref[...])
    _store_residual(x_new, g_ref, o_ref, xg_ref, inv_ref)


def _ffn_down(h, wdb, x, g_next):
    m, kf = h.shape
    n = wdb.shape[1]
    out_specs, out_shape = _residual_out(m, n, TM_DOWN)
    return pl.pallas_call(
        _ffn_down_kernel,
        grid=(m // TM_DOWN, n // TN),
        in_specs=[
            pl.BlockSpec((TM_DOWN, kf), _row_map),
            pl.BlockSpec((kf, TN), _col_map(n // TN)),
            pl.BlockSpec((TM_DOWN, TN), _tile_map(n // TN)),
            pl.BlockSpec((1, TN), _col_map(n // TN)),
        ],
        out_specs=out_specs,
        out_shape=out_shape,
        compiler_params=_params("parallel", "arbitrary"),
        name="ffn_down",
    )(h, wdb, x, g_next)


def _ple_kernel(xg_ref, inv_ref, wg_ref, p_ref, wp_ref, x_ref, g_ref, o_ref, xgn_ref, invn_ref):
    wg = wg_ref[...].astype(BF16)
    wp = wp_ref[...].astype(BF16)

    def x_new(rows):
        gate = jax.nn.sigmoid(_scale_rows(_dot(xg_ref[rows, :], wg), inv_ref.at[rows, :]))
        return x_ref[rows, :] + gate * _dot(p_ref[rows, :], wp)

    _store_residual(x_new, g_ref, o_ref, xgn_ref, invn_ref)


def _ple(xg, inv, wg_all, p, wp_all, l, x, g_next):
    m, k = xg.shape
    out_specs, out_shape = _residual_out(m, k, TM)
    return pl.pallas_call(
        _ple_kernel,
        grid=(m // TM, k // TN),
        in_specs=[
            pl.BlockSpec((TM, k), _row_map),
            pl.BlockSpec((TM, LANES), _row_map),
            pl.BlockSpec((None, k, TN), _layer_col_map(l, k // TN)),
            pl.BlockSpec((None, TM, PLE_DIM), lambda i, j: (l, i, 0)),
            pl.BlockSpec((None, PLE_DIM, TN), _layer_col_map(l, k // TN)),
            pl.BlockSpec((TM, TN), _tile_map(k // TN)),
            pl.BlockSpec((1, TN), _col_map(k // TN)),
        ],
        out_specs=out_specs,
        out_shape=out_shape,
        compiler_params=_params("parallel", "arbitrary"),
        name="ple",
    )(xg, inv, wg_all, p, wp_all, x, g_next)


def _time_major(a):
    return jnp.swapaxes(a, 0, 1)


def kernel(x_prompt, x_sample, state_h, state_conv, state_pool, p_prompt, p_sample, g_mix, w_in, conv_w, conv_b, w_rg, b_rg, w_ig, b_ig, lam, w_pool, pool_scale, w_out, g_ffn, w_gate, w_up, w_down, g_pe, w_pe_gate, w_pe_proj, g_final):
    row = lambda v: v.reshape(1, -1)
    layer_rows = lambda v: v.reshape(DEPTH, 1, -1)
    x, xg, inv = _embed(x_prompt.reshape(N_PROMPT, D_MODEL),
                       _time_major(x_sample).reshape(N_SAMPLE, D_MODEL), row(g_mix[0]))
    mixer_params = (conv_w, layer_rows(conv_b),
                    w_rg.astype(BF16), layer_rows(b_rg), w_ig.astype(BF16), layer_rows(b_ig),
                    layer_rows(lam), w_pool.astype(BF16), layer_rows(pool_scale))
    p_all = jnp.concatenate(
        [p_prompt.reshape(DEPTH, N_PROMPT, PLE_DIM),
         jnp.swapaxes(p_sample, 1, 2).reshape(DEPTH, N_SAMPLE, PLE_DIM)], axis=1).astype(BF16)
    conv_tm, pool_tm = jnp.swapaxes(state_conv, 1, 2), jnp.swapaxes(state_pool, 1, 2)
    hp_l, cp_l, pp_l = [], [], []
    sample_states = None
    for l in range(DEPTH):
        g_after_ple = row(g_mix[l + 1]) if l + 1 < DEPTH else row(g_final)

        z = _in_proj(xg, inv, w_in, l)

        y, hp, ct, pt = _prompt_mixer(z, l, *mixer_params)
        y, *sample_states = _sample_mixer(z, y, state_h, conv_tm, pool_tm, sample_states, l,
                                          *mixer_params)

        x, xg, inv = _out_proj(y, w_out, l, x, row(g_ffn[l]))
        h, wdb = _ffn_up(xg, inv, w_gate, w_up, w_down, l)
        x, xg, inv = _ffn_down(h, wdb, x, row(g_pe[l]))
        x, xg, inv = _ple(xg, inv, w_pe_gate, p_all, w_pe_proj, l, x, g_after_ple)

        hp_l.append(hp.reshape(BATCH, W_A))
        cp_l.append(ct[:, CONV_CARRY - (CONV_W - 1):])
        pp_l.append(pt[:, POOL_CARRY - POOL_PAD:])

    hs_all, cs_tm, ps_tm = sample_states
    gf = row(g_final)
    y_prompt = _final_norm(x, inv, gf, N_PROMPT, 0).reshape(BATCH, SEQ, D_MODEL)
    y_s = _final_norm(x, inv, gf, N_SAMPLE, N_PROMPT)
    y_sample = _time_major(y_s.reshape(DEC_SEQ, DEC_BATCH, D_MODEL))
    return (y_prompt, y_sample,
            jnp.stack(hp_l), jnp.stack(cp_l), jnp.stack(pp_l),
            hs_all, jnp.swapaxes(cs_tm, 1, 2), jnp.swapaxes(ps_tm, 1, 2))
```

```python
import functools

import jax
import jax.numpy as jnp
from jax import lax
from jax.experimental import pallas as pl
from jax.experimental.pallas import tpu as pltpu

D_MODEL = 4096
BATCH = 4
SEQ = 2048
DEPTH = 2
DEC_BATCH = 128
DEC_SEQ = 4
PAST_LEN = 16384
W_A = D_MODEL // 2
W_B = D_MODEL - W_A
LRU_HEAD = 256
N_LRU_HEADS = W_A // LRU_HEAD
LRU_C = 8.0
CONV_W = 4
POOL_WINDOWS = (2, 4, 8, 16)
N_POOL_GROUPS = len(POOL_WINDOWS)
POOL_GC = W_B // N_POOL_GROUPS
POOL_PAD = max(POOL_WINDOWS) - 1
D_FF = 11008
PLE_DIM = 256
EPS = 1e-6

N_PROMPT = BATCH * SEQ
N_SAMPLE = DEC_BATCH * DEC_SEQ
N_TOK = N_PROMPT + N_SAMPLE
Z_W = 2 * W_A + W_B

F32 = jnp.float32
BF16 = jnp.bfloat16

VMEM_LIMIT_BYTES = 60 * 1024 * 1024

TM = 1088
TM_UP = 2176
UP_CHUNK = 272
ROW_CHUNK = 272
TM_DOWN = 544
TM_EMBED = 256
TM_FINAL = 512
LANES = 128
TN = 512
TF = 256
TB = 256
CONV_CARRY = 8
POOL_CARRY = 16
SAMPLE_CB = 512


def _params(*sem):
    return pltpu.CompilerParams(dimension_semantics=sem, vmem_limit_bytes=VMEM_LIMIT_BYTES)


def _dot(a, b):
    return jnp.dot(a, b, preferred_element_type=F32)


def _snake(i, j, nj):
    return jnp.where(i % 2 == 0, j, nj - 1 - j)


def _row_map(i, j):
    return (i, 0)


def _tile_map(nj):
    return lambda i, j: (i, _snake(i, j, nj))


def _col_map(nj):
    return lambda i, j: (0, _snake(i, j, nj))


def _layer_col_map(l, nj):
    return lambda i, j: (l, 0, _snake(i, j, nj))


def _log_sigmoid(x):
    return jnp.minimum(x, 0.0) - jnp.log1p(jnp.exp(-jnp.abs(x)))


def _lru_coeffs(xc, r, i, logsig):
    log_a = r * (LRU_C * logsig)
    a = jnp.exp(log_a)
    u = 1.0 - a * a
    mult = jnp.where(u > 0.0, u * lax.rsqrt(u), 0.0)
    return a, mult, i * xc


def _lane_partial_sumsq(x):
    sq = x * x
    part = sq[:, 0:LANES]
    for c in range(1, x.shape[1] // LANES):
        part = part + sq[:, c * LANES:(c + 1) * LANES]
    return part


def _inv_rms_lanes(partial_sumsq):
    tot = jnp.sum(partial_sumsq, axis=-1, keepdims=True)
    return jnp.broadcast_to(lax.rsqrt(tot * (1.0 / D_MODEL) + EPS), partial_sumsq.shape)


def _scale_rows(acc, inv_ref):
    inv = inv_ref[...]
    return jnp.concatenate(
        [acc[:, c * LANES:(c + 1) * LANES] * inv for c in range(acc.shape[1] // LANES)], axis=1)


def _store_residual(x_new, g_ref, o_ref, xg_ref, inv_ref):
    j, last = pl.program_id(1), pl.num_programs(1) - 1
    parts = []
    for r0 in range(0, o_ref.shape[0], ROW_CHUNK):
        rows = slice(r0, r0 + ROW_CHUNK)
        xc = x_new(rows)
        o_ref[rows, :] = xc
        xg_ref[rows, :] = (xc * g_ref[...]).astype(BF16)
        parts.append(_lane_partial_sumsq(xc))
    part = jnp.concatenate(parts, axis=0)

    @pl.when(j == 0)
    def _():
        inv_ref[...] = part

    @pl.when(jnp.logical_and(j > 0, j < last))
    def _():
        inv_ref[...] += part

    @pl.when(j == last)
    def _():
        inv_ref[...] = _inv_rms_lanes(inv_ref[...] + part)


def _residual_out(m, n, tm):
    specs = [
        pl.BlockSpec((tm, TN), _tile_map(n // TN)),
        pl.BlockSpec((tm, TN), _tile_map(n // TN)),
        pl.BlockSpec((tm, LANES), _row_map),
    ]
    shapes = [
        jax.ShapeDtypeStruct((m, n), F32),
        jax.ShapeDtypeStruct((m, n), BF16),
        jax.ShapeDtypeStruct((m, LANES), F32),
    ]
    return specs, shapes


def _embed_kernel(xp_ref, xs_ref, g_ref, x_ref, xg_ref, inv_ref):
    def emit(x):
        x_ref[...] = x
        xg_ref[...] = (x * g_ref[...]).astype(BF16)
        inv_ref[...] = _inv_rms_lanes(_lane_partial_sumsq(x))

    is_prompt = pl.program_id(0) < N_PROMPT // TM_EMBED

    @pl.when(is_prompt)
    def _():
        emit(xp_ref[...])

    @pl.when(jnp.logical_not(is_prompt))
    def _():
        emit(xs_ref[...])


def _embed(xp, xs, g):
    k = xp.shape[1]
    npb = N_PROMPT // TM_EMBED
    rows = lambda i: (i, 0)
    return pl.pallas_call(
        _embed_kernel,
        grid=(N_TOK // TM_EMBED,),
        in_specs=[
            pl.BlockSpec((TM_EMBED, k), lambda i: (jnp.minimum(i, npb - 1), 0)),
            pl.BlockSpec((TM_EMBED, k), lambda i: (jnp.maximum(i - npb, 0), 0)),
            pl.BlockSpec((1, k), lambda i: (0, 0)),
        ],
        out_specs=[
            pl.BlockSpec((TM_EMBED, k), rows),
            pl.BlockSpec((TM_EMBED, k), rows),
            pl.BlockSpec((TM_EMBED, LANES), rows),
        ],
        out_shape=[
            jax.ShapeDtypeStruct((N_TOK, k), F32),
            jax.ShapeDtypeStruct((N_TOK, k), BF16),
            jax.ShapeDtypeStruct((N_TOK, LANES), F32),
        ],
        compiler_params=_params("arbitrary"),
        name="embed",
    )(xp, xs, g)


def _final_norm_kernel(x_ref, inv_ref, g_ref, o_ref):
    o_ref[...] = _scale_rows(x_ref[...], inv_ref) * g_ref[...]


def _final_norm(x, inv, g, rows, row0):
    k = x.shape[1]
    tm = TM_FINAL
    assert rows % tm == 0 and row0 % tm == 0
    rb0 = row0 // tm
    return pl.pallas_call(
        _final_norm_kernel,
        grid=(rows // tm,),
        in_specs=[pl.BlockSpec((tm, k), lambda i: (rb0 + i, 0)),
                  pl.BlockSpec((tm, LANES), lambda i: (rb0 + i, 0)),
                  pl.BlockSpec((1, k), lambda i: (0, 0))],
        out_specs=pl.BlockSpec((tm, k), lambda i: (i, 0)),
        out_shape=jax.ShapeDtypeStruct((rows, k), F32),
        compiler_params=_params("parallel"),
        name="final_norm",
    )(x, inv, g)


def _in_proj_kernel(xg_ref, inv_ref, w_ref, o_ref):
    jb = _snake(pl.program_id(0), pl.program_id(1), pl.num_programs(1))
    is_gate = jnp.logical_and(jb >= W_A // TN, jb < 2 * W_A // TN)

    def project(act):
        w = w_ref[...].astype(BF16)
        for r0 in range(0, TM, ROW_CHUNK):
            rows = slice(r0, r0 + ROW_CHUNK)
            o_ref[rows, :] = act(_scale_rows(_dot(xg_ref[rows, :], w), inv_ref.at[rows, :]))

    @pl.when(is_gate)
    def _():
        project(jax.nn.gelu)

    @pl.when(jnp.logical_not(is_gate))
    def _():
        project(lambda z: z)


def _in_proj(xg, inv, w_all, l):
    m, k = xg.shape
    n = w_all.shape[2]
    return pl.pallas_call(
        _in_proj_kernel,
        grid=(m // TM, n // TN),
        in_specs=[
            pl.BlockSpec((TM, k), _row_map),
            pl.BlockSpec((TM, LANES), _row_map),
            pl.BlockSpec((None, k, TN), _layer_col_map(l, n // TN)),
        ],
        out_specs=pl.BlockSpec((TM, TN), _tile_map(n // TN)),
        out_shape=jax.ShapeDtypeStruct((m, n), F32),
        compiler_params=_params("parallel", "arbitrary"),
        name="in_proj",
    )(xg, inv, w_all)


def _prompt_mixer_kernel(xa_ref, ga_ref, ub_ref, cw_ref, cb_ref, wr_ref, br_ref, wi_ref, bi_ref,
                         lam_ref, wp_ref, ps_ref,
                         y_ref, hl_ref, ct_ref, pt_ref,
                         hc_ref, cc_ref, pc_ref, a_ref, b_ref):
    t = pl.program_id(1)

    @pl.when(t == 0)
    def _():
        hc_ref[...] = jnp.zeros_like(hc_ref)
        cc_ref[...] = jnp.zeros_like(cc_ref)
        pc_ref[...] = jnp.zeros_like(pc_ref)

    pos = lax.broadcasted_iota(jnp.int32, (TB, 1), 0) + t * TB

    xa = xa_ref[...]
    ext = jnp.concatenate([cc_ref[...], xa], axis=0)
    xc = cb_ref[...] + cw_ref[CONV_W - 1:CONV_W, :] * xa
    for s in range(1, CONV_W):
        xc = xc + cw_ref[CONV_W - 1 - s:CONV_W - s, :] * pltpu.roll(ext, s, axis=0)[CONV_CARRY:]
    xcb = xc.astype(BF16)
    r_parts, i_parts = [], []
    for hd in range(N_LRU_HEADS):
        sl = slice(hd * LRU_HEAD, (hd + 1) * LRU_HEAD)
        r_parts.append(_dot(xcb[:, sl], wr_ref[hd]))
        i_parts.append(_dot(xcb[:, sl], wi_ref[hd]))
    r = jax.nn.sigmoid(jnp.concatenate(r_parts, axis=1) + br_ref[...])
    i = jax.nn.sigmoid(jnp.concatenate(i_parts, axis=1) + bi_ref[...])
    a, mult, ix = _lru_coeffs(xc, r, i, _log_sigmoid(lam_ref[...]))
    a_ref[...] = a
    b_ref[...] = mult * ix

    @pl.when(t == 0)
    def _():
        a_ref[0:1, :] = jnp.zeros((1, W_A), F32)
        b_ref[0:1, :] = ix[0:1, :]

    def step(row, h):
        h = a_ref[pl.ds(row, 1), :] * h + b_ref[pl.ds(row, 1), :]
        b_ref[pl.ds(row, 1), :] = h
        return h

    h_last = lax.fori_loop(0, TB, step, hc_ref[0:1, :], unroll=8)
    hc_ref[0:1, :] = h_last
    hl_ref[0] = h_last
    y_ref[:, 0:W_A] = (ga_ref[...] * b_ref[...]).astype(BF16)
    cc_ref[...] = xa[TB - CONV_CARRY:]
    ct_ref[0] = xa[TB - CONV_CARRY:]

    ub = ub_ref[...]
    pext = jnp.concatenate([pc_ref[...], ub], axis=0)
    for g, w in enumerate(POOL_WINDOWS):
        sl = slice(g * POOL_GC, (g + 1) * POOL_GC)
        s = pext[:, sl]
        shift = 1
        while shift < w:
            s = s + pltpu.roll(s, shift, axis=0)
            shift *= 2
        cnt = jnp.minimum(pos + 1, w).astype(F32)
        d = s[POOL_CARRY:] / cnt - ub[:, sl]
        yb = _dot(d.astype(BF16), wp_ref[g]) * ps_ref[:, sl]
        y_ref[:, W_A + g * POOL_GC:W_A + (g + 1) * POOL_GC] = yb.astype(BF16)
    pc_ref[...] = ub[TB - POOL_CARRY:]
    pt_ref[0] = ub[TB - POOL_CARRY:]


def _prompt_mixer(z, l, cw, cb, wr, br, wi, bi, lam, wp, ps):
    nt = SEQ // TB
    row = lambda b, t: b * nt + t
    full2 = lambda shape: pl.BlockSpec((None,) + shape, lambda b, t: (l, 0, 0))
    full3 = lambda shape: pl.BlockSpec((None,) + shape, lambda b, t: (l, 0, 0, 0))
    return pl.pallas_call(
        _prompt_mixer_kernel,
        grid=(BATCH, nt),
        in_specs=[
            pl.BlockSpec((TB, W_A), lambda b, t: (row(b, t), 0)),
            pl.BlockSpec((TB, W_A), lambda b, t: (row(b, t), 1)),
            pl.BlockSpec((TB, W_B), lambda b, t: (row(b, t), 2)),
            full2((CONV_W, W_A)), full2((1, W_A)),
            full3((N_LRU_HEADS, LRU_HEAD, LRU_HEAD)), full2((1, W_A)),
            full3((N_LRU_HEADS, LRU_HEAD, LRU_HEAD)), full2((1, W_A)),
            full2((1, W_A)),
            full3((N_POOL_GROUPS, POOL_GC, POOL_GC)), full2((1, W_B)),
        ],
        out_specs=[
            pl.BlockSpec((TB, D_MODEL), lambda b, t: (row(b, t), 0)),
            pl.BlockSpec((1, 1, W_A), lambda b, t: (b, 0, 0)),
            pl.BlockSpec((1, CONV_CARRY, W_A), lambda b, t: (b, 0, 0)),
            pl.BlockSpec((1, POOL_CARRY, W_B), lambda b, t: (b, 0, 0)),
        ],
        out_shape=[
            jax.ShapeDtypeStruct((N_TOK, D_MODEL), BF16),
            jax.ShapeDtypeStruct((BATCH, 1, W_A), F32),
            jax.ShapeDtypeStruct((BATCH, CONV_CARRY, W_A), F32),
            jax.ShapeDtypeStruct((BATCH, POOL_CARRY, W_B), F32),
        ],
        scratch_shapes=[
            pltpu.VMEM((8, W_A), F32),
            pltpu.VMEM((CONV_CARRY, W_A), F32),
            pltpu.VMEM((POOL_CARRY, W_B), F32),
            pltpu.VMEM((TB, W_A), F32),
            pltpu.VMEM((TB, W_A), F32),
        ],
        compiler_params=_params("parallel", "arbitrary"),
        name="prompt_mixer",
    )(z, z, z, cw, cb, wr, br, wi, bi, lam, wp, ps)


def _sample_mixer_kernel(xa_ref, ga_ref, ub_ref, h0_ref, cs_ref, pst_ref,
                         cw_ref, cb_ref, wr_ref, br_ref, wi_ref, bi_ref, lam_ref, wp_ref, ps_ref,
                         *aliased_and_out_refs):
    y_ref, hn_ref, cn_ref, pn_ref = aliased_and_out_refs[-4:]
    c = pl.program_id(0)
    heads = SAMPLE_CB // LRU_HEAD
    rows = lambda t: slice(t * DEC_BATCH, (t + 1) * DEC_BATCH)

    ext = [cs_ref[k] for k in range(CONV_W - 1)] + [xa_ref[rows(t), :] for t in range(DEC_SEQ)]
    logsig = _log_sigmoid(lam_ref[...])
    h = h0_ref[...]
    ya = []
    for t in range(DEC_SEQ):
        xc = cb_ref[...] + cw_ref[0:1, :] * ext[t]
        for k in range(1, CONV_W):
            xc = xc + cw_ref[k:k + 1, :] * ext[t + k]
        xcb = xc.astype(BF16)
        r_parts, i_parts = [], []
        for hd in range(heads):
            sl = slice(hd * LRU_HEAD, (hd + 1) * LRU_HEAD)
            r_parts.append(_dot(xcb[:, sl], wr_ref[hd]))
            i_parts.append(_dot(xcb[:, sl], wi_ref[hd]))
        r = jax.nn.sigmoid(jnp.concatenate(r_parts, axis=1) + br_ref[...])
        i = jax.nn.sigmoid(jnp.concatenate(i_parts, axis=1) + bi_ref[...])
        a, mult, ix = _lru_coeffs(xc, r, i, logsig)
        if PAST_LEN + t == 0:
            a, mult = jnp.zeros_like(a), jnp.ones_like(mult)
        h = a * h + mult * ix
        ya.append((ga_ref[rows(t), :] * h).astype(BF16))
    hn_ref[...] = h
    for k in range(CONV_W - 1):
        cn_ref[k] = ext[DEC_SEQ + k]

    for k in range(POOL_PAD):
        src = k + DEC_SEQ
        pn_ref[k] = pst_ref[src] if src < POOL_PAD else ub_ref[rows(src - POOL_PAD), :]

    for g, w in enumerate(POOL_WINDOWS):
        @pl.when(c == g)
        def _(g=g, w=w):
            pext = ([pst_ref[k] for k in range(POOL_PAD - w + 1, POOL_PAD)]
                    + [ub_ref[rows(t), :] for t in range(DEC_SEQ)])
            for t in range(DEC_SEQ):
                s = pext[t]
                for j in range(1, w):
                    s = s + pext[t + j]
                cnt = float(min(PAST_LEN + t + 1, w))
                d = s / cnt - pext[t + w - 1]
                yb = _dot(d.astype(BF16), wp_ref[0]) * ps_ref[...]
                y_ref[rows(t), g * SAMPLE_CB:(g + 1) * SAMPLE_CB] = ya[t]
                y_ref[rows(t), W_A + g * POOL_GC:W_A + (g + 1) * POOL_GC] = yb.astype(BF16)


def _sample_mixer(z, y, state_h, state_conv, state_pool, prev_states, l,
                  cw, cb, wr, br, wi, bi, lam, wp, ps):
    cbw = SAMPLE_CB
    na = W_A // cbw
    heads = cbw // LRU_HEAD
    rb = N_PROMPT // N_SAMPLE
    vec = pl.BlockSpec((None, 1, cbw), lambda c: (l, 0, c))
    state3 = lambda steps: pl.BlockSpec((None, steps, DEC_BATCH, cbw), lambda c: (l, 0, 0, c))
    state2 = pl.BlockSpec((None, DEC_BATCH, cbw), lambda c: (l, 0, c))
    aliased = (y,) + (() if prev_states is None else tuple(prev_states))
    n_in = 15
    return pl.pallas_call(
        _sample_mixer_kernel,
        grid=(na,),
        in_specs=[
            pl.BlockSpec((N_SAMPLE, cbw), lambda c: (rb, c)),
            pl.BlockSpec((N_SAMPLE, cbw), lambda c: (rb, na + c)),
            pl.BlockSpec((N_SAMPLE, cbw), lambda c: (rb, 2 * na + c)),
            state2,
            state3(CONV_W - 1),
            state3(POOL_PAD),
            pl.BlockSpec((None, CONV_W, cbw), lambda c: (l, 0, c)), vec,
            pl.BlockSpec((None, heads, LRU_HEAD, LRU_HEAD), lambda c: (l, c, 0, 0)), vec,
            pl.BlockSpec((None, heads, LRU_HEAD, LRU_HEAD), lambda c: (l, c, 0, 0)), vec,
            vec,
            pl.BlockSpec((None, 1, POOL_GC, POOL_GC), lambda c: (l, c, 0, 0)), vec,
        ] + [pl.BlockSpec(memory_space=pl.ANY)] * len(aliased),
        out_specs=[
            pl.BlockSpec((N_SAMPLE, D_MODEL), lambda c: (rb, 0)),
            state2,
            state3(CONV_W - 1),
            state3(POOL_PAD),
        ],
        out_shape=[
            jax.ShapeDtypeStruct((N_TOK, D_MODEL), BF16),
            jax.ShapeDtypeStruct((DEPTH, DEC_BATCH, W_A), F32),
            jax.ShapeDtypeStruct((DEPTH, CONV_W - 1, DEC_BATCH, W_A), F32),
            jax.ShapeDtypeStruct((DEPTH, POOL_PAD, DEC_BATCH, W_B), F32),
        ],
        input_output_aliases={n_in + a: a for a in range(len(aliased))},
        compiler_params=_params("arbitrary"),
        name="sample_mixer",
    )(z, z, z, state_h, state_conv, state_pool, cw, cb, wr, br, wi, bi, lam, wp, ps, *aliased)


def _out_proj_kernel(y_ref, w_ref, x_ref, g_ref, o_ref, xg_ref, inv_ref):
    w = w_ref[...].astype(BF16)
    x_new = lambda rows: x_ref[rows, :] + _dot(y_ref[rows, :], w)
    _store_residual(x_new, g_ref, o_ref, xg_ref, inv_ref)


def _out_proj(y, w_all, l, x, g_next):
    m, k = y.shape
    n = w_all.shape[2]
    out_specs, out_shape = _residual_out(m, n, TM)
    return pl.pallas_call(
        _out_proj_kernel,
        grid=(m // TM, n // TN),
        in_specs=[
            pl.BlockSpec((TM, k), _row_map),
            pl.BlockSpec((None, k, TN), _layer_col_map(l, n // TN)),
            pl.BlockSpec((TM, TN), _tile_map(n // TN)),
            pl.BlockSpec((1, TN), _col_map(n // TN)),
        ],
        out_specs=out_specs,
        out_shape=out_shape,
        compiler_params=_params("parallel", "arbitrary"),
        name="out_proj",
    )(y, w_all, x, g_next)


def _ffn_up_kernel(xg_ref, inv_ref, wg_ref, wu_ref, wd_ref, h_ref, wdb_ref):
    wg = wg_ref[...].astype(BF16)
    wu = wu_ref[...].astype(BF16)
    for r0 in range(0, TM_UP, UP_CHUNK):
        rows = slice(r0, r0 + UP_CHUNK)
        xg = xg_ref[rows, :]
        inv = inv_ref.at[rows, :]
        gate = _scale_rows(_dot(xg, wg), inv)
        up = _scale_rows(_dot(xg, wu), inv)
        h_ref[rows, :] = (jax.nn.silu(gate) * up).astype(BF16)
    wdb_ref[...] = wd_ref[...].astype(BF16)


def _ffn_up(xg, inv, wg_all, wu_all, wd_all, l):
    m, k = xg.shape
    nf = D_FF // TF
    steps = (m // TM_UP) * nf
    wd_rows = D_FF // steps
    assert wd_rows * steps == D_FF and wd_rows % 16 == 0
    return pl.pallas_call(
        _ffn_up_kernel,
        grid=(m // TM_UP, nf),
        in_specs=[
            pl.BlockSpec((TM_UP, k), _row_map, pipeline_mode=pl.Buffered(1)),
            pl.BlockSpec((TM_UP, LANES), _row_map),
            pl.BlockSpec((None, k, TF), _layer_col_map(l, nf)),
            pl.BlockSpec((None, k, TF), _layer_col_map(l, nf)),
            pl.BlockSpec((None, wd_rows, k), lambda i, j: (l, i * nf + j, 0)),
        ],
        out_specs=[
            pl.BlockSpec((TM_UP, TF), _tile_map(nf)),
            pl.BlockSpec((wd_rows, k), lambda i, j: (i * nf + j, 0)),
        ],
        out_shape=[
            jax.ShapeDtypeStruct((m, D_FF), BF16),
            jax.ShapeDtypeStruct((D_FF, k), BF16),
        ],
        compiler_params=_params("arbitrary", "arbitrary"),
        name="ffn_up",
    )(xg, inv, wg_all, wu_all, wd_all)


def _ffn_down_kernel(h_ref, w_ref, x_ref, g_ref, o_ref, xg_ref, inv_ref):
    x_new = lambda rows: x_ref[rows, :] + _dot(h_ref[rows, :], w_ref[...])
    _store_residual(x_new, g_ref, o_ref, xg_ref, inv_ref)


def _ffn_down(h, wdb, x, g_next):
    m, kf = h.shape
    n = wdb.shape[1]
    out_specs, out_shape = _residual_out(m, n, TM_DOWN)
    return pl.pallas_call(
        _ffn_down_kernel,
        grid=(m // TM_DOWN, n // TN),
        in_specs=[
            pl.BlockSpec((TM_DOWN, kf), _row_map),
            pl.BlockSpec((kf, TN), _col_map(n // TN)),
            pl.BlockSpec((TM_DOWN, TN), _tile_map(n // TN)),
            pl.BlockSpec((1, TN), _col_map(n // TN)),
        ],
        out_specs=out_specs,
        out_shape=out_shape,
        compiler_params=_params("parallel", "arbitrary"),
        name="ffn_down",
    )(h, wdb, x, g_next)


def _ple_kernel(xg_ref, inv_ref, wg_ref, p_ref, wp_ref, x_ref, g_ref, o_ref, xgn_ref, invn_ref):
    wg = wg_ref[...].astype(BF16)
    wp = wp_ref[...].astype(BF16)

    def x_new(rows):
        gate = jax.nn.sigmoid(_scale_rows(_dot(xg_ref[rows, :], wg), inv_ref.at[rows, :]))
        return x_ref[rows, :] + gate * _dot(p_ref[rows, :], wp)

    _store_residual(x_new, g_ref, o_ref, xgn_ref, invn_ref)


def _ple(xg, inv, wg_all, p, wp_all, l, x, g_next):
    m, k = xg.shape
    out_specs, out_shape = _residual_out(m, k, TM)
    return pl.pallas_call(
        _ple_kernel,
        grid=(m // TM, k // TN),
        in_specs=[
            pl.BlockSpec((TM, k), _row_map),
            pl.BlockSpec((TM, LANES), _row_map),
            pl.BlockSpec((None, k, TN), _layer_col_map(l, k // TN)),
            pl.BlockSpec((None, TM, PLE_DIM), lambda i, j: (l, i, 0)),
            pl.BlockSpec((None, PLE_DIM, TN), _layer_col_map(l, k // TN)),
            pl.BlockSpec((TM, TN), _tile_map(k // TN)),
            pl.BlockSpec((1, TN), _col_map(k // TN)),
        ],
        out_specs=out_specs,
        out_shape=out_shape,
        compiler_params=_params("parallel", "arbitrary"),
        name="ple",
    )(xg, inv, wg_all, p, wp_all, x, g_next)


def _time_major(a):
    return jnp.swapaxes(a, 0, 1)


def kernel(x_prompt, x_sample, state_h, state_conv, state_pool, p_prompt, p_sample, g_mix, w_in, conv_w, conv_b, w_rg, b_rg, w_ig, b_ig, lam, w_pool, pool_scale, w_out, g_ffn, w_gate, w_up, w_down, g_pe, w_pe_gate, w_pe_proj, g_final):
    row = lambda v: v.reshape(1, -1)
    layer_rows = lambda v: v.reshape(DEPTH, 1, -1)
    x, xg, inv = _embed(x_prompt.reshape(N_PROMPT, D_MODEL),
                       _time_major(x_sample).reshape(N_SAMPLE, D_MODEL), row(g_mix[0]))
    mixer_params = (conv_w, layer_rows(conv_b),
                    w_rg.astype(BF16), layer_rows(b_rg), w_ig.astype(BF16), layer_rows(b_ig),
                    layer_rows(lam), w_pool.astype(BF16), layer_rows(pool_scale))
    p_all = jnp.concatenate(
        [p_prompt.reshape(DEPTH, N_PROMPT, PLE_DIM),
         jnp.swapaxes(p_sample, 1, 2).reshape(DEPTH, N_SAMPLE, PLE_DIM)], axis=1).astype(BF16)
    conv_tm, pool_tm = jnp.swapaxes(state_conv, 1, 2), jnp.swapaxes(state_pool, 1, 2)
    hp_l, cp_l, pp_l = [], [], []
    sample_states = None
    for l in range(DEPTH):
        g_after_ple = row(g_mix[l + 1]) if l + 1 < DEPTH else row(g_final)

        z = _in_proj(xg, inv, w_in, l)

        y, hp, ct, pt = _prompt_mixer(z, l, *mixer_params)
        y, *sample_states = _sample_mixer(z, y, state_h, conv_tm, pool_tm, sample_states, l,
                                          *mixer_params)

        x, xg, inv = _out_proj(y, w_out, l, x, row(g_ffn[l]))
        h, wdb = _ffn_up(xg, inv, w_gate, w_up, w_down, l)
        x, xg, inv = _ffn_down(h, wdb, x, row(g_pe[l]))
        x, xg, inv = _ple(xg, inv, w_pe_gate, p_all, w_pe_proj, l, x, g_after_ple)

        hp_l.append(hp.reshape(BATCH, W_A))
        cp_l.append(ct[:, CONV_CARRY - (CONV_W - 1):])
        pp_l.append(pt[:, POOL_CARRY - POOL_PAD:])

    hs_all, cs_tm, ps_tm = sample_states
    gf = row(g_final)
    y_prompt = _final_norm(x, inv, gf, N_PROMPT, 0).reshape(BATCH, SEQ, D_MODEL)
    y_s = _final_norm(x, inv, gf, N_SAMPLE, N_PROMPT)
    y_sample = _time_major(y_s.reshape(DEC_SEQ, DEC_BATCH, D_MODEL))
    return (y_prompt, y_sample,
            jnp.stack(hp_l), jnp.stack(cp_l), jnp.stack(pp_l),
            hs_all, jnp.swapaxes(cs_tm, 1, 2), jnp.swapaxes(ps_tm, 1, 2))
```

```python
import functools

import jax
import jax.numpy as jnp
from jax import lax
from jax.experimental import pallas as pl
from jax.experimental.pallas import tpu as pltpu

D_MODEL = 4096
BATCH = 4
SEQ = 2048
DEPTH = 2
DEC_BATCH = 128
DEC_SEQ = 4
PAST_LEN = 16384
W_A = D_MODEL // 2
W_B = D_MODEL - W_A
LRU_HEAD = 256
N_LRU_HEADS = W_A // LRU_HEAD
LRU_C = 8.0
CONV_W = 4
POOL_WINDOWS = (2, 4, 8, 16)
N_POOL_GROUPS = len(POOL_WINDOWS)
POOL_GC = W_B // N_POOL_GROUPS
POOL_PAD = max(POOL_WINDOWS) - 1
D_FF = 11008
PLE_DIM = 256
EPS = 1e-6

N_PROMPT = BATCH * SEQ
N_SAMPLE = DEC_BATCH * DEC_SEQ
N_TOK = N_PROMPT + N_SAMPLE
Z_W = 2 * W_A + W_B

F32 = jnp.float32
BF16 = jnp.bfloat16

VMEM_LIMIT_BYTES = 60 * 1024 * 1024

TM = 1088
TM_UP = 2176
UP_CHUNK = 272
ROW_CHUNK = 272
TM_DOWN = 544
TM_EMBED = 256
TM_FINAL = 512
LANES = 128
TN = 512
TF = 256
TB = 512
CONV_CARRY = 8
POOL_CARRY = 16
SAMPLE_CB = 512


def _params(*sem):
    return pltpu.CompilerParams(dimension_semantics=sem, vmem_limit_bytes=VMEM_LIMIT_BYTES)


def _dot(a, b):
    return jnp.dot(a, b, preferred_element_type=F32)


def _snake(i, j, nj):
    return jnp.where(i % 2 == 0, j, nj - 1 - j)


def _row_map(i, j):
    return (i, 0)


def _tile_map(nj):
    return lambda i, j: (i, _snake(i, j, nj))


def _col_map(nj):
    return lambda i, j: (0, _snake(i, j, nj))


def _layer_col_map(l, nj):
    return lambda i, j: (l, 0, _snake(i, j, nj))


def _log_sigmoid(x):
    return jnp.minimum(x, 0.0) - jnp.log1p(jnp.exp(-jnp.abs(x)))


def _lru_coeffs(xc, r, i, logsig):
    log_a = r * (LRU_C * logsig)
    a = jnp.exp(log_a)
    u = 1.0 - a * a
    mult = jnp.where(u > 0.0, u * lax.rsqrt(u), 0.0)
    return a, mult, i * xc


def _lane_partial_sumsq(x):
    sq = x * x
    part = sq[:, 0:LANES]
    for c in range(1, x.shape[1] // LANES):
        part = part + sq[:, c * LANES:(c + 1) * LANES]
    return part


def _inv_rms_lanes(partial_sumsq):
    tot = jnp.sum(partial_sumsq, axis=-1, keepdims=True)
    return jnp.broadcast_to(lax.rsqrt(tot * (1.0 / D_MODEL) + EPS), partial_sumsq.shape)


def _scale_rows(acc, inv_ref):
    inv = inv_ref[...]
    return jnp.concatenate(
        [acc[:, c * LANES:(c + 1) * LANES] * inv for c in range(acc.shape[1] // LANES)], axis=1)


def _store_residual(x_new, g_ref, o_ref, xg_ref, inv_ref):
    j, last = pl.program_id(1), pl.num_programs(1) - 1
    parts = []
    for r0 in range(0, o_ref.shape[0], ROW_CHUNK):
        rows = slice(r0, r0 + ROW_CHUNK)
        xc = x_new(rows)
        o_ref[rows, :] = xc
        xg_ref[rows, :] = (xc * g_ref[...]).astype(BF16)
        parts.append(_lane_partial_sumsq(xc))
    part = jnp.concatenate(parts, axis=0)

    @pl.when(j == 0)
    def _():
        inv_ref[...] = part

    @pl.when(jnp.logical_and(j > 0, j < last))
    def _():
        inv_ref[...] += part

    @pl.when(j == last)
    def _():
        inv_ref[...] = _inv_rms_lanes(inv_ref[...] + part)


def _residual_out(m, n, tm):
    specs = [
        pl.BlockSpec((tm, TN), _tile_map(n // TN)),
        pl.BlockSpec((tm, TN), _tile_map(n // TN)),
        pl.BlockSpec((tm, LANES), _row_map),
    ]
    shapes = [
        jax.ShapeDtypeStruct((m, n), F32),
        jax.ShapeDtypeStruct((m, n), BF16),
        jax.ShapeDtypeStruct((m, LANES), F32),
    ]
    return specs, shapes


def _embed_kernel(xp_ref, xs_ref, g_ref, x_ref, xg_ref, inv_ref):
    def emit(x):
        x_ref[...] = x
        xg_ref[...] = (x * g_ref[...]).astype(BF16)
        inv_ref[...] = _inv_rms_lanes(_lane_partial_sumsq(x))

    is_prompt = pl.program_id(0) < N_PROMPT // TM_EMBED

    @pl.when(is_prompt)
    def _():
        emit(xp_ref[...])

    @pl.when(jnp.logical_not(is_prompt))
    def _():
        emit(xs_ref[...])


def _embed(xp, xs, g):
    k = xp.shape[1]
    npb = N_PROMPT // TM_EMBED
    rows = lambda i: (i, 0)
    return pl.pallas_call(
        _embed_kernel,
        grid=(N_TOK // TM_EMBED,),
        in_specs=[
            pl.BlockSpec((TM_EMBED, k), lambda i: (jnp.minimum(i, npb - 1), 0)),
            pl.BlockSpec((TM_EMBED, k), lambda i: (jnp.maximum(i - npb, 0), 0)),
            pl.BlockSpec((1, k), lambda i: (0, 0)),
        ],
        out_specs=[
            pl.BlockSpec((TM_EMBED, k), rows),
            pl.BlockSpec((TM_EMBED, k), rows),
            pl.BlockSpec((TM_EMBED, LANES), rows),
        ],
        out_shape=[
            jax.ShapeDtypeStruct((N_TOK, k), F32),
            jax.ShapeDtypeStruct((N_TOK, k), BF16),
            jax.ShapeDtypeStruct((N_TOK, LANES), F32),
        ],
        compiler_params=_params("arbitrary"),
        name="embed",
    )(xp, xs, g)


def _final_norm_kernel(x_ref, inv_ref, g_ref, o_ref):
    o_ref[...] = _scale_rows(x_ref[...], inv_ref) * g_ref[...]


def _final_norm(x, inv, g, rows, row0):
    k = x.shape[1]
    tm = TM_FINAL
    assert rows % tm == 0 and row0 % tm == 0
    rb0 = row0 // tm
    return pl.pallas_call(
        _final_norm_kernel,
        grid=(rows // tm,),
        in_specs=[pl.BlockSpec((tm, k), lambda i: (rb0 + i, 0)),
                  pl.BlockSpec((tm, LANES), lambda i: (rb0 + i, 0)),
                  pl.BlockSpec((1, k), lambda i: (0, 0))],
        out_specs=pl.BlockSpec((tm, k), lambda i: (i, 0)),
        out_shape=jax.ShapeDtypeStruct((rows, k), F32),
        compiler_params=_params("parallel"),
        name="final_norm",
    )(x, inv, g)


def _in_proj_kernel(xg_ref, inv_ref, w_ref, o_ref):
    jb = _snake(pl.program_id(0), pl.program_id(1), pl.num_programs(1))
    is_gate = jnp.logical_and(jb >= W_A // TN, jb < 2 * W_A // TN)

    def project(act):
        w = w_ref[...].astype(BF16)
        for r0 in range(0, TM, ROW_CHUNK):
            rows = slice(r0, r0 + ROW_CHUNK)
            o_ref[rows, :] = act(_scale_rows(_dot(xg_ref[rows, :], w), inv_ref.at[rows, :]))

    @pl.when(is_gate)
    def _():
        project(jax.nn.gelu)

    @pl.when(jnp.logical_not(is_gate))
    def _():
        project(lambda z: z)


def _in_proj(xg, inv, w_all, l):
    m, k = xg.shape
    n = w_all.shape[2]
    return pl.pallas_call(
        _in_proj_kernel,
        grid=(m // TM, n // TN),
        in_specs=[
            pl.BlockSpec((TM, k), _row_map),
            pl.BlockSpec((TM, LANES), _row_map),
            pl.BlockSpec((None, k, TN), _layer_col_map(l, n // TN)),
        ],
        out_specs=pl.BlockSpec((TM, TN), _tile_map(n // TN)),
        out_shape=jax.ShapeDtypeStruct((m, n), F32),
        compiler_params=_params("parallel", "arbitrary"),
        name="in_proj",
    )(xg, inv, w_all)


def _prompt_mixer_kernel(xa_ref, ga_ref, ub_ref, cw_ref, cb_ref, wr_ref, br_ref, wi_ref, bi_ref,
                         lam_ref, wp_ref, ps_ref,
                         y_ref, hl_ref, ct_ref, pt_ref,
                         hc_ref, cc_ref, pc_ref, a_ref, b_ref):
    t = pl.program_id(1)

    @pl.when(t == 0)
    def _():
        hc_ref[...] = jnp.zeros_like(hc_ref)
        cc_ref[...] = jnp.zeros_like(cc_ref)
        pc_ref[...] = jnp.zeros_like(pc_ref)

    pos = lax.broadcasted_iota(jnp.int32, (TB, 1), 0) + t * TB

    xa = xa_ref[...]
    ext = jnp.concatenate([cc_ref[...], xa], axis=0)
    xc = cb_ref[...] + cw_ref[CONV_W - 1:CONV_W, :] * xa
    for s in range(1, CONV_W):
        xc = xc + cw_ref[CONV_W - 1 - s:CONV_W - s, :] * pltpu.roll(ext, s, axis=0)[CONV_CARRY:]
    xcb = xc.astype(BF16)
    r_parts, i_parts = [], []
    for hd in range(N_LRU_HEADS):
        sl = slice(hd * LRU_HEAD, (hd + 1) * LRU_HEAD)
        r_parts.append(_dot(xcb[:, sl], wr_ref[hd]))
        i_parts.append(_dot(xcb[:, sl], wi_ref[hd]))
    r = jax.nn.sigmoid(jnp.concatenate(r_parts, axis=1) + br_ref[...])
    i = jax.nn.sigmoid(jnp.concatenate(i_parts, axis=1) + bi_ref[...])
    a, mult, ix = _lru_coeffs(xc, r, i, _log_sigmoid(lam_ref[...]))
    a_ref[...] = a
    b_ref[...] = mult * ix

    @pl.when(t == 0)
    def _():
        a_ref[0:1, :] = jnp.zeros((1, W_A), F32)
        b_ref[0:1, :] = ix[0:1, :]

    def step(row, h):
        h = a_ref[pl.ds(row, 1), :] * h + b_ref[pl.ds(row, 1), :]
        b_ref[pl.ds(row, 1), :] = h
        return h

    h_last = lax.fori_loop(0, TB, step, hc_ref[0:1, :], unroll=8)
    hc_ref[0:1, :] = h_last
    hl_ref[0] = h_last
    y_ref[:, 0:W_A] = (ga_ref[...] * b_ref[...]).astype(BF16)
    cc_ref[...] = xa[TB - CONV_CARRY:]
    ct_ref[0] = xa[TB - CONV_CARRY:]

    ub = ub_ref[...]
    pext = jnp.concatenate([pc_ref[...], ub], axis=0)
    for g, w in enumerate(POOL_WINDOWS):
        sl = slice(g * POOL_GC, (g + 1) * POOL_GC)
        s = pext[:, sl]
        shift = 1
        while shift < w:
            s = s + pltpu.roll(s, shift, axis=0)
            shift *= 2
        cnt = jnp.minimum(pos + 1, w).astype(F32)
        d = s[POOL_CARRY:] / cnt - ub[:, sl]
        yb = _dot(d.astype(BF16), wp_ref[g]) * ps_ref[:, sl]
        y_ref[:, W_A + g * POOL_GC:W_A + (g + 1) * POOL_GC] = yb.astype(BF16)
    pc_ref[...] = ub[TB - POOL_CARRY:]
    pt_ref[0] = ub[TB - POOL_CARRY:]


def _prompt_mixer(z, l, cw, cb, wr, br, wi, bi, lam, wp, ps):
    nt = SEQ // TB
    row = lambda b, t: b * nt + t
    full2 = lambda shape: pl.BlockSpec((None,) + shape, lambda b, t: (l, 0, 0))
    full3 = lambda shape: pl.BlockSpec((None,) + shape, lambda b, t: (l, 0, 0, 0))
    return pl.pallas_call(
        _prompt_mixer_kernel,
        grid=(BATCH, nt),
        in_specs=[
            pl.BlockSpec((TB, W_A), lambda b, t: (row(b, t), 0)),
            pl.BlockSpec((TB, W_A), lambda b, t: (row(b, t), 1)),
            pl.BlockSpec((TB, W_B), lambda b, t: (row(b, t), 2)),
            full2((CONV_W, W_A)), full2((1, W_A)),
            full3((N_LRU_HEADS, LRU_HEAD, LRU_HEAD)), full2((1, W_A)),
            full3((N_LRU_HEADS, LRU_HEAD, LRU_HEAD)), full2((1, W_A)),
            full2((1, W_A)),
            full3((N_POOL_GROUPS, POOL_GC, POOL_GC)), full2((1, W_B)),
        ],
        out_specs=[
            pl.BlockSpec((TB, D_MODEL), lambda b, t: (row(b, t), 0)),
            pl.BlockSpec((1, 1, W_A), lambda b, t: (b, 0, 0)),
            pl.BlockSpec((1, CONV_CARRY, W_A), lambda b, t: (b, 0, 0)),
            pl.BlockSpec((1, POOL_CARRY, W_B), lambda b, t: (b, 0, 0)),
        ],
        out_shape=[
            jax.ShapeDtypeStruct((N_TOK, D_MODEL), BF16),
            jax.ShapeDtypeStruct((BATCH, 1, W_A), F32),
            jax.ShapeDtypeStruct((BATCH, CONV_CARRY, W_A), F32),
            jax.ShapeDtypeStruct((BATCH, POOL_CARRY, W_B), F32),
        ],
        scratch_shapes=[
            pltpu.VMEM((8, W_A), F32),
            pltpu.VMEM((CONV_CARRY, W_A), F32),
            pltpu.VMEM((POOL_CARRY, W_B), F32),
            pltpu.VMEM((TB, W_A), F32),
            pltpu.VMEM((TB, W_A), F32),
        ],
        compiler_params=_params("parallel", "arbitrary"),
        name="prompt_mixer",
    )(z, z, z, cw, cb, wr, br, wi, bi, lam, wp, ps)


def _sample_mixer_kernel(xa_ref, ga_ref, ub_ref, h0_ref, cs_ref, pst_ref,
                         cw_ref, cb_ref, wr_ref, br_ref, wi_ref, bi_ref, lam_ref, wp_ref, ps_ref,
                         *aliased_and_out_refs):
    y_ref, hn_ref, cn_ref, pn_ref = aliased_and_out_refs[-4:]
    c = pl.program_id(0)
    heads = SAMPLE_CB // LRU_HEAD
    rows = lambda t: slice(t * DEC_BATCH, (t + 1) * DEC_BATCH)

    ext = [cs_ref[k] for k in range(CONV_W - 1)] + [xa_ref[rows(t), :] for t in range(DEC_SEQ)]
    logsig = _log_sigmoid(lam_ref[...])
    h = h0_ref[...]
    ya = []
    for t in range(DEC_SEQ):
        xc = cb_ref[...] + cw_ref[0:1, :] * ext[t]
        for k in range(1, CONV_W):
            xc = xc + cw_ref[k:k + 1, :] * ext[t + k]
        xcb = xc.astype(BF16)
        r_parts, i_parts = [], []
        for hd in range(heads):
            sl = slice(hd * LRU_HEAD, (hd + 1) * LRU_HEAD)
            r_parts.append(_dot(xcb[:, sl], wr_ref[hd]))
            i_parts.append(_dot(xcb[:, sl], wi_ref[hd]))
        r = jax.nn.sigmoid(jnp.concatenate(r_parts, axis=1) + br_ref[...])
        i = jax.nn.sigmoid(jnp.concatenate(i_parts, axis=1) + bi_ref[...])
        a, mult, ix = _lru_coeffs(xc, r, i, logsig)
        if PAST_LEN + t == 0:
            a, mult = jnp.zeros_like(a), jnp.ones_like(mult)
        h = a * h + mult * ix
        ya.append((ga_ref[rows(t), :] * h).astype(BF16))
    hn_ref[...] = h
    for k in range(CONV_W - 1):
        cn_ref[k] = ext[DEC_SEQ + k]

    for k in range(POOL_PAD):
        src = k + DEC_SEQ
        pn_ref[k] = pst_ref[src] if src < POOL_PAD else ub_ref[rows(src - POOL_PAD), :]

    for g, w in enumerate(POOL_WINDOWS):
        @pl.when(c == g)
        def _(g=g, w=w):
            pext = ([pst_ref[k] for k in range(POOL_PAD - w + 1, POOL_PAD)]
                    + [ub_ref[rows(t), :] for t in range(DEC_SEQ)])
            for t in range(DEC_SEQ):
                s = pext[t]
                for j in range(1, w):
                    s = s + pext[t + j]
                cnt = float(min(PAST_LEN + t + 1, w))
                d = s / cnt - pext[t + w - 1]
                yb = _dot(d.astype(BF16), wp_ref[0]) * ps_ref[...]
                y_ref[rows(t), g * SAMPLE_CB:(g + 1) * SAMPLE_CB] = ya[t]
                y_ref[rows(t), W_A + g * POOL_GC:W_A + (g + 1) * POOL_GC] = yb.astype(BF16)


def _sample_mixer(z, y, state_h, state_conv, state_pool, prev_states, l,
                  cw, cb, wr, br, wi, bi, lam, wp, ps):
    cbw = SAMPLE_CB
    na = W_A // cbw
    heads = cbw // LRU_HEAD
    rb = N_PROMPT // N_SAMPLE
    vec = pl.BlockSpec((None, 1, cbw), lambda c: (l, 0, c))
    state3 = lambda steps: pl.BlockSpec((None, steps, DEC_BATCH, cbw), lambda c: (l, 0, 0, c))
    state2 = pl.BlockSpec((None, DEC_BATCH, cbw), lambda c: (l, 0, c))
    aliased = (y,) + (() if prev_states is None else tuple(prev_states))
    n_in = 15
    return pl.pallas_call(
        _sample_mixer_kernel,
        grid=(na,),
        in_specs=[
            pl.BlockSpec((N_SAMPLE, cbw), lambda c: (rb, c)),
            pl.BlockSpec((N_SAMPLE, cbw), lambda c: (rb, na + c)),
            pl.BlockSpec((N_SAMPLE, cbw), lambda c: (rb, 2 * na + c)),
            state2,
            state3(CONV_W - 1),
            state3(POOL_PAD),
            pl.BlockSpec((None, CONV_W, cbw), lambda c: (l, 0, c)), vec,
            pl.BlockSpec((None, heads, LRU_HEAD, LRU_HEAD), lambda c: (l, c, 0, 0)), vec,
            pl.BlockSpec((None, heads, LRU_HEAD, LRU_HEAD), lambda c: (l, c, 0, 0)), vec,
            vec,
            pl.BlockSpec((None, 1, POOL_GC, POOL_GC), lambda c: (l, c, 0, 0)), vec,
        ] + [pl.BlockSpec(memory_space=pl.ANY)] * len(aliased),
        out_specs=[
            pl.BlockSpec((N_SAMPLE, D_MODEL), lambda c: (rb, 0)),
            state2,
            state3(CONV_W - 1),
            state3(POOL_PAD),
        ],
        out_shape=[
            jax.ShapeDtypeStruct((N_TOK, D_MODEL), BF16),
            jax.ShapeDtypeStruct((DEPTH, DEC_BATCH, W_A), F32),
            jax.ShapeDtypeStruct((DEPTH, CONV_W - 1, DEC_BATCH, W_A), F32),
            jax.ShapeDtypeStruct((DEPTH, POOL_PAD, DEC_BATCH, W_B), F32),
        ],
        input_output_aliases={n_in + a: a for a in range(len(aliased))},
        compiler_params=_params("arbitrary"),
        name="sample_mixer",
    )(z, z, z, state_h, state_conv, state_pool, cw, cb, wr, br, wi, bi, lam, wp, ps, *aliased)


def _out_proj_kernel(y_ref, w_ref, x_ref, g_ref, o_ref, xg_ref, inv_ref):
    w = w_ref[...].astype(BF16)
    x_new = lambda rows: x_ref[rows, :] + _dot(y_ref[rows, :], w)
    _store_residual(x_new, g_ref, o_ref, xg_ref, inv_ref)


def _out_proj(y, w_all, l, x, g_next):
    m, k = y.shape
    n = w_all.shape[2]
    out_specs, out_shape = _residual_out(m, n, TM)
    return pl.pallas_call(
        _out_proj_kernel,
        grid=(m // TM, n // TN),
        in_specs=[
            pl.BlockSpec((TM, k), _row_map),
            pl.BlockSpec((None, k, TN), _layer_col_map(l, n // TN)),
            pl.BlockSpec((TM, TN), _tile_map(n // TN)),
            pl.BlockSpec((1, TN), _col_map(n // TN)),
        ],
        out_specs=out_specs,
        out_shape=out_shape,
        compiler_params=_params("parallel", "arbitrary"),
        name="out_proj",
    )(y, w_all, x, g_next)


def _ffn_up_kernel(xg_ref, inv_ref, wg_ref, wu_ref, wd_ref, h_ref, wdb_ref):
    wg = wg_ref[...].astype(BF16)
    wu = wu_ref[...].astype(BF16)
    for r0 in range(0, TM_UP, UP_CHUNK):
        rows = slice(r0, r0 + UP_CHUNK)
        xg = xg_ref[rows, :]
        inv = inv_ref.at[rows, :]
        gate = _scale_rows(_dot(xg, wg), inv)
        up = _scale_rows(_dot(xg, wu), inv)
        h_ref[rows, :] = (jax.nn.silu(gate) * up).astype(BF16)
    wdb_ref[...] = wd_ref[...].astype(BF16)


def _ffn_up(xg, inv, wg_all, wu_all, wd_all, l):
    m, k = xg.shape
    nf = D_FF // TF
    steps = (m // TM_UP) * nf
    wd_rows = D_FF // steps
    assert wd_rows * steps == D_FF and wd_rows % 16 == 0
    return pl.pallas_call(
        _ffn_up_kernel,
        grid=(m // TM_UP, nf),
        in_specs=[
            pl.BlockSpec((TM_UP, k), _row_map, pipeline_mode=pl.Buffered(1)),
            pl.BlockSpec((TM_UP, LANES), _row_map),
            pl.BlockSpec((None, k, TF), _layer_col_map(l, nf)),
            pl.BlockSpec((None, k, TF), _layer_col_map(l, nf)),
            pl.BlockSpec((None, wd_rows, k), lambda i, j: (l, i * nf + j, 0)),
        ],
        out_specs=[
            pl.BlockSpec((TM_UP, TF), _tile_map(nf)),
            pl.BlockSpec((wd_rows, k), lambda i, j: (i * nf + j, 0)),
        ],
        out_shape=[
            jax.ShapeDtypeStruct((m, D_FF), BF16),
            jax.ShapeDtypeStruct((D_FF, k), BF16),
        ],
        compiler_params=_params("arbitrary", "arbitrary"),
        name="ffn_up",
    )(xg, inv, wg_all, wu_all, wd_all)


def _ffn_down_kernel(h_ref, w_ref, x_ref, g_ref, o_ref, xg_ref, inv_ref):
    x_new = lambda rows: x_ref[rows, :] + _dot(h_ref[rows, :], w_ref[...])
    _store_residual(x_new, g_ref, o_ref, xg_ref, inv_ref)


def _ffn_down(h, wdb, x, g_next):
    m, kf = h.shape
    n = wdb.shape[1]
    out_specs, out_shape = _residual_out(m, n, TM_DOWN)
    return pl.pallas_call(
        _ffn_down_kernel,
        grid=(m // TM_DOWN, n // TN),
        in_specs=[
            pl.BlockSpec((TM_DOWN, kf), _row_map),
            pl.BlockSpec((kf, TN), _col_map(n // TN)),
            pl.BlockSpec((TM_DOWN, TN), _tile_map(n // TN)),
            pl.BlockSpec((1, TN), _col_map(n // TN)),
        ],
        out_specs=out_specs,
        out_shape=out_shape,
        compiler_params=_params("parallel", "arbitrary"),
        name="ffn_down",
    )(h, wdb, x, g_next)


def _ple_kernel(xg_ref, inv_ref, wg_ref, p_ref, wp_ref, x_ref, g_ref, o_ref, xgn_ref, invn_ref):
    wg = wg_ref[...].astype(BF16)
    wp = wp_ref[...].astype(BF16)

    def x_new(rows):
        gate = jax.nn.sigmoid(_scale_rows(_dot(xg_ref[rows, :], wg), inv_ref.at[rows, :]))
        return x_ref[rows, :] + gate * _dot(p_ref[rows, :], wp)

    _store_residual(x_new, g_ref, o_ref, xgn_ref, invn_ref)


def _ple(xg, inv, wg_all, p, wp_all, l, x, g_next):
    m, k = xg.shape
    out_specs, out_shape = _residual_out(m, k, TM)
    return pl.pallas_call(
        _ple_kernel,
        grid=(m // TM, k // TN),
        in_specs=[
            pl.BlockSpec((TM, k), _row_map),
            pl.BlockSpec((TM, LANES), _row_map),
            pl.BlockSpec((None, k, TN), _layer_col_map(l, k // TN)),
            pl.BlockSpec((None, TM, PLE_DIM), lambda i, j: (l, i, 0)),
            pl.BlockSpec((None, PLE_DIM, TN), _layer_col_map(l, k // TN)),
            pl.BlockSpec((TM, TN), _tile_map(k // TN)),
            pl.BlockSpec((1, TN), _col_map(k // TN)),
        ],
        out_specs=out_specs,
        out_shape=out_shape,
        compiler_params=_params("parallel", "arbitrary"),
        name="ple",
    )(xg, inv, wg_all, p, wp_all, x, g_next)


def _time_major(a):
    return jnp.swapaxes(a, 0, 1)


def kernel(x_prompt, x_sample, state_h, state_conv, state_pool, p_prompt, p_sample, g_mix, w_in, conv_w, conv_b, w_rg, b_rg, w_ig, b_ig, lam, w_pool, pool_scale, w_out, g_ffn, w_gate, w_up, w_down, g_pe, w_pe_gate, w_pe_proj, g_final):
    row = lambda v: v.reshape(1, -1)
    layer_rows = lambda v: v.reshape(DEPTH, 1, -1)
    x, xg, inv = _embed(x_prompt.reshape(N_PROMPT, D_MODEL),
                       _time_major(x_sample).reshape(N_SAMPLE, D_MODEL), row(g_mix[0]))
    mixer_params = (conv_w, layer_rows(conv_b),
                    w_rg.astype(BF16), layer_rows(b_rg), w_ig.astype(BF16), layer_rows(b_ig),
                    layer_rows(lam), w_pool.astype(BF16), layer_rows(pool_scale))
    p_all = jnp.concatenate(
        [p_prompt.reshape(DEPTH, N_PROMPT, PLE_DIM),
         jnp.swapaxes(p_sample, 1, 2).reshape(DEPTH, N_SAMPLE, PLE_DIM)], axis=1).astype(BF16)
    conv_tm, pool_tm = jnp.swapaxes(state_conv, 1, 2), jnp.swapaxes(state_pool, 1, 2)
    hp_l, cp_l, pp_l = [], [], []
    sample_states = None
    for l in range(DEPTH):
        g_after_ple = row(g_mix[l + 1]) if l + 1 < DEPTH else row(g_final)

        z = _in_proj(xg, inv, w_in, l)

        y, hp, ct, pt = _prompt_mixer(z, l, *mixer_params)
        y, *sample_states = _sample_mixer(z, y, state_h, conv_tm, pool_tm, sample_states, l,
                                          *mixer_params)

        x, xg, inv = _out_proj(y, w_out, l, x, row(g_ffn[l]))
        h, wdb = _ffn_up(xg, inv, w_gate, w_up, w_down, l)
        x, xg, inv = _ffn_down(h, wdb, x, row(g_pe[l]))
        x, xg, inv = _ple(xg, inv, w_pe_gate, p_all, w_pe_proj, l, x, g_after_ple)

        hp_l.append(hp.reshape(BATCH, W_A))
        cp_l.append(ct[:, CONV_CARRY - (CONV_W - 1):])
        pp_l.append(pt[:, POOL_CARRY - POOL_PAD:])

    hs_all, cs_tm, ps_tm = sample_states
    gf = row(g_final)
    y_prompt = _final_norm(x, inv, gf, N_PROMPT, 0).reshape(BATCH, SEQ, D_MODEL)
    y_s = _final_norm(x, inv, gf, N_SAMPLE, N_PROMPT)
    y_sample = _time_major(y_s.reshape(DEC_SEQ, DEC_BATCH, D_MODEL))
    return (y_prompt, y_sample,
            jnp.stack(hp_l), jnp.stack(cp_l), jnp.stack(pp_l),
            hs_all, jnp.swapaxes(cs_tm, 1, 2), jnp.swapaxes(ps_tm, 1, 2))
```

```python
import functools

import jax
import jax.numpy as jnp
from jax import lax
from jax.experimental import pallas as pl
from jax.experimental.pallas import tpu as pltpu

D_MODEL = 4096
BATCH = 4
SEQ = 2048
DEPTH = 2
DEC_BATCH = 128
DEC_SEQ = 4
PAST_LEN = 16384
W_A = D_MODEL // 2
W_B = D_MODEL - W_A
LRU_HEAD = 256
N_LRU_HEADS = W_A // LRU_HEAD
LRU_C = 8.0
CONV_W = 4
POOL_WINDOWS = (2, 4, 8, 16)
N_POOL_GROUPS = len(POOL_WINDOWS)
POOL_GC = W_B // N_POOL_GROUPS
POOL_PAD = max(POOL_WINDOWS) - 1
D_FF = 11008
PLE_DIM = 256
EPS = 1e-6

N_PROMPT = BATCH * SEQ
N_SAMPLE = DEC_BATCH * DEC_SEQ
N_TOK = N_PROMPT + N_SAMPLE
Z_W = 2 * W_A + W_B

F32 = jnp.float32
BF16 = jnp.bfloat16

VMEM_LIMIT_BYTES = 60 * 1024 * 1024
VMEM_LIMIT_UP_BYTES = 63 * 1024 * 1024

TM = 1088
TM_UP = 2176
UP_CHUNK = 272
ROW_CHUNK = 272
TM_DOWN = 544
TM_EMBED = 256
TM_FINAL = 512
LANES = 128
TN = 512
TF = 256
TB = 512
CONV_CARRY = 8
POOL_CARRY = 16
SAMPLE_CB = 512


def _params(*sem):
    return pltpu.CompilerParams(dimension_semantics=sem, vmem_limit_bytes=VMEM_LIMIT_BYTES)


def _dot(a, b):
    return jnp.dot(a, b, preferred_element_type=F32)


def _snake(i, j, nj):
    return jnp.where(i % 2 == 0, j, nj - 1 - j)


def _row_map(i, j):
    return (i, 0)


def _tile_map(nj):
    return lambda i, j: (i, _snake(i, j, nj))


def _col_map(nj):
    return lambda i, j: (0, _snake(i, j, nj))


def _layer_col_map(l, nj):
    return lambda i, j: (l, 0, _snake(i, j, nj))


def _log_sigmoid(x):
    return jnp.minimum(x, 0.0) - jnp.log1p(jnp.exp(-jnp.abs(x)))


def _lru_coeffs(xc, r, i, logsig):
    log_a = r * (LRU_C * logsig)
    a = jnp.exp(log_a)
    u = 1.0 - a * a
    mult = jnp.where(u > 0.0, u * lax.rsqrt(u), 0.0)
    return a, mult, i * xc


def _lane_partial_sumsq(x):
    sq = x * x
    part = sq[:, 0:LANES]
    for c in range(1, x.shape[1] // LANES):
        part = part + sq[:, c * LANES:(c + 1) * LANES]
    return part


def _inv_rms_lanes(partial_sumsq):
    tot = jnp.sum(partial_sumsq, axis=-1, keepdims=True)
    return jnp.broadcast_to(lax.rsqrt(tot * (1.0 / D_MODEL) + EPS), partial_sumsq.shape)


def _scale_rows(acc, inv_ref):
    inv = inv_ref[...]
    return jnp.concatenate(
        [acc[:, c * LANES:(c + 1) * LANES] * inv for c in range(acc.shape[1] // LANES)], axis=1)


def _store_residual(x_new, g_ref, o_ref, xg_ref, inv_ref):
    j, last = pl.program_id(1), pl.num_programs(1) - 1
    parts = []
    for r0 in range(0, o_ref.shape[0], ROW_CHUNK):
        rows = slice(r0, r0 + ROW_CHUNK)
        xc = x_new(rows)
        o_ref[rows, :] = xc
        xg_ref[rows, :] = (xc * g_ref[...]).astype(BF16)
        parts.append(_lane_partial_sumsq(xc))
    part = jnp.concatenate(parts, axis=0)

    @pl.when(j == 0)
    def _():
        inv_ref[...] = part

    @pl.when(jnp.logical_and(j > 0, j < last))
    def _():
        inv_ref[...] += part

    @pl.when(j == last)
    def _():
        inv_ref[...] = _inv_rms_lanes(inv_ref[...] + part)


def _residual_out(m, n, tm):
    specs = [
        pl.BlockSpec((tm, TN), _tile_map(n // TN)),
        pl.BlockSpec((tm, TN), _tile_map(n // TN)),
        pl.BlockSpec((tm, LANES), _row_map),
    ]
    shapes = [
        jax.ShapeDtypeStruct((m, n), F32),
        jax.ShapeDtypeStruct((m, n), BF16),
        jax.ShapeDtypeStruct((m, LANES), F32),
    ]
    return specs, shapes


def _embed_kernel(xp_ref, xs_ref, g_ref, x_ref, xg_ref, inv_ref):
    def emit(x):
        x_ref[...] = x
        xg_ref[...] = (x * g_ref[...]).astype(BF16)
        inv_ref[...] = _inv_rms_lanes(_lane_partial_sumsq(x))

    is_prompt = pl.program_id(0) < N_PROMPT // TM_EMBED

    @pl.when(is_prompt)
    def _():
        emit(xp_ref[...])

    @pl.when(jnp.logical_not(is_prompt))
    def _():
        emit(xs_ref[...])


def _embed(xp, xs, g):
    k = xp.shape[1]
    npb = N_PROMPT // TM_EMBED
    rows = lambda i: (i, 0)
    return pl.pallas_call(
        _embed_kernel,
        grid=(N_TOK // TM_EMBED,),
        in_specs=[
            pl.BlockSpec((TM_EMBED, k), lambda i: (jnp.minimum(i, npb - 1), 0)),
            pl.BlockSpec((TM_EMBED, k), lambda i: (jnp.maximum(i - npb, 0), 0)),
            pl.BlockSpec((1, k), lambda i: (0, 0)),
        ],
        out_specs=[
            pl.BlockSpec((TM_EMBED, k), rows),
            pl.BlockSpec((TM_EMBED, k), rows),
            pl.BlockSpec((TM_EMBED, LANES), rows),
        ],
        out_shape=[
            jax.ShapeDtypeStruct((N_TOK, k), F32),
            jax.ShapeDtypeStruct((N_TOK, k), BF16),
            jax.ShapeDtypeStruct((N_TOK, LANES), F32),
        ],
        compiler_params=_params("arbitrary"),
        name="embed",
    )(xp, xs, g)


def _final_norm_kernel(x_ref, inv_ref, g_ref, o_ref):
    o_ref[...] = _scale_rows(x_ref[...], inv_ref) * g_ref[...]


def _final_norm(x, inv, g, rows, row0):
    k = x.shape[1]
    tm = TM_FINAL
    assert rows % tm == 0 and row0 % tm == 0
    rb0 = row0 // tm
    return pl.pallas_call(
        _final_norm_kernel,
        grid=(rows // tm,),
        in_specs=[pl.BlockSpec((tm, k), lambda i: (rb0 + i, 0)),
                  pl.BlockSpec((tm, LANES), lambda i: (rb0 + i, 0)),
                  pl.BlockSpec((1, k), lambda i: (0, 0))],
        out_specs=pl.BlockSpec((tm, k), lambda i: (i, 0)),
        out_shape=jax.ShapeDtypeStruct((rows, k), F32),
        compiler_params=_params("parallel"),
        name="final_norm",
    )(x, inv, g)


def _in_proj_kernel(xg_ref, inv_ref, w_ref, o_ref):
    jb = _snake(pl.program_id(0), pl.program_id(1), pl.num_programs(1))
    is_gate = jnp.logical_and(jb >= W_A // TN, jb < 2 * W_A // TN)

    def project(act):
        w = w_ref[...].astype(BF16)
        for r0 in range(0, TM, ROW_CHUNK):
            rows = slice(r0, r0 + ROW_CHUNK)
            o_ref[rows, :] = act(_scale_rows(_dot(xg_ref[rows, :], w), inv_ref.at[rows, :]))

    @pl.when(is_gate)
    def _():
        project(jax.nn.gelu)

    @pl.when(jnp.logical_not(is_gate))
    def _():
        project(lambda z: z)


def _in_proj(xg, inv, w_all, l):
    m, k = xg.shape
    n = w_all.shape[2]
    return pl.pallas_call(
        _in_proj_kernel,
        grid=(m // TM, n // TN),
        in_specs=[
            pl.BlockSpec((TM, k), _row_map),
            pl.BlockSpec((TM, LANES), _row_map),
            pl.BlockSpec((None, k, TN), _layer_col_map(l, n // TN)),
        ],
        out_specs=pl.BlockSpec((TM, TN), _tile_map(n // TN)),
        out_shape=jax.ShapeDtypeStruct((m, n), F32),
        compiler_params=_params("parallel", "arbitrary"),
        name="in_proj",
    )(xg, inv, w_all)


def _prompt_mixer_kernel(xa_ref, ga_ref, ub_ref, cw_ref, cb_ref, wr_ref, br_ref, wi_ref, bi_ref,
                         lam_ref, wp_ref, ps_ref,
                         y_ref, hl_ref, ct_ref, pt_ref,
                         hc_ref, cc_ref, pc_ref, a_ref, b_ref):
    t = pl.program_id(1)

    @pl.when(t == 0)
    def _():
        hc_ref[...] = jnp.zeros_like(hc_ref)
        cc_ref[...] = jnp.zeros_like(cc_ref)
        pc_ref[...] = jnp.zeros_like(pc_ref)

    pos = lax.broadcasted_iota(jnp.int32, (TB, 1), 0) + t * TB

    xa = xa_ref[...]
    ext = jnp.concatenate([cc_ref[...], xa], axis=0)
    xc = cb_ref[...] + cw_ref[CONV_W - 1:CONV_W, :] * xa
    for s in range(1, CONV_W):
        xc = xc + cw_ref[CONV_W - 1 - s:CONV_W - s, :] * pltpu.roll(ext, s, axis=0)[CONV_CARRY:]
    xcb = xc.astype(BF16)
    r_parts, i_parts = [], []
    for hd in range(N_LRU_HEADS):
        sl = slice(hd * LRU_HEAD, (hd + 1) * LRU_HEAD)
        r_parts.append(_dot(xcb[:, sl], wr_ref[hd]))
        i_parts.append(_dot(xcb[:, sl], wi_ref[hd]))
    r = jax.nn.sigmoid(jnp.concatenate(r_parts, axis=1) + br_ref[...])
    i = jax.nn.sigmoid(jnp.concatenate(i_parts, axis=1) + bi_ref[...])
    a, mult, ix = _lru_coeffs(xc, r, i, _log_sigmoid(lam_ref[...]))
    a_ref[...] = a
    b_ref[...] = mult * ix

    @pl.when(t == 0)
    def _():
        a_ref[0:1, :] = jnp.zeros((1, W_A), F32)
        b_ref[0:1, :] = ix[0:1, :]

    def step(row, h):
        h = a_ref[pl.ds(row, 1), :] * h + b_ref[pl.ds(row, 1), :]
        b_ref[pl.ds(row, 1), :] = h
        return h

    h_last = lax.fori_loop(0, TB, step, hc_ref[0:1, :], unroll=8)
    hc_ref[0:1, :] = h_last
    hl_ref[0] = h_last
    y_ref[:, 0:W_A] = (ga_ref[...] * b_ref[...]).astype(BF16)
    cc_ref[...] = xa[TB - CONV_CARRY:]
    ct_ref[0] = xa[TB - CONV_CARRY:]

    ub = ub_ref[...]
    pext = jnp.concatenate([pc_ref[...], ub], axis=0)
    for g, w in enumerate(POOL_WINDOWS):
        sl = slice(g * POOL_GC, (g + 1) * POOL_GC)
        s = pext[:, sl]
        shift = 1
        while shift < w:
            s = s + pltpu.roll(s, shift, axis=0)
            shift *= 2
        cnt = jnp.minimum(pos + 1, w).astype(F32)
        d = s[POOL_CARRY:] / cnt - ub[:, sl]
        yb = _dot(d.astype(BF16), wp_ref[g]) * ps_ref[:, sl]
        y_ref[:, W_A + g * POOL_GC:W_A + (g + 1) * POOL_GC] = yb.astype(BF16)
    pc_ref[...] = ub[TB - POOL_CARRY:]
    pt_ref[0] = ub[TB - POOL_CARRY:]


def _prompt_mixer(z, l, cw, cb, wr, br, wi, bi, lam, wp, ps):
    nt = SEQ // TB
    row = lambda b, t: b * nt + t
    full2 = lambda shape: pl.BlockSpec((None,) + shape, lambda b, t: (l, 0, 0))
    full3 = lambda shape: pl.BlockSpec((None,) + shape, lambda b, t: (l, 0, 0, 0))
    return pl.pallas_call(
        _prompt_mixer_kernel,
        grid=(BATCH, nt),
        in_specs=[
            pl.BlockSpec((TB, W_A), lambda b, t: (row(b, t), 0)),
            pl.BlockSpec((TB, W_A), lambda b, t: (row(b, t), 1)),
            pl.BlockSpec((TB, W_B), lambda b, t: (row(b, t), 2)),
            full2((CONV_W, W_A)), full2((1, W_A)),
            full3((N_LRU_HEADS, LRU_HEAD, LRU_HEAD)), full2((1, W_A)),
            full3((N_LRU_HEADS, LRU_HEAD, LRU_HEAD)), full2((1, W_A)),
            full2((1, W_A)),
            full3((N_POOL_GROUPS, POOL_GC, POOL_GC)), full2((1, W_B)),
        ],
        out_specs=[
            pl.BlockSpec((TB, D_MODEL), lambda b, t: (row(b, t), 0)),
            pl.BlockSpec((1, 1, W_A), lambda b, t: (b, 0, 0)),
            pl.BlockSpec((1, CONV_CARRY, W_A), lambda b, t: (b, 0, 0)),
            pl.BlockSpec((1, POOL_CARRY, W_B), lambda b, t: (b, 0, 0)),
        ],
        out_shape=[
            jax.ShapeDtypeStruct((N_TOK, D_MODEL), BF16),
            jax.ShapeDtypeStruct((BATCH, 1, W_A), F32),
            jax.ShapeDtypeStruct((BATCH, CONV_CARRY, W_A), F32),
            jax.ShapeDtypeStruct((BATCH, POOL_CARRY, W_B), F32),
        ],
        scratch_shapes=[
            pltpu.VMEM((8, W_A), F32),
            pltpu.VMEM((CONV_CARRY, W_A), F32),
            pltpu.VMEM((POOL_CARRY, W_B), F32),
            pltpu.VMEM((TB, W_A), F32),
            pltpu.VMEM((TB, W_A), F32),
        ],
        compiler_params=_params("parallel", "arbitrary"),
        name="prompt_mixer",
    )(z, z, z, cw, cb, wr, br, wi, bi, lam, wp, ps)


def _sample_mixer_kernel(xa_ref, ga_ref, ub_ref, h0_ref, cs_ref, pst_ref,
                         cw_ref, cb_ref, wr_ref, br_ref, wi_ref, bi_ref, lam_ref, wp_ref, ps_ref,
                         *aliased_and_out_refs):
    y_ref, hn_ref, cn_ref, pn_ref = aliased_and_out_refs[-4:]
    c = pl.program_id(0)
    heads = SAMPLE_CB // LRU_HEAD
    rows = lambda t: slice(t * DEC_BATCH, (t + 1) * DEC_BATCH)

    ext = [cs_ref[k] for k in range(CONV_W - 1)] + [xa_ref[rows(t), :] for t in range(DEC_SEQ)]
    logsig = _log_sigmoid(lam_ref[...])
    h = h0_ref[...]
    ya = []
    for t in range(DEC_SEQ):
        xc = cb_ref[...] + cw_ref[0:1, :] * ext[t]
        for k in range(1, CONV_W):
            xc = xc + cw_ref[k:k + 1, :] * ext[t + k]
        xcb = xc.astype(BF16)
        r_parts, i_parts = [], []
        for hd in range(heads):
            sl = slice(hd * LRU_HEAD, (hd + 1) * LRU_HEAD)
            r_parts.append(_dot(xcb[:, sl], wr_ref[hd]))
            i_parts.append(_dot(xcb[:, sl], wi_ref[hd]))
        r = jax.nn.sigmoid(jnp.concatenate(r_parts, axis=1) + br_ref[...])
        i = jax.nn.sigmoid(jnp.concatenate(i_parts, axis=1) + bi_ref[...])
        a, mult, ix = _lru_coeffs(xc, r, i, logsig)
        if PAST_LEN + t == 0:
            a, mult = jnp.zeros_like(a), jnp.ones_like(mult)
        h = a * h + mult * ix
        ya.append((ga_ref[rows(t), :] * h).astype(BF16))
    hn_ref[...] = h
    for k in range(CONV_W - 1):
        cn_ref[k] = ext[DEC_SEQ + k]

    for k in range(POOL_PAD):
        src = k + DEC_SEQ
        pn_ref[k] = pst_ref[src] if src < POOL_PAD else ub_ref[rows(src - POOL_PAD), :]

    for g, w in enumerate(POOL_WINDOWS):
        @pl.when(c == g)
        def _(g=g, w=w):
            pext = ([pst_ref[k] for k in range(POOL_PAD - w + 1, POOL_PAD)]
                    + [ub_ref[rows(t), :] for t in range(DEC_SEQ)])
            for t in range(DEC_SEQ):
                s = pext[t]
                for j in range(1, w):
                    s = s + pext[t + j]
                cnt = float(min(PAST_LEN + t + 1, w))
                d = s / cnt - pext[t + w - 1]
                yb = _dot(d.astype(BF16), wp_ref[0]) * ps_ref[...]
                y_ref[rows(t), g * SAMPLE_CB:(g + 1) * SAMPLE_CB] = ya[t]
                y_ref[rows(t), W_A + g * POOL_GC:W_A + (g + 1) * POOL_GC] = yb.astype(BF16)


def _sample_mixer(z, y, state_h, state_conv, state_pool, prev_states, l,
                  cw, cb, wr, br, wi, bi, lam, wp, ps):
    cbw = SAMPLE_CB
    na = W_A // cbw
    heads = cbw // LRU_HEAD
    rb = N_PROMPT // N_SAMPLE
    vec = pl.BlockSpec((None, 1, cbw), lambda c: (l, 0, c))
    state3 = lambda steps: pl.BlockSpec((None, steps, DEC_BATCH, cbw), lambda c: (l, 0, 0, c))
    state2 = pl.BlockSpec((None, DEC_BATCH, cbw), lambda c: (l, 0, c))
    aliased = (y,) + (() if prev_states is None else tuple(prev_states))
    n_in = 15
    return pl.pallas_call(
        _sample_mixer_kernel,
        grid=(na,),
        in_specs=[
            pl.BlockSpec((N_SAMPLE, cbw), lambda c: (rb, c)),
            pl.BlockSpec((N_SAMPLE, cbw), lambda c: (rb, na + c)),
            pl.BlockSpec((N_SAMPLE, cbw), lambda c: (rb, 2 * na + c)),
            state2,
            state3(CONV_W - 1),
            state3(POOL_PAD),
            pl.BlockSpec((None, CONV_W, cbw), lambda c: (l, 0, c)), vec,
            pl.BlockSpec((None, heads, LRU_HEAD, LRU_HEAD), lambda c: (l, c, 0, 0)), vec,
            pl.BlockSpec((None, heads, LRU_HEAD, LRU_HEAD), lambda c: (l, c, 0, 0)), vec,
            vec,
            pl.BlockSpec((None, 1, POOL_GC, POOL_GC), lambda c: (l, c, 0, 0)), vec,
        ] + [pl.BlockSpec(memory_space=pl.ANY)] * len(aliased),
        out_specs=[
            pl.BlockSpec((N_SAMPLE, D_MODEL), lambda c: (rb, 0)),
            state2,
            state3(CONV_W - 1),
            state3(POOL_PAD),
        ],
        out_shape=[
            jax.ShapeDtypeStruct((N_TOK, D_MODEL), BF16),
            jax.ShapeDtypeStruct((DEPTH, DEC_BATCH, W_A), F32),
            jax.ShapeDtypeStruct((DEPTH, CONV_W - 1, DEC_BATCH, W_A), F32),
            jax.ShapeDtypeStruct((DEPTH, POOL_PAD, DEC_BATCH, W_B), F32),
        ],
        input_output_aliases={n_in + a: a for a in range(len(aliased))},
        compiler_params=_params("arbitrary"),
        name="sample_mixer",
    )(z, z, z, state_h, state_conv, state_pool, cw, cb, wr, br, wi, bi, lam, wp, ps, *aliased)


def _out_proj_kernel(y_ref, w_ref, x_ref, g_ref, o_ref, xg_ref, inv_ref):
    w = w_ref[...].astype(BF16)
    x_new = lambda rows: x_ref[rows, :] + _dot(y_ref[rows, :], w)
    _store_residual(x_new, g_ref, o_ref, xg_ref, inv_ref)


def _out_proj(y, w_all, l, x, g_next):
    m, k = y.shape
    n = w_all.shape[2]
    out_specs, out_shape = _residual_out(m, n, TM)
    return pl.pallas_call(
        _out_proj_kernel,
        grid=(m // TM, n // TN),
        in_specs=[
            pl.BlockSpec((TM, k), _row_map),
            pl.BlockSpec((None, k, TN), _layer_col_map(l, n // TN)),
            pl.BlockSpec((TM, TN), _tile_map(n // TN)),
            pl.BlockSpec((1, TN), _col_map(n // TN)),
        ],
        out_specs=out_specs,
        out_shape=out_shape,
        compiler_params=_params("parallel", "arbitrary"),
        name="out_proj",
    )(y, w_all, x, g_next)


def _ffn_up_kernel(xg_ref, inv_ref, wg_ref, wu_ref, wd_ref, h_ref, wdb_ref):
    wg = wg_ref[...].astype(BF16)
    wu = wu_ref[...].astype(BF16)
    for r0 in range(0, TM_UP, UP_CHUNK):
        rows = slice(r0, r0 + UP_CHUNK)
        xg = xg_ref[rows, :]
        inv = inv_ref.at[rows, :]
        gate = _scale_rows(_dot(xg, wg), inv)
        up = _scale_rows(_dot(xg, wu), inv)
        h_ref[rows, :] = (jax.nn.silu(gate) * up).astype(BF16)
    wdb_ref[...] = wd_ref[...].astype(BF16)


def _ffn_up(xg, inv, wg_all, wu_all, wd_all, l):
    m, k = xg.shape
    nf = D_FF // TF
    steps = (m // TM_UP) * nf
    wd_rows = D_FF // steps
    assert wd_rows * steps == D_FF and wd_rows % 16 == 0
    return pl.pallas_call(
        _ffn_up_kernel,
        grid=(m // TM_UP, nf),
        in_specs=[
            pl.BlockSpec((TM_UP, k), _row_map),
            pl.BlockSpec((TM_UP, LANES), _row_map, pipeline_mode=pl.Buffered(1)),
            pl.BlockSpec((None, k, TF), _layer_col_map(l, nf)),
            pl.BlockSpec((None, k, TF), _layer_col_map(l, nf)),
            pl.BlockSpec((None, wd_rows, k), lambda i, j: (l, i * nf + j, 0)),
        ],
        out_specs=[
            pl.BlockSpec((TM_UP, TF), _tile_map(nf)),
            pl.BlockSpec((wd_rows, k), lambda i, j: (i * nf + j, 0)),
        ],
        out_shape=[
            jax.ShapeDtypeStruct((m, D_FF), BF16),
            jax.ShapeDtypeStruct((D_FF, k), BF16),
        ],
        compiler_params=pltpu.CompilerParams(dimension_semantics=("arbitrary", "arbitrary"),
                                             vmem_limit_bytes=VMEM_LIMIT_UP_BYTES),
        name="ffn_up",
    )(xg, inv, wg_all, wu_all, wd_all)


def _ffn_down_kernel(h_ref, w_ref, x_ref, g_ref, o_ref, xg_ref, inv_ref):
    x_new = lambda rows: x_ref[rows, :] + _dot(h_ref[rows, :], w_ref[...])
    _store_residual(x_new, g_ref, o_ref, xg_ref, inv_ref)


def _ffn_down(h, wdb, x, g_next):
    m, kf = h.shape
    n = wdb.shape[1]
    out_specs, out_shape = _residual_out(m, n, TM_DOWN)
    return pl.pallas_call(
        _ffn_down_kernel,
        grid=(m // TM_DOWN, n // TN),
        in_specs=[
            pl.BlockSpec((TM_DOWN, kf), _row_map),
            pl.BlockSpec((kf, TN), _col_map(n // TN)),
            pl.BlockSpec((TM_DOWN, TN), _tile_map(n // TN)),
            pl.BlockSpec((1, TN), _col_map(n // TN)),
        ],
        out_specs=out_specs,
        out_shape=out_shape,
        compiler_params=_params("parallel", "arbitrary"),
        name="ffn_down",
    )(h, wdb, x, g_next)


def _ple_kernel(xg_ref, inv_ref, wg_ref, p_ref, wp_ref, x_ref, g_ref, o_ref, xgn_ref, invn_ref):
    wg = wg_ref[...].astype(BF16)
    wp = wp_ref[...].astype(BF16)

    def x_new(rows):
        gate = jax.nn.sigmoid(_scale_rows(_dot(xg_ref[rows, :], wg), inv_ref.at[rows, :]))
        return x_ref[rows, :] + gate * _dot(p_ref[rows, :], wp)

    _store_residual(x_new, g_ref, o_ref, xgn_ref, invn_ref)


def _ple(xg, inv, wg_all, p, wp_all, l, x, g_next):
    m, k = xg.shape
    out_specs, out_shape = _residual_out(m, k, TM)
    return pl.pallas_call(
        _ple_kernel,
        grid=(m // TM, k // TN),
        in_specs=[
            pl.BlockSpec((TM, k), _row_map),
            pl.BlockSpec((TM, LANES), _row_map),
            pl.BlockSpec((None, k, TN), _layer_col_map(l, k // TN)),
            pl.BlockSpec((None, TM, PLE_DIM), lambda i, j: (l, i, 0)),
            pl.BlockSpec((None, PLE_DIM, TN), _layer_col_map(l, k // TN)),
            pl.BlockSpec((TM, TN), _tile_map(k // TN)),
            pl.BlockSpec((1, TN), _col_map(k // TN)),
        ],
        out_specs=out_specs,
        out_shape=out_shape,
        compiler_params=_params("parallel", "arbitrary"),
        name="ple",
    )(xg, inv, wg_all, p, wp_all, x, g_next)


def _time_major(a):
    return jnp.swapaxes(a, 0, 1)


def kernel(x_prompt, x_sample, state_h, state_conv, state_pool, p_prompt, p_sample, g_mix, w_in, conv_w, conv_b, w_rg, b_rg, w_ig, b_ig, lam, w_pool, pool_scale, w_out, g_ffn, w_gate, w_up, w_down, g_pe, w_pe_gate, w_pe_proj, g_final):
    row = lambda v: v.reshape(1, -1)
    layer_rows = lambda v: v.reshape(DEPTH, 1, -1)
    x, xg, inv = _embed(x_prompt.reshape(N_PROMPT, D_MODEL),
                       _time_major(x_sample).reshape(N_SAMPLE, D_MODEL), row(g_mix[0]))
    mixer_params = (conv_w, layer_rows(conv_b),
                    w_rg.astype(BF16), layer_rows(b_rg), w_ig.astype(BF16), layer_rows(b_ig),
                    layer_rows(lam), w_pool.astype(BF16), layer_rows(pool_scale))
    p_all = jnp.concatenate(
        [p_prompt.reshape(DEPTH, N_PROMPT, PLE_DIM),
         jnp.swapaxes(p_sample, 1, 2).reshape(DEPTH, N_SAMPLE, PLE_DIM)], axis=1).astype(BF16)
    conv_tm, pool_tm = jnp.swapaxes(state_conv, 1, 2), jnp.swapaxes(state_pool, 1, 2)
    hp_l, cp_l, pp_l = [], [], []
    sample_states = None
    for l in range(DEPTH):
        g_after_ple = row(g_mix[l + 1]) if l + 1 < DEPTH else row(g_final)

        z = _in_proj(xg, inv, w_in, l)

        y, hp, ct, pt = _prompt_mixer(z, l, *mixer_params)
        y, *sample_states = _sample_mixer(z, y, state_h, conv_tm, pool_tm, sample_states, l,
                                          *mixer_params)

        x, xg, inv = _out_proj(y, w_out, l, x, row(g_ffn[l]))
        h, wdb = _ffn_up(xg, inv, w_gate, w_up, w_down, l)
        x, xg, inv = _ffn_down(h, wdb, x, row(g_pe[l]))
        x, xg, inv = _ple(xg, inv, w_pe_gate, p_all, w_pe_proj, l, x, g_after_ple)

        hp_l.append(hp.reshape(BATCH, W_A))
        cp_l.append(ct[:, CONV_CARRY - (CONV_W - 1):])
        pp_l.append(pt[:, POOL_CARRY - POOL_PAD:])

    hs_all, cs_tm, ps_tm = sample_states
    gf = row(g_final)
    y_prompt = _final_norm(x, inv, gf, N_PROMPT, 0).reshape(BATCH, SEQ, D_MODEL)
    y_s = _final_norm(x, inv, gf, N_SAMPLE, N_PROMPT)
    y_sample = _time_major(y_s.reshape(DEC_SEQ, DEC_BATCH, D_MODEL))
    return (y_prompt, y_sample,
            jnp.stack(hp_l), jnp.stack(cp_l), jnp.stack(pp_l),
            hs_all, jnp.swapaxes(cs_tm, 1, 2), jnp.swapaxes(ps_tm, 1, 2))
```

```python
import jax
import jax.numpy as jnp
from jax import lax
from jax.experimental import pallas as pl
from jax.experimental.pallas import tpu as pltpu

D_MODEL = 4096
BATCH = 4
SEQ = 2048
DEPTH = 2
DEC_BATCH = 128
DEC_SEQ = 4
PAST_LEN = 16384
W_A = D_MODEL // 2
W_B = D_MODEL - W_A
LRU_HEAD = 256
N_LRU_HEADS = W_A // LRU_HEAD
LRU_C = 8.0
CONV_W = 4
POOL_WINDOWS = (2, 4, 8, 16)
N_POOL_GROUPS = len(POOL_WINDOWS)
POOL_GC = W_B // N_POOL_GROUPS
POOL_PAD = max(POOL_WINDOWS) - 1
D_FF = 11008
PLE_DIM = 256
EPS = 1e-6

N_PROMPT = BATCH * SEQ
N_SAMPLE = DEC_BATCH * DEC_SEQ
N_TOK = N_PROMPT + N_SAMPLE
Z_W = 2 * W_A + W_B

F32 = jnp.float32
BF16 = jnp.bfloat16

VMEM_LIMIT_BYTES = 60 * 1024 * 1024
VMEM_LIMIT_UP_BYTES = 63 * 1024 * 1024

TM = 1088
TM_UP = 2176
UP_CHUNK = 272
ROW_CHUNK = 272
TM_DOWN = 544
TM_EMBED = 256
TM_FINAL = 512
LANES = 128
TN = 512
TF = 256
TB = 512
CONV_CARRY = 8
POOL_CARRY = 16
SAMPLE_CB = 512


def _params(*sem):
    return pltpu.CompilerParams(dimension_semantics=sem, vmem_limit_bytes=VMEM_LIMIT_BYTES)


def _dot(a, b):
    return jnp.dot(a, b, preferred_element_type=F32)


def _snake(i, j, nj):
    return jnp.where(i % 2 == 0, j, nj - 1 - j)


def _row_map(i, j):
    return (i, 0)


def _tile_map(nj):
    return lambda i, j: (i, _snake(i, j, nj))


def _col_map(nj):
    return lambda i, j: (0, _snake(i, j, nj))


def _layer_col_map(l, nj):
    return lambda i, j: (l, 0, _snake(i, j, nj))


def _log_sigmoid(x):
    return jnp.minimum(x, 0.0) - jnp.log1p(jnp.exp(-jnp.abs(x)))


def _lru_coeffs(xc, r, i, logsig):
    log_a = r * (LRU_C * logsig)
    a = jnp.exp(log_a)
    u = 1.0 - a * a
    mult = jnp.where(u > 0.0, u * lax.rsqrt(u), 0.0)
    return a, mult, i * xc


def _lane_partial_sumsq(x):
    sq = x * x
    part = sq[:, 0:LANES]
    for c in range(1, x.shape[1] // LANES):
        part = part + sq[:, c * LANES:(c + 1) * LANES]
    return part


def _inv_rms_lanes(partial_sumsq):
    tot = jnp.sum(partial_sumsq, axis=-1, keepdims=True)
    return jnp.broadcast_to(lax.rsqrt(tot * (1.0 / D_MODEL) + EPS), partial_sumsq.shape)


def _scale_rows(acc, inv_ref):
    inv = inv_ref[...]
    return jnp.concatenate(
        [acc[:, c * LANES:(c + 1) * LANES] * inv for c in range(acc.shape[1] // LANES)], axis=1)


def _store_residual(x_new, g_ref, o_ref, xg_ref, inv_ref):
    j, last = pl.program_id(1), pl.num_programs(1) - 1
    parts = []
    for r0 in range(0, o_ref.shape[0], ROW_CHUNK):
        rows = slice(r0, r0 + ROW_CHUNK)
        xc = x_new(rows)
        o_ref[rows, :] = xc
        xg_ref[rows, :] = (xc * g_ref[...]).astype(BF16)
        parts.append(_lane_partial_sumsq(xc))
    part = jnp.concatenate(parts, axis=0)

    @pl.when(j == 0)
    def _():
        inv_ref[...] = part

    @pl.when(jnp.logical_and(j > 0, j < last))
    def _():
        inv_ref[...] += part

    @pl.when(j == last)
    def _():
        inv_ref[...] = _inv_rms_lanes(inv_ref[...] + part)


def _residual_out(m, n, tm):
    specs = [
        pl.BlockSpec((tm, TN), _tile_map(n // TN)),
        pl.BlockSpec((tm, TN), _tile_map(n // TN)),
        pl.BlockSpec((tm, LANES), _row_map),
    ]
    shapes = [
        jax.ShapeDtypeStruct((m, n), F32),
        jax.ShapeDtypeStruct((m, n), BF16),
        jax.ShapeDtypeStruct((m, LANES), F32),
    ]
    return specs, shapes


def _embed_kernel(xp_ref, xs_ref, g_ref, x_ref, xg_ref, inv_ref):
    def emit(x):
        x_ref[...] = x
        xg_ref[...] = (x * g_ref[...]).astype(BF16)
        inv_ref[...] = _inv_rms_lanes(_lane_partial_sumsq(x))

    is_prompt = pl.program_id(0) < N_PROMPT // TM_EMBED

    @pl.when(is_prompt)
    def _():
        emit(xp_ref[...])

    @pl.when(jnp.logical_not(is_prompt))
    def _():
        emit(xs_ref[...])


def _embed(xp, xs, g):
    k = xp.shape[1]
    npb = N_PROMPT // TM_EMBED
    rows = lambda i: (i, 0)
    return pl.pallas_call(
        _embed_kernel,
        grid=(N_TOK // TM_EMBED,),
        in_specs=[
            pl.BlockSpec((TM_EMBED, k), lambda i: (jnp.minimum(i, npb - 1), 0)),
            pl.BlockSpec((TM_EMBED, k), lambda i: (jnp.maximum(i - npb, 0), 0)),
            pl.BlockSpec((1, k), lambda i: (0, 0)),
        ],
        out_specs=[
            pl.BlockSpec((TM_EMBED, k), rows),
            pl.BlockSpec((TM_EMBED, k), rows),
            pl.BlockSpec((TM_EMBED, LANES), rows),
        ],
        out_shape=[
            jax.ShapeDtypeStruct((N_TOK, k), F32),
            jax.ShapeDtypeStruct((N_TOK, k), BF16),
            jax.ShapeDtypeStruct((N_TOK, LANES), F32),
        ],
        compiler_params=_params("arbitrary"),
        name="embed",
    )(xp, xs, g)


def _final_norm_kernel(x_ref, inv_ref, g_ref, o_ref):
    o_ref[...] = _scale_rows(x_ref[...], inv_ref) * g_ref[...]


def _final_norm(x, inv, g, rows, row0):
    k = x.shape[1]
    tm = TM_FINAL
    assert rows % tm == 0 and row0 % tm == 0
    rb0 = row0 // tm
    return pl.pallas_call(
        _final_norm_kernel,
        grid=(rows // tm,),
        in_specs=[pl.BlockSpec((tm, k), lambda i: (rb0 + i, 0)),
                  pl.BlockSpec((tm, LANES), lambda i: (rb0 + i, 0)),
                  pl.BlockSpec((1, k), lambda i: (0, 0))],
        out_specs=pl.BlockSpec((tm, k), lambda i: (i, 0)),
        out_shape=jax.ShapeDtypeStruct((rows, k), F32),
        compiler_params=_params("parallel"),
        name="final_norm",
    )(x, inv, g)


def _in_proj_kernel(xg_ref, inv_ref, w_ref, o_ref):
    jb = _snake(pl.program_id(0), pl.program_id(1), pl.num_programs(1))
    is_gate = jnp.logical_and(jb >= W_A // TN, jb < 2 * W_A // TN)

    def project(act):
        w = w_ref[...].astype(BF16)
        for r0 in range(0, TM, ROW_CHUNK):
            rows = slice(r0, r0 + ROW_CHUNK)
            o_ref[rows, :] = act(_scale_rows(_dot(xg_ref[rows, :], w), inv_ref.at[rows, :]))

    @pl.when(is_gate)
    def _():
        project(jax.nn.gelu)

    @pl.when(jnp.logical_not(is_gate))
    def _():
        project(lambda z: z)


def _in_proj(xg, inv, w_all, l):
    m, k = xg.shape
    n = w_all.shape[2]
    return pl.pallas_call(
        _in_proj_kernel,
        grid=(m // TM, n // TN),
        in_specs=[
            pl.BlockSpec((TM, k), _row_map),
            pl.BlockSpec((TM, LANES), _row_map),
            pl.BlockSpec((None, k, TN), _layer_col_map(l, n // TN)),
        ],
        out_specs=pl.BlockSpec((TM, TN), _tile_map(n // TN)),
        out_shape=jax.ShapeDtypeStruct((m, n), F32),
        compiler_params=_params("parallel", "arbitrary"),
        name="in_proj",
    )(xg, inv, w_all)


def _prompt_mixer_kernel(xa_ref, ga_ref, ub_ref, cw_ref, cb_ref, wr_ref, br_ref, wi_ref, bi_ref,
                         lam_ref, wp_ref, ps_ref,
                         y_ref, hl_ref, ct_ref, pt_ref,
                         hc_ref, cc_ref, pc_ref, a_ref, b_ref):
    t = pl.program_id(1)

    @pl.when(t == 0)
    def _():
        hc_ref[...] = jnp.zeros_like(hc_ref)
        cc_ref[...] = jnp.zeros_like(cc_ref)
        pc_ref[...] = jnp.zeros_like(pc_ref)

    pos = lax.broadcasted_iota(jnp.int32, (TB, 1), 0) + t * TB

    xa = xa_ref[...]
    ext = jnp.concatenate([cc_ref[...], xa], axis=0)
    xc = cb_ref[...] + cw_ref[CONV_W - 1:CONV_W, :] * xa
    for s in range(1, CONV_W):
        xc = xc + cw_ref[CONV_W - 1 - s:CONV_W - s, :] * pltpu.roll(ext, s, axis=0)[CONV_CARRY:]
    xcb = xc.astype(BF16)
    r_parts, i_parts = [], []
    for hd in range(N_LRU_HEADS):
        sl = slice(hd * LRU_HEAD, (hd + 1) * LRU_HEAD)
        r_parts.append(_dot(xcb[:, sl], wr_ref[hd]))
        i_parts.append(_dot(xcb[:, sl], wi_ref[hd]))
    r = jax.nn.sigmoid(jnp.concatenate(r_parts, axis=1) + br_ref[...])
    i = jax.nn.sigmoid(jnp.concatenate(i_parts, axis=1) + bi_ref[...])
    a, mult, ix = _lru_coeffs(xc, r, i, _log_sigmoid(lam_ref[...]))
    a_ref[...] = a
    b_ref[...] = mult * ix

    @pl.when(t == 0)
    def _():
        a_ref[0:1, :] = jnp.zeros((1, W_A), F32)
        b_ref[0:1, :] = ix[0:1, :]

    def step(row, h):
        h = a_ref[pl.ds(row, 1), :] * h + b_ref[pl.ds(row, 1), :]
        b_ref[pl.ds(row, 1), :] = h
        return h

    h_last = lax.fori_loop(0, TB, step, hc_ref[0:1, :], unroll=8)
    hc_ref[0:1, :] = h_last
    hl_ref[0] = h_last
    y_ref[:, 0:W_A] = (ga_ref[...] * b_ref[...]).astype(BF16)
    cc_ref[...] = xa[TB - CONV_CARRY:]
    ct_ref[0] = xa[TB - CONV_CARRY:]

    ub = ub_ref[...]
    pext = jnp.concatenate([pc_ref[...], ub], axis=0)
    for g, w in enumerate(POOL_WINDOWS):
        sl = slice(g * POOL_GC, (g + 1) * POOL_GC)
        s = pext[:, sl]
        shift = 1
        while shift < w:
            s = s + pltpu.roll(s, shift, axis=0)
            shift *= 2
        cnt = jnp.minimum(pos + 1, w).astype(F32)
        d = s[POOL_CARRY:] / cnt - ub[:, sl]
        yb = _dot(d.astype(BF16), wp_ref[g]) * ps_ref[:, sl]
        y_ref[:, W_A + g * POOL_GC:W_A + (g + 1) * POOL_GC] = yb.astype(BF16)
    pc_ref[...] = ub[TB - POOL_CARRY:]
    pt_ref[0] = ub[TB - POOL_CARRY:]


def _prompt_mixer(z, l, cw, cb, wr, br, wi, bi, lam, wp, ps):
    nt = SEQ // TB
    row = lambda b, t: b * nt + t
    full2 = lambda shape: pl.BlockSpec((None,) + shape, lambda b, t: (l, 0, 0))
    full3 = lambda shape: pl.BlockSpec((None,) + shape, lambda b, t: (l, 0, 0, 0))
    return pl.pallas_call(
        _prompt_mixer_kernel,
        grid=(BATCH, nt),
        in_specs=[
            pl.BlockSpec((TB, W_A), lambda b, t: (row(b, t), 0)),
            pl.BlockSpec((TB, W_A), lambda b, t: (row(b, t), 1)),
            pl.BlockSpec((TB, W_B), lambda b, t: (row(b, t), 2)),
            full2((CONV_W, W_A)), full2((1, W_A)),
            full3((N_LRU_HEADS, LRU_HEAD, LRU_HEAD)), full2((1, W_A)),
            full3((N_LRU_HEADS, LRU_HEAD, LRU_HEAD)), full2((1, W_A)),
            full2((1, W_A)),
            full3((N_POOL_GROUPS, POOL_GC, POOL_GC)), full2((1, W_B)),
        ],
        out_specs=[
            pl.BlockSpec((TB, D_MODEL), lambda b, t: (row(b, t), 0)),
            pl.BlockSpec((1, 1, W_A), lambda b, t: (b, 0, 0)),
            pl.BlockSpec((1, CONV_CARRY, W_A), lambda b, t: (b, 0, 0)),
            pl.BlockSpec((1, POOL_CARRY, W_B), lambda b, t: (b, 0, 0)),
        ],
        out_shape=[
            jax.ShapeDtypeStruct((N_TOK, D_MODEL), BF16),
            jax.ShapeDtypeStruct((BATCH, 1, W_A), F32),
            jax.ShapeDtypeStruct((BATCH, CONV_CARRY, W_A), F32),
            jax.ShapeDtypeStruct((BATCH, POOL_CARRY, W_B), F32),
        ],
        scratch_shapes=[
            pltpu.VMEM((8, W_A), F32),
            pltpu.VMEM((CONV_CARRY, W_A), F32),
            pltpu.VMEM((POOL_CARRY, W_B), F32),
            pltpu.VMEM((TB, W_A), F32),
            pltpu.VMEM((TB, W_A), F32),
        ],
        compiler_params=_params("parallel", "arbitrary"),
        name="prompt_mixer",
    )(z, z, z, cw, cb, wr, br, wi, bi, lam, wp, ps)


def _sample_mixer_kernel(xa_ref, ga_ref, ub_ref, h0_ref, cs_ref, pst_ref,
                         cw_ref, cb_ref, wr_ref, br_ref, wi_ref, bi_ref, lam_ref, wp_ref, ps_ref,
                         *aliased_and_out_refs):
    y_ref, hn_ref, cn_ref, pn_ref = aliased_and_out_refs[-4:]
    c = pl.program_id(0)
    heads = SAMPLE_CB // LRU_HEAD
    rows = lambda t: slice(t * DEC_BATCH, (t + 1) * DEC_BATCH)

    ext = [cs_ref[k] for k in range(CONV_W - 1)] + [xa_ref[rows(t), :] for t in range(DEC_SEQ)]
    logsig = _log_sigmoid(lam_ref[...])
    h = h0_ref[...]
    ya = []
    for t in range(DEC_SEQ):
        xc = cb_ref[...] + cw_ref[0:1, :] * ext[t]
        for k in range(1, CONV_W):
            xc = xc + cw_ref[k:k + 1, :] * ext[t + k]
        xcb = xc.astype(BF16)
        r_parts, i_parts = [], []
        for hd in range(heads):
            sl = slice(hd * LRU_HEAD, (hd + 1) * LRU_HEAD)
            r_parts.append(_dot(xcb[:, sl], wr_ref[hd]))
            i_parts.append(_dot(xcb[:, sl], wi_ref[hd]))
        r = jax.nn.sigmoid(jnp.concatenate(r_parts, axis=1) + br_ref[...])
        i = jax.nn.sigmoid(jnp.concatenate(i_parts, axis=1) + bi_ref[...])
        a, mult, ix = _lru_coeffs(xc, r, i, logsig)
        if PAST_LEN + t == 0:
            a, mult = jnp.zeros_like(a), jnp.ones_like(mult)
        h = a * h + mult * ix
        ya.append((ga_ref[rows(t), :] * h).astype(BF16))
    hn_ref[...] = h
    for k in range(CONV_W - 1):
        cn_ref[k] = ext[DEC_SEQ + k]

    for k in range(POOL_PAD):
        src = k + DEC_SEQ
        pn_ref[k] = pst_ref[src] if src < POOL_PAD else ub_ref[rows(src - POOL_PAD), :]

    for g, w in enumerate(POOL_WINDOWS):
        @pl.when(c == g)
        def _(g=g, w=w):
            pext = ([pst_ref[k] for k in range(POOL_PAD - w + 1, POOL_PAD)]
                    + [ub_ref[rows(t), :] for t in range(DEC_SEQ)])
            for t in range(DEC_SEQ):
                s = pext[t]
                for j in range(1, w):
                    s = s + pext[t + j]
                cnt = float(min(PAST_LEN + t + 1, w))
                d = s / cnt - pext[t + w - 1]
                yb = _dot(d.astype(BF16), wp_ref[0]) * ps_ref[...]
                y_ref[rows(t), g * SAMPLE_CB:(g + 1) * SAMPLE_CB] = ya[t]
                y_ref[rows(t), W_A + g * POOL_GC:W_A + (g + 1) * POOL_GC] = yb.astype(BF16)


def _sample_mixer(z, y, state_h, state_conv, state_pool, prev_states, l,
                  cw, cb, wr, br, wi, bi, lam, wp, ps):
    cbw = SAMPLE_CB
    na = W_A // cbw
    heads = cbw // LRU_HEAD
    rb = N_PROMPT // N_SAMPLE
    vec = pl.BlockSpec((None, 1, cbw), lambda c: (l, 0, c))
    state3 = lambda steps: pl.BlockSpec((None, steps, DEC_BATCH, cbw), lambda c: (l, 0, 0, c))
    state2 = pl.BlockSpec((None, DEC_BATCH, cbw), lambda c: (l, 0, c))
    aliased = (y,) + (() if prev_states is None else tuple(prev_states))
    n_in = 15
    return pl.pallas_call(
        _sample_mixer_kernel,
        grid=(na,),
        in_specs=[
            pl.BlockSpec((N_SAMPLE, cbw), lambda c: (rb, c)),
            pl.BlockSpec((N_SAMPLE, cbw), lambda c: (rb, na + c)),
            pl.BlockSpec((N_SAMPLE, cbw), lambda c: (rb, 2 * na + c)),
            state2,
            state3(CONV_W - 1),
            state3(POOL_PAD),
            pl.BlockSpec((None, CONV_W, cbw), lambda c: (l, 0, c)), vec,
            pl.BlockSpec((None, heads, LRU_HEAD, LRU_HEAD), lambda c: (l, c, 0, 0)), vec,
            pl.BlockSpec((None, heads, LRU_HEAD, LRU_HEAD), lambda c: (l, c, 0, 0)), vec,
            vec,
            pl.BlockSpec((None, 1, POOL_GC, POOL_GC), lambda c: (l, c, 0, 0)), vec,
        ] + [pl.BlockSpec(memory_space=pl.ANY)] * len(aliased),
        out_specs=[
            pl.BlockSpec((N_SAMPLE, D_MODEL), lambda c: (rb, 0)),
            state2,
            state3(CONV_W - 1),
            state3(POOL_PAD),
        ],
        out_shape=[
            jax.ShapeDtypeStruct((N_TOK, D_MODEL), BF16),
            jax.ShapeDtypeStruct((DEPTH, DEC_BATCH, W_A), F32),
            jax.ShapeDtypeStruct((DEPTH, CONV_W - 1, DEC_BATCH, W_A), F32),
            jax.ShapeDtypeStruct((DEPTH, POOL_PAD, DEC_BATCH, W_B), F32),
        ],
        input_output_aliases={n_in + a: a for a in range(len(aliased))},
        compiler_params=_params("arbitrary"),
        name="sample_mixer",
    )(z, z, z, state_h, state_conv, state_pool, cw, cb, wr, br, wi, bi, lam, wp, ps, *aliased)


def _out_proj_kernel(y_ref, w_ref, x_ref, g_ref, o_ref, xg_ref, inv_ref):
    w = w_ref[...].astype(BF16)
    x_new = lambda rows: x_ref[rows, :] + _dot(y_ref[rows, :], w)
    _store_residual(x_new, g_ref, o_ref, xg_ref, inv_ref)


def _out_proj(y, w_all, l, x, g_next):
    m, k = y.shape
    n = w_all.shape[2]
    out_specs, out_shape = _residual_out(m, n, TM)
    return pl.pallas_call(
        _out_proj_kernel,
        grid=(m // TM, n // TN),
        in_specs=[
            pl.BlockSpec((TM, k), _row_map),
            pl.BlockSpec((None, k, TN), _layer_col_map(l, n // TN)),
            pl.BlockSpec((TM, TN), _tile_map(n // TN)),
            pl.BlockSpec((1, TN), _col_map(n // TN)),
        ],
        out_specs=out_specs,
        out_shape=out_shape,
        compiler_params=_params("parallel", "arbitrary"),
        name="out_proj",
    )(y, w_all, x, g_next)


def _ffn_up_kernel(xg_ref, inv_ref, wg_ref, wu_ref, wd_ref, h_ref, wdb_ref):
    wg = wg_ref[...].astype(BF16)
    wu = wu_ref[...].astype(BF16)
    for r0 in range(0, TM_UP, UP_CHUNK):
        rows = slice(r0, r0 + UP_CHUNK)
        xg = xg_ref[rows, :]
        inv = inv_ref.at[rows, :]
        gate = _scale_rows(_dot(xg, wg), inv)
        up = _scale_rows(_dot(xg, wu), inv)
        h_ref[rows, :] = (jax.nn.silu(gate) * up).astype(BF16)
    wdb_ref[...] = wd_ref[...].astype(BF16)


def _ffn_up(xg, inv, wg_all, wu_all, wd_all, l):
    m, k = xg.shape
    nf = D_FF // TF
    steps = (m // TM_UP) * nf
    wd_rows = D_FF // steps
    assert wd_rows * steps == D_FF and wd_rows % 16 == 0
    return pl.pallas_call(
        _ffn_up_kernel,
        grid=(m // TM_UP, nf),
        in_specs=[
            pl.BlockSpec((TM_UP, k), _row_map),
            pl.BlockSpec((TM_UP, LANES), _row_map, pipeline_mode=pl.Buffered(1)),
            pl.BlockSpec((None, k, TF), _layer_col_map(l, nf)),
            pl.BlockSpec((None, k, TF), _layer_col_map(l, nf)),
            pl.BlockSpec((None, wd_rows, k), lambda i, j: (l, i * nf + j, 0)),
        ],
        out_specs=[
            pl.BlockSpec((TM_UP, TF), _tile_map(nf)),
            pl.BlockSpec((wd_rows, k), lambda i, j: (i * nf + j, 0)),
        ],
        out_shape=[
            jax.ShapeDtypeStruct((m, D_FF), BF16),
            jax.ShapeDtypeStruct((D_FF, k), BF16),
        ],
        compiler_params=pltpu.CompilerParams(dimension_semantics=("arbitrary", "arbitrary"),
                                             vmem_limit_bytes=VMEM_LIMIT_UP_BYTES),
        name="ffn_up",
    )(xg, inv, wg_all, wu_all, wd_all)


def _ffn_down_kernel(h_ref, w_ref, x_ref, g_ref, o_ref, xg_ref, inv_ref):
    x_new = lambda rows: x_ref[rows, :] + _dot(h_ref[rows, :], w_ref[...])
    _store_residual(x_new, g_ref, o_ref, xg_ref, inv_ref)


def _ffn_down(h, wdb, x, g_next):
    m, kf = h.shape
    n = wdb.shape[1]
    out_specs, out_shape = _residual_out(m, n, TM_DOWN)
    return pl.pallas_call(
        _ffn_down_kernel,
        grid=(m // TM_DOWN, n // TN),
        in_specs=[
            pl.BlockSpec((TM_DOWN, kf), _row_map),
            pl.BlockSpec((kf, TN), _col_map(n // TN)),
            pl.BlockSpec((TM_DOWN, TN), _tile_map(n // TN)),
            pl.BlockSpec((1, TN), _col_map(n // TN)),
        ],
        out_specs=out_specs,
        out_shape=out_shape,
        compiler_params=_params("parallel", "arbitrary"),
        name="ffn_down",
    )(h, wdb, x, g_next)


def _ple_kernel(xg_ref, inv_ref, wg_ref, p_ref, wp_ref, x_ref, g_ref, o_ref, xgn_ref, invn_ref):
    wg = wg_ref[...].astype(BF16)
    wp = wp_ref[...].astype(BF16)

    def x_new(rows):
        gate = jax.nn.sigmoid(_scale_rows(_dot(xg_ref[rows, :], wg), inv_ref.at[rows, :]))
        return x_ref[rows, :] + gate * _dot(p_ref[rows, :], wp)

    _store_residual(x_new, g_ref, o_ref, xgn_ref, invn_ref)


def _ple(xg, inv, wg_all, p, wp_all, l, x, g_next):
    m, k = xg.shape
    out_specs, out_shape = _residual_out(m, k, TM)
    return pl.pallas_call(
        _ple_kernel,
        grid=(m // TM, k // TN),
        in_specs=[
            pl.BlockSpec((TM, k), _row_map),
            pl.BlockSpec((TM, LANES), _row_map),
            pl.BlockSpec((None, k, TN), _layer_col_map(l, k // TN)),
            pl.BlockSpec((None, TM, PLE_DIM), lambda i, j: (l, i, 0)),
            pl.BlockSpec((None, PLE_DIM, TN), _layer_col_map(l, k // TN)),
            pl.BlockSpec((TM, TN), _tile_map(k // TN)),
            pl.BlockSpec((1, TN), _col_map(k // TN)),
        ],
        out_specs=out_specs,
        out_shape=out_shape,
        compiler_params=_params("parallel", "arbitrary"),
        name="ple",
    )(xg, inv, wg_all, p, wp_all, x, g_next)


def _time_major(a):
    return jnp.swapaxes(a, 0, 1)


def kernel(x_prompt, x_sample, state_h, state_conv, state_pool, p_prompt, p_sample, g_mix, w_in, conv_w, conv_b, w_rg, b_rg, w_ig, b_ig, lam, w_pool, pool_scale, w_out, g_ffn, w_gate, w_up, w_down, g_pe, w_pe_gate, w_pe_proj, g_final):
    row = lambda v: v.reshape(1, -1)
    layer_rows = lambda v: v.reshape(DEPTH, 1, -1)
    x, xg, inv = _embed(x_prompt.reshape(N_PROMPT, D_MODEL),
                       _time_major(x_sample).reshape(N_SAMPLE, D_MODEL), row(g_mix[0]))
    mixer_params = (conv_w, layer_rows(conv_b),
                    w_rg.astype(BF16), layer_rows(b_rg), w_ig.astype(BF16), layer_rows(b_ig),
                    layer_rows(lam), w_pool.astype(BF16), layer_rows(pool_scale))
    p_all = jnp.concatenate(
        [p_prompt.reshape(DEPTH, N_PROMPT, PLE_DIM),
         jnp.swapaxes(p_sample, 1, 2).reshape(DEPTH, N_SAMPLE, PLE_DIM)], axis=1).astype(BF16)
    conv_tm, pool_tm = jnp.swapaxes(state_conv, 1, 2), jnp.swapaxes(state_pool, 1, 2)
    hp_l, cp_l, pp_l = [], [], []
    sample_states = None
    for l in range(DEPTH):
        g_after_ple = row(g_mix[l + 1]) if l + 1 < DEPTH else row(g_final)

        z = _in_proj(xg, inv, w_in, l)

        y, hp, ct, pt = _prompt_mixer(z, l, *mixer_params)
        y, *sample_states = _sample_mixer(z, y, state_h, conv_tm, pool_tm, sample_states, l,
                                          *mixer_params)

        x, xg, inv = _out_proj(y, w_out, l, x, row(g_ffn[l]))
        h, wdb = _ffn_up(xg, inv, w_gate, w_up, w_down, l)
        x, xg, inv = _ffn_down(h, wdb, x, row(g_pe[l]))
        x, xg, inv = _ple(xg, inv, w_pe_gate, p_all, w_pe_proj, l, x, g_after_ple)

        hp_l.append(hp.reshape(BATCH, W_A))
        cp_l.append(ct[:, CONV_CARRY - (CONV_W - 1):])
        pp_l.append(pt[:, POOL_CARRY - POOL_PAD:])

    hs_all, cs_tm, ps_tm = sample_states
    gf = row(g_final)
    y_prompt = _final_norm(x, inv, gf, N_PROMPT, 0).reshape(BATCH, SEQ, D_MODEL)
    y_s = _final_norm(x, inv, gf, N_SAMPLE, N_PROMPT)
    y_sample = _time_major(y_s.reshape(DEC_SEQ, DEC_BATCH, D_MODEL))
    return (y_prompt, y_sample,
            jnp.stack(hp_l), jnp.stack(cp_l), jnp.stack(pp_l),
            hs_all, jnp.swapaxes(cs_tm, 1, 2), jnp.swapaxes(ps_tm, 1, 2))
```

```python
import jax
import jax.numpy as jnp
from jax import lax
from jax.experimental import pallas as pl
from jax.experimental.pallas import tpu as pltpu

D_MODEL = 4096
BATCH = 4
SEQ = 2048
DEPTH = 2
DEC_BATCH = 128
DEC_SEQ = 4
PAST_LEN = 16384
W_A = D_MODEL // 2
W_B = D_MODEL - W_A
LRU_HEAD = 256
N_LRU_HEADS = W_A // LRU_HEAD
LRU_C = 8.0
CONV_W = 4
POOL_WINDOWS = (2, 4, 8, 16)
N_POOL_GROUPS = len(POOL_WINDOWS)
POOL_GC = W_B // N_POOL_GROUPS
POOL_PAD = max(POOL_WINDOWS) - 1
D_FF = 11008
PLE_DIM = 256
EPS = 1e-6

N_PROMPT = BATCH * SEQ
N_SAMPLE = DEC_BATCH * DEC_SEQ
N_TOK = N_PROMPT + N_SAMPLE
Z_W = 2 * W_A + W_B

F32 = jnp.float32
BF16 = jnp.bfloat16

VMEM_LIMIT_BYTES = 60 * 1024 * 1024
VMEM_LIMIT_UP_BYTES = 63 * 1024 * 1024

TM = 1088
TM_UP = 2176
UP_CHUNK = 272
ROW_CHUNK = 272
TM_DOWN = 544
TM_EMBED = 256
TM_LAST = 512
LANES = 128
TN = 512
TF = 256
TB = 512
CONV_CARRY = 8
POOL_CARRY = 16
SAMPLE_CB = 512


def _params(*sem):
    return pltpu.CompilerParams(dimension_semantics=sem, vmem_limit_bytes=VMEM_LIMIT_BYTES)


def _dot(a, b):
    return jnp.dot(a, b, preferred_element_type=F32)


def _snake(i, j, nj):
    return jnp.where(i % 2 == 0, j, nj - 1 - j)


def _row_map(i, j):
    return (i, 0)


def _tile_map(nj):
    return lambda i, j: (i, _snake(i, j, nj))


def _col_map(nj):
    return lambda i, j: (0, _snake(i, j, nj))


def _layer_col_map(l, nj):
    return lambda i, j: (l, 0, _snake(i, j, nj))


def _log_sigmoid(x):
    return jnp.minimum(x, 0.0) - jnp.log1p(jnp.exp(-jnp.abs(x)))


def _lru_coeffs(xc, r, i, logsig):
    log_a = r * (LRU_C * logsig)
    a = jnp.exp(log_a)
    u = 1.0 - a * a
    mult = jnp.where(u > 0.0, u * lax.rsqrt(u), 0.0)
    return a, mult, i * xc


def _lane_partial_sumsq(x):
    sq = x * x
    part = sq[:, 0:LANES]
    for c in range(1, x.shape[1] // LANES):
        part = part + sq[:, c * LANES:(c + 1) * LANES]
    return part


def _inv_rms_lanes(partial_sumsq):
    tot = jnp.sum(partial_sumsq, axis=-1, keepdims=True)
    return jnp.broadcast_to(lax.rsqrt(tot * (1.0 / D_MODEL) + EPS), partial_sumsq.shape)


def _scale_rows_by(acc, inv):
    return jnp.concatenate(
        [acc[:, c * LANES:(c + 1) * LANES] * inv for c in range(acc.shape[1] // LANES)], axis=1)


def _scale_rows(acc, inv_ref):
    return _scale_rows_by(acc, inv_ref[...])


def _store_residual(x_new, g_ref, o_ref, xg_ref, inv_ref):
    j, last = pl.program_id(1), pl.num_programs(1) - 1
    parts = []
    for r0 in range(0, o_ref.shape[0], ROW_CHUNK):
        rows = slice(r0, r0 + ROW_CHUNK)
        xc = x_new(rows)
        o_ref[rows, :] = xc
        xg_ref[rows, :] = (xc * g_ref[...]).astype(BF16)
        parts.append(_lane_partial_sumsq(xc))
    part = jnp.concatenate(parts, axis=0)

    @pl.when(j == 0)
    def _():
        inv_ref[...] = part

    @pl.when(jnp.logical_and(j > 0, j < last))
    def _():
        inv_ref[...] += part

    @pl.when(j == last)
    def _():
        inv_ref[...] = _inv_rms_lanes(inv_ref[...] + part)


def _residual_out(m, n, tm):
    specs = [
        pl.BlockSpec((tm, TN), _tile_map(n // TN)),
        pl.BlockSpec((tm, TN), _tile_map(n // TN)),
        pl.BlockSpec((tm, LANES), _row_map),
    ]
    shapes = [
        jax.ShapeDtypeStruct((m, n), F32),
        jax.ShapeDtypeStruct((m, n), BF16),
        jax.ShapeDtypeStruct((m, LANES), F32),
    ]
    return specs, shapes


def _embed_kernel(xp_ref, xs_ref, g_ref, x_ref, xg_ref, inv_ref):
    def emit(x):
        x_ref[...] = x
        xg_ref[...] = (x * g_ref[...]).astype(BF16)
        inv_ref[...] = _inv_rms_lanes(_lane_partial_sumsq(x))

    is_prompt = pl.program_id(0) < N_PROMPT // TM_EMBED

    @pl.when(is_prompt)
    def _():
        emit(xp_ref[...])

    @pl.when(jnp.logical_not(is_prompt))
    def _():
        emit(xs_ref[...])


def _embed(xp, xs, g):
    k = xp.shape[1]
    npb = N_PROMPT // TM_EMBED
    rows = lambda i: (i, 0)
    return pl.pallas_call(
        _embed_kernel,
        grid=(N_TOK // TM_EMBED,),
        in_specs=[
            pl.BlockSpec((TM_EMBED, k), lambda i: (jnp.minimum(i, npb - 1), 0)),
            pl.BlockSpec((TM_EMBED, k), lambda i: (jnp.maximum(i - npb, 0), 0)),
            pl.BlockSpec((1, k), lambda i: (0, 0)),
        ],
        out_specs=[
            pl.BlockSpec((TM_EMBED, k), rows),
            pl.BlockSpec((TM_EMBED, k), rows),
            pl.BlockSpec((TM_EMBED, LANES), rows),
        ],
        out_shape=[
            jax.ShapeDtypeStruct((N_TOK, k), F32),
            jax.ShapeDtypeStruct((N_TOK, k), BF16),
            jax.ShapeDtypeStruct((N_TOK, LANES), F32),
        ],
        compiler_params=_params("arbitrary"),
        name="embed",
    )(xp, xs, g)


def _in_proj_kernel(xg_ref, inv_ref, w_ref, o_ref):
    jb = _snake(pl.program_id(0), pl.program_id(1), pl.num_programs(1))
    is_gate = jnp.logical_and(jb >= W_A // TN, jb < 2 * W_A // TN)

    def project(act):
        w = w_ref[...].astype(BF16)
        for r0 in range(0, TM, ROW_CHUNK):
            rows = slice(r0, r0 + ROW_CHUNK)
            o_ref[rows, :] = act(_scale_rows(_dot(xg_ref[rows, :], w), inv_ref.at[rows, :]))

    @pl.when(is_gate)
    def _():
        project(jax.nn.gelu)

    @pl.when(jnp.logical_not(is_gate))
    def _():
        project(lambda z: z)


def _in_proj(xg, inv, w_all, l):
    m, k = xg.shape
    n = w_all.shape[2]
    return pl.pallas_call(
        _in_proj_kernel,
        grid=(m // TM, n // TN),
        in_specs=[
            pl.BlockSpec((TM, k), _row_map),
            pl.BlockSpec((TM, LANES), _row_map),
            pl.BlockSpec((None, k, TN), _layer_col_map(l, n // TN)),
        ],
        out_specs=pl.BlockSpec((TM, TN), _tile_map(n // TN)),
        out_shape=jax.ShapeDtypeStruct((m, n), F32),
        compiler_params=_params("parallel", "arbitrary"),
        name="in_proj",
    )(xg, inv, w_all)


def _prompt_mixer_kernel(xa_ref, ga_ref, ub_ref, cw_ref, cb_ref, wr_ref, br_ref, wi_ref, bi_ref,
                         lam_ref, wp_ref, ps_ref,
                         y_ref, hl_ref, ct_ref, pt_ref,
                         hc_ref, cc_ref, pc_ref, a_ref, b_ref):
    t = pl.program_id(1)

    @pl.when(t == 0)
    def _():
        hc_ref[...] = jnp.zeros_like(hc_ref)
        cc_ref[...] = jnp.zeros_like(cc_ref)
        pc_ref[...] = jnp.zeros_like(pc_ref)

    pos = lax.broadcasted_iota(jnp.int32, (TB, 1), 0) + t * TB

    xa = xa_ref[...]
    ext = jnp.concatenate([cc_ref[...], xa], axis=0)
    xc = cb_ref[...] + cw_ref[CONV_W - 1:CONV_W, :] * xa
    for s in range(1, CONV_W):
        xc = xc + cw_ref[CONV_W - 1 - s:CONV_W - s, :] * pltpu.roll(ext, s, axis=0)[CONV_CARRY:]
    xcb = xc.astype(BF16)
    r_parts, i_parts = [], []
    for hd in range(N_LRU_HEADS):
        sl = slice(hd * LRU_HEAD, (hd + 1) * LRU_HEAD)
        r_parts.append(_dot(xcb[:, sl], wr_ref[hd]))
        i_parts.append(_dot(xcb[:, sl], wi_ref[hd]))
    r = jax.nn.sigmoid(jnp.concatenate(r_parts, axis=1) + br_ref[...])
    i = jax.nn.sigmoid(jnp.concatenate(i_parts, axis=1) + bi_ref[...])
    a, mult, ix = _lru_coeffs(xc, r, i, _log_sigmoid(lam_ref[...]))
    a_ref[...] = a
    b_ref[...] = mult * ix

    @pl.when(t == 0)
    def _():
        a_ref[0:1, :] = jnp.zeros((1, W_A), F32)
        b_ref[0:1, :] = ix[0:1, :]

    def step(row, h):
        h = a_ref[pl.ds(row, 1), :] * h + b_ref[pl.ds(row, 1), :]
        b_ref[pl.ds(row, 1), :] = h
        return h

    h_last = lax.fori_loop(0, TB, step, hc_ref[0:1, :], unroll=8)
    hc_ref[0:1, :] = h_last
    hl_ref[0] = h_last
    y_ref[:, 0:W_A] = (ga_ref[...] * b_ref[...]).astype(BF16)
    cc_ref[...] = xa[TB - CONV_CARRY:]
    ct_ref[0] = xa[TB - CONV_CARRY:]

    ub = ub_ref[...]
    pext = jnp.concatenate([pc_ref[...], ub], axis=0)
    for g, w in enumerate(POOL_WINDOWS):
        sl = slice(g * POOL_GC, (g + 1) * POOL_GC)
        s = pext[:, sl]
        shift = 1
        while shift < w:
            s = s + pltpu.roll(s, shift, axis=0)
            shift *= 2
        cnt = jnp.minimum(pos + 1, w).astype(F32)
        d = s[POOL_CARRY:] / cnt - ub[:, sl]
        yb = _dot(d.astype(BF16), wp_ref[g]) * ps_ref[:, sl]
        y_ref[:, W_A + g * POOL_GC:W_A + (g + 1) * POOL_GC] = yb.astype(BF16)
    pc_ref[...] = ub[TB - POOL_CARRY:]
    pt_ref[0] = ub[TB - POOL_CARRY:]


def _prompt_mixer(z, l, cw, cb, wr, br, wi, bi, lam, wp, ps):
    nt = SEQ // TB
    row = lambda b, t: b * nt + t
    full2 = lambda shape: pl.BlockSpec((None,) + shape, lambda b, t: (l, 0, 0))
    full3 = lambda shape: pl.BlockSpec((None,) + shape, lambda b, t: (l, 0, 0, 0))
    return pl.pallas_call(
        _prompt_mixer_kernel,
        grid=(BATCH, nt),
        in_specs=[
            pl.BlockSpec((TB, W_A), lambda b, t: (row(b, t), 0)),
            pl.BlockSpec((TB, W_A), lambda b, t: (row(b, t), 1)),
            pl.BlockSpec((TB, W_B), lambda b, t: (row(b, t), 2)),
            full2((CONV_W, W_A)), full2((1, W_A)),
            full3((N_LRU_HEADS, LRU_HEAD, LRU_HEAD)), full2((1, W_A)),
            full3((N_LRU_HEADS, LRU_HEAD, LRU_HEAD)), full2((1, W_A)),
            full2((1, W_A)),
            full3((N_POOL_GROUPS, POOL_GC, POOL_GC)), full2((1, W_B)),
        ],
        out_specs=[
            pl.BlockSpec((TB, D_MODEL), lambda b, t: (row(b, t), 0)),
            pl.BlockSpec((1, 1, W_A), lambda b, t: (b, 0, 0)),
            pl.BlockSpec((1, CONV_CARRY, W_A), lambda b, t: (b, 0, 0)),
            pl.BlockSpec((1, POOL_CARRY, W_B), lambda b, t: (b, 0, 0)),
        ],
        out_shape=[
            jax.ShapeDtypeStruct((N_TOK, D_MODEL), BF16),
            jax.ShapeDtypeStruct((BATCH, 1, W_A), F32),
            jax.ShapeDtypeStruct((BATCH, CONV_CARRY, W_A), F32),
            jax.ShapeDtypeStruct((BATCH, POOL_CARRY, W_B), F32),
        ],
        scratch_shapes=[
            pltpu.VMEM((8, W_A), F32),
            pltpu.VMEM((CONV_CARRY, W_A), F32),
            pltpu.VMEM((POOL_CARRY, W_B), F32),
            pltpu.VMEM((TB, W_A), F32),
            pltpu.VMEM((TB, W_A), F32),
        ],
        compiler_params=_params("parallel", "arbitrary"),
        name="prompt_mixer",
    )(z, z, z, cw, cb, wr, br, wi, bi, lam, wp, ps)


def _sample_mixer_kernel(xa_ref, ga_ref, ub_ref, h0_ref, cs_ref, pst_ref,
                         cw_ref, cb_ref, wr_ref, br_ref, wi_ref, bi_ref, lam_ref, wp_ref, ps_ref,
                         *aliased_and_out_refs):
    y_ref, hn_ref, cn_ref, pn_ref = aliased_and_out_refs[-4:]
    c = pl.program_id(0)
    heads = SAMPLE_CB // LRU_HEAD
    rows = lambda t: slice(t * DEC_BATCH, (t + 1) * DEC_BATCH)

    ext = [cs_ref[k] for k in range(CONV_W - 1)] + [xa_ref[rows(t), :] for t in range(DEC_SEQ)]
    logsig = _log_sigmoid(lam_ref[...])
    h = h0_ref[...]
    ya = []
    for t in range(DEC_SEQ):
        xc = cb_ref[...] + cw_ref[0:1, :] * ext[t]
        for k in range(1, CONV_W):
            xc = xc + cw_ref[k:k + 1, :] * ext[t + k]
        xcb = xc.astype(BF16)
        r_parts, i_parts = [], []
        for hd in range(heads):
            sl = slice(hd * LRU_HEAD, (hd + 1) * LRU_HEAD)
            r_parts.append(_dot(xcb[:, sl], wr_ref[hd]))
            i_parts.append(_dot(xcb[:, sl], wi_ref[hd]))
        r = jax.nn.sigmoid(jnp.concatenate(r_parts, axis=1) + br_ref[...])
        i = jax.nn.sigmoid(jnp.concatenate(i_parts, axis=1) + bi_ref[...])
        a, mult, ix = _lru_coeffs(xc, r, i, logsig)
        if PAST_LEN + t == 0:
            a, mult = jnp.zeros_like(a), jnp.ones_like(mult)
        h = a * h + mult * ix
        ya.append((ga_ref[rows(t), :] * h).astype(BF16))
    hn_ref[...] = h
    for k in range(CONV_W - 1):
        cn_ref[k] = ext[DEC_SEQ + k]

    for k in range(POOL_PAD):
        src = k + DEC_SEQ
        pn_ref[k] = pst_ref[src] if src < POOL_PAD else ub_ref[rows(src - POOL_PAD), :]

    for g, w in enumerate(POOL_WINDOWS):
        @pl.when(c == g)
        def _(g=g, w=w):
            pext = ([pst_ref[k] for k in range(POOL_PAD - w + 1, POOL_PAD)]
                    + [ub_ref[rows(t), :] for t in range(DEC_SEQ)])
            for t in range(DEC_SEQ):
                s = pext[t]
                for j in range(1, w):
                    s = s + pext[t + j]
                cnt = float(min(PAST_LEN + t + 1, w))
                d = s / cnt - pext[t + w - 1]
                yb = _dot(d.astype(BF16), wp_ref[0]) * ps_ref[...]
                y_ref[rows(t), g * SAMPLE_CB:(g + 1) * SAMPLE_CB] = ya[t]
                y_ref[rows(t), W_A + g * POOL_GC:W_A + (g + 1) * POOL_GC] = yb.astype(BF16)


def _sample_mixer(z, y, state_h, state_conv, state_pool, prev_states, l,
                  cw, cb, wr, br, wi, bi, lam, wp, ps):
    cbw = SAMPLE_CB
    na = W_A // cbw
    heads = cbw // LRU_HEAD
    rb = N_PROMPT // N_SAMPLE
    vec = pl.BlockSpec((None, 1, cbw), lambda c: (l, 0, c))
    state3 = lambda steps: pl.BlockSpec((None, steps, DEC_BATCH, cbw), lambda c: (l, 0, 0, c))
    state2 = pl.BlockSpec((None, DEC_BATCH, cbw), lambda c: (l, 0, c))
    aliased = (y,) + (() if prev_states is None else tuple(prev_states))
    n_in = 15
    return pl.pallas_call(
        _sample_mixer_kernel,
        grid=(na,),
        in_specs=[
            pl.BlockSpec((N_SAMPLE, cbw), lambda c: (rb, c)),
            pl.BlockSpec((N_SAMPLE, cbw), lambda c: (rb, na + c)),
            pl.BlockSpec((N_SAMPLE, cbw), lambda c: (rb, 2 * na + c)),
            state2,
            state3(CONV_W - 1),
            state3(POOL_PAD),
            pl.BlockSpec((None, CONV_W, cbw), lambda c: (l, 0, c)), vec,
            pl.BlockSpec((None, heads, LRU_HEAD, LRU_HEAD), lambda c: (l, c, 0, 0)), vec,
            pl.BlockSpec((None, heads, LRU_HEAD, LRU_HEAD), lambda c: (l, c, 0, 0)), vec,
            vec,
            pl.BlockSpec((None, 1, POOL_GC, POOL_GC), lambda c: (l, c, 0, 0)), vec,
        ] + [pl.BlockSpec(memory_space=pl.ANY)] * len(aliased),
        out_specs=[
            pl.BlockSpec((N_SAMPLE, D_MODEL), lambda c: (rb, 0)),
            state2,
            state3(CONV_W - 1),
            state3(POOL_PAD),
        ],
        out_shape=[
            jax.ShapeDtypeStruct((N_TOK, D_MODEL), BF16),
            jax.ShapeDtypeStruct((DEPTH, DEC_BATCH, W_A), F32),
            jax.ShapeDtypeStruct((DEPTH, CONV_W - 1, DEC_BATCH, W_A), F32),
            jax.ShapeDtypeStruct((DEPTH, POOL_PAD, DEC_BATCH, W_B), F32),
        ],
        input_output_aliases={n_in + a: a for a in range(len(aliased))},
        compiler_params=_params("arbitrary"),
        name="sample_mixer",
    )(z, z, z, state_h, state_conv, state_pool, cw, cb, wr, br, wi, bi, lam, wp, ps, *aliased)


def _out_proj_kernel(y_ref, w_ref, x_ref, g_ref, o_ref, xg_ref, inv_ref):
    w = w_ref[...].astype(BF16)
    x_new = lambda rows: x_ref[rows, :] + _dot(y_ref[rows, :], w)
    _store_residual(x_new, g_ref, o_ref, xg_ref, inv_ref)


def _out_proj(y, w_all, l, x, g_next):
    m, k = y.shape
    n = w_all.shape[2]
    out_specs, out_shape = _residual_out(m, n, TM)
    return pl.pallas_call(
        _out_proj_kernel,
        grid=(m // TM, n // TN),
        in_specs=[
            pl.BlockSpec((TM, k), _row_map),
            pl.BlockSpec((None, k, TN), _layer_col_map(l, n // TN)),
            pl.BlockSpec((TM, TN), _tile_map(n // TN)),
            pl.BlockSpec((1, TN), _col_map(n // TN)),
        ],
        out_specs=out_specs,
        out_shape=out_shape,
        compiler_params=_params("parallel", "arbitrary"),
        name="out_proj",
    )(y, w_all, x, g_next)


def _ffn_up_kernel(xg_ref, inv_ref, wg_ref, wu_ref, wd_ref, h_ref, wdb_ref):
    wg = wg_ref[...].astype(BF16)
    wu = wu_ref[...].astype(BF16)
    for r0 in range(0, TM_UP, UP_CHUNK):
        rows = slice(r0, r0 + UP_CHUNK)
        xg = xg_ref[rows, :]
        inv = inv_ref.at[rows, :]
        gate = _scale_rows(_dot(xg, wg), inv)
        up = _scale_rows(_dot(xg, wu), inv)
        h_ref[rows, :] = (jax.nn.silu(gate) * up).astype(BF16)
    wdb_ref[...] = wd_ref[...].astype(BF16)


def _ffn_up(xg, inv, wg_all, wu_all, wd_all, l):
    m, k = xg.shape
    nf = D_FF // TF
    steps = (m // TM_UP) * nf
    wd_rows = D_FF // steps
    assert wd_rows * steps == D_FF and wd_rows % 16 == 0
    return pl.pallas_call(
        _ffn_up_kernel,
        grid=(m // TM_UP, nf),
        in_specs=[
            pl.BlockSpec((TM_UP, k), _row_map),
            pl.BlockSpec((TM_UP, LANES), _row_map, pipeline_mode=pl.Buffered(1)),
            pl.BlockSpec((None, k, TF), _layer_col_map(l, nf)),
            pl.BlockSpec((None, k, TF), _layer_col_map(l, nf)),
            pl.BlockSpec((None, wd_rows, k), lambda i, j: (l, i * nf + j, 0)),
        ],
        out_specs=[
            pl.BlockSpec((TM_UP, TF), _tile_map(nf)),
            pl.BlockSpec((wd_rows, k), lambda i, j: (i * nf + j, 0)),
        ],
        out_shape=[
            jax.ShapeDtypeStruct((m, D_FF), BF16),
            jax.ShapeDtypeStruct((D_FF, k), BF16),
        ],
        compiler_params=pltpu.CompilerParams(dimension_semantics=("arbitrary", "arbitrary"),
                                             vmem_limit_bytes=VMEM_LIMIT_UP_BYTES),
        name="ffn_up",
    )(xg, inv, wg_all, wu_all, wd_all)


def _ffn_down_kernel(h_ref, w_ref, x_ref, g_ref, o_ref, xg_ref, inv_ref):
    x_new = lambda rows: x_ref[rows, :] + _dot(h_ref[rows, :], w_ref[...])
    _store_residual(x_new, g_ref, o_ref, xg_ref, inv_ref)


def _ffn_down(h, wdb, x, g_next):
    m, kf = h.shape
    n = wdb.shape[1]
    out_specs, out_shape = _residual_out(m, n, TM_DOWN)
    return pl.pallas_call(
        _ffn_down_kernel,
        grid=(m // TM_DOWN, n // TN),
        in_specs=[
            pl.BlockSpec((TM_DOWN, kf), _row_map),
            pl.BlockSpec((kf, TN), _col_map(n // TN)),
            pl.BlockSpec((TM_DOWN, TN), _tile_map(n // TN)),
            pl.BlockSpec((1, TN), _col_map(n // TN)),
        ],
        out_specs=out_specs,
        out_shape=out_shape,
        compiler_params=_params("parallel", "arbitrary"),
        name="ffn_down",
    )(h, wdb, x, g_next)


def _ple_kernel(xg_ref, inv_ref, wg_ref, p_ref, wp_ref, x_ref, g_ref, o_ref, xgn_ref, invn_ref):
    wg = wg_ref[...].astype(BF16)
    wp = wp_ref[...].astype(BF16)

    def x_new(rows):
        gate = jax.nn.sigmoid(_scale_rows(_dot(xg_ref[rows, :], wg), inv_ref.at[rows, :]))
        return x_ref[rows, :] + gate * _dot(p_ref[rows, :], wp)

    _store_residual(x_new, g_ref, o_ref, xgn_ref, invn_ref)


def _ple(xg, inv, wg_all, p, wp_all, l, x, g_next):
    m, k = xg.shape
    out_specs, out_shape = _residual_out(m, k, TM)
    return pl.pallas_call(
        _ple_kernel,
        grid=(m // TM, k // TN),
        in_specs=[
            pl.BlockSpec((TM, k), _row_map),
            pl.BlockSpec((TM, LANES), _row_map),
            pl.BlockSpec((None, k, TN), _layer_col_map(l, k // TN)),
            pl.BlockSpec((None, TM, PLE_DIM), lambda i, j: (l, i, 0)),
            pl.BlockSpec((None, PLE_DIM, TN), _layer_col_map(l, k // TN)),
            pl.BlockSpec((TM, TN), _tile_map(k // TN)),
            pl.BlockSpec((1, TN), _col_map(k // TN)),
        ],
        out_specs=out_specs,
        out_shape=out_shape,
        compiler_params=_params("parallel", "arbitrary"),
        name="ple",
    )(xg, inv, wg_all, p, wp_all, x, g_next)


def _ple_final_kernel(xg_ref, inv_ref, wg_ref, p_ref, wp_ref, x_ref, gf_ref, yp_hbm, ys_hbm,
                      panel_ref, ss_ref, sem):
    i, j = pl.program_id(0), pl.program_id(1)
    ni, nj = pl.num_programs(0), pl.num_programs(1)
    slot = i % 2
    jb = _snake(i, j, nj)

    def prompt_rows_copy(tile, s):
        rows = pl.ds(pl.multiple_of(tile * TM_LAST, TM_LAST), TM_LAST)
        return pltpu.make_async_copy(panel_ref.at[s], yp_hbm.at[rows], sem.at[s])

    def sample_rows_copy(s):
        return pltpu.make_async_copy(panel_ref.at[s], ys_hbm, sem.at[s])

    @pl.when(jnp.logical_and(j == 0, i >= 2))
    def _():
        prompt_rows_copy(i - 2, slot).wait()

    wg = wg_ref[...].astype(BF16)
    wp = wp_ref[...].astype(BF16)
    gate = jax.nn.sigmoid(_scale_rows(_dot(xg_ref[...], wg), inv_ref))
    x_new = x_ref[...] + gate * _dot(p_ref[...], wp)
    for c in range(D_MODEL // TN):
        @pl.when(jb == c)
        def _(c=c):
            panel_ref[slot, :, c * TN:(c + 1) * TN] = x_new
    part = _lane_partial_sumsq(x_new)

    @pl.when(j == 0)
    def _():
        ss_ref[...] = part

    @pl.when(j > 0)
    def _():
        ss_ref[...] += part

    @pl.when(j == nj - 1)
    def _():
        inv_new = _inv_rms_lanes(ss_ref[...])
        panel_ref[slot] = _scale_rows_by(panel_ref[slot], inv_new) * gf_ref[...]

        @pl.when(i < ni - 1)
        def _():
            prompt_rows_copy(i, slot).start()

        @pl.when(i == ni - 1)
        def _():
            sample_rows_copy(slot).start()
            prompt_rows_copy(i - 1, 1 - slot).wait()
            sample_rows_copy(slot).wait()


def _ple_final(xg, inv, wg_all, p, wp_all, l, x, g_final):
    m, k = xg.shape
    nj = k // TN
    assert N_PROMPT % TM_LAST == 0 and m - N_PROMPT == TM_LAST and m // TM_LAST >= 3
    return pl.pallas_call(
        _ple_final_kernel,
        grid=(m // TM_LAST, nj),
        in_specs=[
            pl.BlockSpec((TM_LAST, k), _row_map),
            pl.BlockSpec((TM_LAST, LANES), _row_map),
            pl.BlockSpec((None, k, TN), _layer_col_map(l, nj)),
            pl.BlockSpec((None, TM_LAST, PLE_DIM), lambda i, j: (l, i, 0)),
            pl.BlockSpec((None, PLE_DIM, TN), _layer_col_map(l, nj)),
            pl.BlockSpec((TM_LAST, TN), _tile_map(nj)),
            pl.BlockSpec((1, k), lambda i, j: (0, 0)),
        ],
        out_specs=[pl.BlockSpec(memory_space=pl.ANY), pl.BlockSpec(memory_space=pl.ANY)],
        out_shape=[
            jax.ShapeDtypeStruct((N_PROMPT, k), F32),
            jax.ShapeDtypeStruct((m - N_PROMPT, k), F32),
        ],
        scratch_shapes=[
            pltpu.VMEM((2, TM_LAST, k), F32),
            pltpu.VMEM((TM_LAST, LANES), F32),
            pltpu.SemaphoreType.DMA((2,)),
        ],
        compiler_params=_params("arbitrary", "arbitrary"),
        name="ple_final",
    )(xg, inv, wg_all, p, wp_all, x, g_final)


def _time_major(a):
    return jnp.swapaxes(a, 0, 1)


def kernel(x_prompt, x_sample, state_h, state_conv, state_pool, p_prompt, p_sample, g_mix, w_in, conv_w, conv_b, w_rg, b_rg, w_ig, b_ig, lam, w_pool, pool_scale, w_out, g_ffn, w_gate, w_up, w_down, g_pe, w_pe_gate, w_pe_proj, g_final):
    row = lambda v: v.reshape(1, -1)
    layer_rows = lambda v: v.reshape(DEPTH, 1, -1)
    x, xg, inv = _embed(x_prompt.reshape(N_PROMPT, D_MODEL),
                       _time_major(x_sample).reshape(N_SAMPLE, D_MODEL), row(g_mix[0]))
    mixer_params = (conv_w, layer_rows(conv_b),
                    w_rg.astype(BF16), layer_rows(b_rg), w_ig.astype(BF16), layer_rows(b_ig),
                    layer_rows(lam), w_pool.astype(BF16), layer_rows(pool_scale))
    p_all = jnp.concatenate(
        [p_prompt.reshape(DEPTH, N_PROMPT, PLE_DIM),
         jnp.swapaxes(p_sample, 1, 2).reshape(DEPTH, N_SAMPLE, PLE_DIM)], axis=1).astype(BF16)
    conv_tm, pool_tm = jnp.swapaxes(state_conv, 1, 2), jnp.swapaxes(state_pool, 1, 2)
    hp_l, cp_l, pp_l = [], [], []
    sample_states = None
    for l in range(DEPTH):
        z = _in_proj(xg, inv, w_in, l)

        y, hp, ct, pt = _prompt_mixer(z, l, *mixer_params)
        y, *sample_states = _sample_mixer(z, y, state_h, conv_tm, pool_tm, sample_states, l,
                                          *mixer_params)

        x, xg, inv = _out_proj(y, w_out, l, x, row(g_ffn[l]))
        h, wdb = _ffn_up(xg, inv, w_gate, w_up, w_down, l)
        x, xg, inv = _ffn_down(h, wdb, x, row(g_pe[l]))
        if l + 1 < DEPTH:
            x, xg, inv = _ple(xg, inv, w_pe_gate, p_all, w_pe_proj, l, x, row(g_mix[l + 1]))
        else:
            y_p, y_s = _ple_final(xg, inv, w_pe_gate, p_all, w_pe_proj, l, x, row(g_final))

        hp_l.append(hp.reshape(BATCH, W_A))
        cp_l.append(ct[:, CONV_CARRY - (CONV_W - 1):])
        pp_l.append(pt[:, POOL_CARRY - POOL_PAD:])

    hs_all, cs_tm, ps_tm = sample_states
    y_prompt = y_p.reshape(BATCH, SEQ, D_MODEL)
    y_sample = _time_major(y_s.reshape(DEC_SEQ, DEC_BATCH, D_MODEL))
    return (y_prompt, y_sample,
            jnp.stack(hp_l), jnp.stack(cp_l), jnp.stack(pp_l),
            hs_all, jnp.swapaxes(cs_tm, 1, 2), jnp.swapaxes(ps_tm, 1, 2))
```

```python
import jax
import jax.numpy as jnp
from jax import lax
from jax.experimental import pallas as pl
from jax.experimental.pallas import tpu as pltpu

D_MODEL = 4096
BATCH = 4
SEQ = 2048
DEPTH = 2
DEC_BATCH = 128
DEC_SEQ = 4
PAST_LEN = 16384
W_A = D_MODEL // 2
W_B = D_MODEL - W_A
LRU_HEAD = 256
N_LRU_HEADS = W_A // LRU_HEAD
LRU_C = 8.0
CONV_W = 4
POOL_WINDOWS = (2, 4, 8, 16)
N_POOL_GROUPS = len(POOL_WINDOWS)
POOL_GC = W_B // N_POOL_GROUPS
POOL_PAD = max(POOL_WINDOWS) - 1
D_FF = 11008
PLE_DIM = 256
EPS = 1e-6

N_PROMPT = BATCH * SEQ
N_SAMPLE = DEC_BATCH * DEC_SEQ
N_TOK = N_PROMPT + N_SAMPLE
Z_W = 2 * W_A + W_B

F32 = jnp.float32
BF16 = jnp.bfloat16

VMEM_LIMIT_BYTES = 60 * 1024 * 1024
VMEM_LIMIT_UP_BYTES = 63 * 1024 * 1024

TM = 1088
TM_UP = 2176
UP_CHUNK = 272
ROW_CHUNK = 272
TM_DOWN = 544
TM_EMBED = 256
TM_FINAL = 512
LANES = 128
TN = 512
TF = 256
TB = 256
SEQ_PER_STEP = 2
CONV_CARRY = 8
POOL_CARRY = 16
SAMPLE_CB = 512


def _params(*sem):
    return pltpu.CompilerParams(dimension_semantics=sem, vmem_limit_bytes=VMEM_LIMIT_BYTES)


def _dot(a, b):
    return jnp.dot(a, b, preferred_element_type=F32)


def _snake(i, j, nj):
    return jnp.where(i % 2 == 0, j, nj - 1 - j)


def _row_map(i, j):
    return (i, 0)


def _tile_map(nj):
    return lambda i, j: (i, _snake(i, j, nj))


def _col_map(nj):
    return lambda i, j: (0, _snake(i, j, nj))


def _layer_col_map(l, nj):
    return lambda i, j: (l, 0, _snake(i, j, nj))


def _log_sigmoid(x):
    return jnp.minimum(x, 0.0) - jnp.log1p(jnp.exp(-jnp.abs(x)))


def _lru_coeffs(xc, r, i, logsig):
    log_a = r * (LRU_C * logsig)
    a = jnp.exp(log_a)
    u = 1.0 - a * a
    mult = jnp.where(u > 0.0, u * lax.rsqrt(u), 0.0)
    return a, mult, i * xc


def _lane_partial_sumsq(x):
    sq = x * x
    part = sq[:, 0:LANES]
    for c in range(1, x.shape[1] // LANES):
        part = part + sq[:, c * LANES:(c + 1) * LANES]
    return part


def _inv_rms_lanes(partial_sumsq):
    tot = jnp.sum(partial_sumsq, axis=-1, keepdims=True)
    return jnp.broadcast_to(lax.rsqrt(tot * (1.0 / D_MODEL) + EPS), partial_sumsq.shape)


def _scale_rows(acc, inv_ref):
    inv = inv_ref[...]
    return jnp.concatenate(
        [acc[:, c * LANES:(c + 1) * LANES] * inv for c in range(acc.shape[1] // LANES)], axis=1)


def _store_residual(x_new, g_ref, o_ref, xg_ref, inv_ref):
    j, last = pl.program_id(1), pl.num_programs(1) - 1
    parts = []
    for r0 in range(0, o_ref.shape[0], ROW_CHUNK):
        rows = slice(r0, r0 + ROW_CHUNK)
        xc = x_new(rows)
        o_ref[rows, :] = xc
        xg_ref[rows, :] = (xc * g_ref[...]).astype(BF16)
        parts.append(_lane_partial_sumsq(xc))
    part = jnp.concatenate(parts, axis=0)

    @pl.when(j == 0)
    def _():
        inv_ref[...] = part

    @pl.when(jnp.logical_and(j > 0, j < last))
    def _():
        inv_ref[...] += part

    @pl.when(j == last)
    def _():
        inv_ref[...] = _inv_rms_lanes(inv_ref[...] + part)


def _residual_out(m, n, tm):
    specs = [
        pl.BlockSpec((tm, TN), _tile_map(n // TN)),
        pl.BlockSpec((tm, TN), _tile_map(n // TN)),
        pl.BlockSpec((tm, LANES), _row_map),
    ]
    shapes = [
        jax.ShapeDtypeStruct((m, n), F32),
        jax.ShapeDtypeStruct((m, n), BF16),
        jax.ShapeDtypeStruct((m, LANES), F32),
    ]
    return specs, shapes


def _embed_kernel(xp_ref, xs_ref, g_ref, x_ref, xg_ref, inv_ref):
    def emit(x):
        x_ref[...] = x
        xg_ref[...] = (x * g_ref[...]).astype(BF16)
        inv_ref[...] = _inv_rms_lanes(_lane_partial_sumsq(x))

    is_prompt = pl.program_id(0) < N_PROMPT // TM_EMBED

    @pl.when(is_prompt)
    def _():
        emit(xp_ref[...])

    @pl.when(jnp.logical_not(is_prompt))
    def _():
        emit(xs_ref[...])


def _embed(xp, xs, g):
    k = xp.shape[1]
    npb = N_PROMPT // TM_EMBED
    rows = lambda i: (i, 0)
    return pl.pallas_call(
        _embed_kernel,
        grid=(N_TOK // TM_EMBED,),
        in_specs=[
            pl.BlockSpec((TM_EMBED, k), lambda i: (jnp.minimum(i, npb - 1), 0)),
            pl.BlockSpec((TM_EMBED, k), lambda i: (jnp.maximum(i - npb, 0), 0)),
            pl.BlockSpec((1, k), lambda i: (0, 0)),
        ],
        out_specs=[
            pl.BlockSpec((TM_EMBED, k), rows),
            pl.BlockSpec((TM_EMBED, k), rows),
            pl.BlockSpec((TM_EMBED, LANES), rows),
        ],
        out_shape=[
            jax.ShapeDtypeStruct((N_TOK, k), F32),
            jax.ShapeDtypeStruct((N_TOK, k), BF16),
            jax.ShapeDtypeStruct((N_TOK, LANES), F32),
        ],
        compiler_params=_params("arbitrary"),
        name="embed",
    )(xp, xs, g)


def _final_norm_kernel(x_ref, inv_ref, g_ref, o_ref):
    o_ref[...] = _scale_rows(x_ref[...], inv_ref) * g_ref[...]


def _final_norm(x, inv, g, rows, row0):
    k = x.shape[1]
    tm = TM_FINAL
    assert rows % tm == 0 and row0 % tm == 0
    rb0 = row0 // tm
    return pl.pallas_call(
        _final_norm_kernel,
        grid=(rows // tm,),
        in_specs=[pl.BlockSpec((tm, k), lambda i: (rb0 + i, 0)),
                  pl.BlockSpec((tm, LANES), lambda i: (rb0 + i, 0)),
                  pl.BlockSpec((1, k), lambda i: (0, 0))],
        out_specs=pl.BlockSpec((tm, k), lambda i: (i, 0)),
        out_shape=jax.ShapeDtypeStruct((rows, k), F32),
        compiler_params=_params("parallel"),
        name="final_norm",
    )(x, inv, g)


def _in_proj_kernel(xg_ref, inv_ref, w_ref, o_ref):
    jb = _snake(pl.program_id(0), pl.program_id(1), pl.num_programs(1))
    is_gate = jnp.logical_and(jb >= W_A // TN, jb < 2 * W_A // TN)

    def project(act):
        w = w_ref[...].astype(BF16)
        for r0 in range(0, TM, ROW_CHUNK):
            rows = slice(r0, r0 + ROW_CHUNK)
            o_ref[rows, :] = act(_scale_rows(_dot(xg_ref[rows, :], w), inv_ref.at[rows, :]))

    @pl.when(is_gate)
    def _():
        project(jax.nn.gelu)

    @pl.when(jnp.logical_not(is_gate))
    def _():
        project(lambda z: z)


def _in_proj(xg, inv, w_all, l):
    m, k = xg.shape
    n = w_all.shape[2]
    return pl.pallas_call(
        _in_proj_kernel,
        grid=(m // TM, n // TN),
        in_specs=[
            pl.BlockSpec((TM, k), _row_map),
            pl.BlockSpec((TM, LANES), _row_map),
            pl.BlockSpec((None, k, TN), _layer_col_map(l, n // TN)),
        ],
        out_specs=pl.BlockSpec((TM, TN), _tile_map(n // TN)),
        out_shape=jax.ShapeDtypeStruct((m, n), F32),
        compiler_params=_params("parallel", "arbitrary"),
        name="in_proj",
    )(xg, inv, w_all)


def _prompt_mixer_kernel(*refs):
    ns = SEQ_PER_STEP
    z_refs = refs[:3 * ns]
    cw_ref, cb_ref, wr_ref, br_ref, wi_ref, bi_ref, lam_ref, wp_ref, ps_ref = refs[3 * ns:3 * ns + 9]
    y_hbm, hl_ref, ct_ref, pt_ref = refs[3 * ns + 9:3 * ns + 13]
    hc_ref, cc_ref, pc_ref, a_ref, b_ref, ybuf, sem = refs[3 * ns + 13:]
    bp, t = pl.program_id(0), pl.program_id(1)
    nt = pl.num_programs(1)
    step_no = bp * nt + t
    last_step = pl.num_programs(0) * nt - 1
    slot = step_no % 2

    def y_copy(s, sl):
        row0 = pl.multiple_of(((bp * ns + s) * nt + t) * TB, TB)
        return pltpu.make_async_copy(ybuf.at[sl, s], y_hbm.at[pl.ds(row0, TB)], sem.at[sl, s])

    @pl.when(step_no >= 2)
    def _():
        for s in range(ns):
            y_copy(s, slot).wait()

    @pl.when(t == 0)
    def _():
        hc_ref[...] = jnp.zeros_like(hc_ref)
        cc_ref[...] = jnp.zeros_like(cc_ref)
        pc_ref[...] = jnp.zeros_like(pc_ref)

    pos = lax.broadcasted_iota(jnp.int32, (TB, 1), 0) + t * TB
    logsig = _log_sigmoid(lam_ref[...])

    for s in range(ns):
        xa = z_refs[3 * s][...]
        ext = jnp.concatenate([cc_ref[s], xa], axis=0)
        xc = cb_ref[...] + cw_ref[CONV_W - 1:CONV_W, :] * xa
        for sh in range(1, CONV_W):
            xc = xc + cw_ref[CONV_W - 1 - sh:CONV_W - sh, :] * pltpu.roll(ext, sh, axis=0)[CONV_CARRY:]
        xcb = xc.astype(BF16)
        r_parts, i_parts = [], []
        for hd in range(N_LRU_HEADS):
            sl = slice(hd * LRU_HEAD, (hd + 1) * LRU_HEAD)
            r_parts.append(_dot(xcb[:, sl], wr_ref[hd]))
            i_parts.append(_dot(xcb[:, sl], wi_ref[hd]))
        r = jax.nn.sigmoid(jnp.concatenate(r_parts, axis=1) + br_ref[...])
        i = jax.nn.sigmoid(jnp.concatenate(i_parts, axis=1) + bi_ref[...])
        a, mult, ix = _lru_coeffs(xc, r, i, logsig)
        a_ref[s] = a
        b_ref[s] = mult * ix

        @pl.when(t == 0)
        def _(s=s, ix=ix):
            a_ref[s, 0:1, :] = jnp.zeros((1, W_A), F32)
            b_ref[s, 0:1, :] = ix[0:1, :]

        cc_ref[s] = xa[TB - CONV_CARRY:]
        ct_ref[s] = xa[TB - CONV_CARRY:]

    def step(row, hs):
        out = []
        for s in range(ns):
            h = a_ref[s, pl.ds(row, 1), :] * hs[s] + b_ref[s, pl.ds(row, 1), :]
            b_ref[s, pl.ds(row, 1), :] = h
            out.append(h)
        return tuple(out)

    hs = lax.fori_loop(0, TB, step, tuple(hc_ref[s, 0:1, :] for s in range(ns)), unroll=8)

    for s in range(ns):
        hc_ref[s, 0:1, :] = hs[s]
        hl_ref[s] = hs[s]
        ybuf[slot, s, :, 0:W_A] = (z_refs[3 * s + 1][...] * b_ref[s]).astype(BF16)

        ub = z_refs[3 * s + 2][...]
        pext = jnp.concatenate([pc_ref[s], ub], axis=0)
        for g, w in enumerate(POOL_WINDOWS):
            sl = slice(g * POOL_GC, (g + 1) * POOL_GC)
            acc = pext[:, sl]
            shift = 1
            while shift < w:
                acc = acc + pltpu.roll(acc, shift, axis=0)
                shift *= 2
            cnt = jnp.minimum(pos + 1, w).astype(F32)
            d = acc[POOL_CARRY:] / cnt - ub[:, sl]
            yb = _dot(d.astype(BF16), wp_ref[g]) * ps_ref[:, sl]
            ybuf[slot, s, :, W_A + g * POOL_GC:W_A + (g + 1) * POOL_GC] = yb.astype(BF16)
        pc_ref[s] = ub[TB - POOL_CARRY:]
        pt_ref[s] = ub[TB - POOL_CARRY:]

    for s in range(ns):
        y_copy(s, slot).start()

    @pl.when(step_no == last_step)
    def _():
        for s in range(ns):
            y_copy(s, 1 - slot).wait()
            y_copy(s, slot).wait()


def _prompt_mixer(z, l, cw, cb, wr, br, wi, bi, lam, wp, ps):
    ns = SEQ_PER_STEP
    nt = SEQ // TB
    assert BATCH % ns == 0 and (BATCH // ns) * nt >= 2
    full2 = lambda shape: pl.BlockSpec((None,) + shape, lambda bp, t: (l, 0, 0))
    full3 = lambda shape: pl.BlockSpec((None,) + shape, lambda bp, t: (l, 0, 0, 0))

    def z_spec(s, col):
        return pl.BlockSpec((TB, W_A), lambda bp, t: ((bp * ns + s) * nt + t, col))

    state = lambda rows, w: pl.BlockSpec((ns, rows, w), lambda bp, t: (bp, 0, 0))
    return pl.pallas_call(
        _prompt_mixer_kernel,
        grid=(BATCH // ns, nt),
        in_specs=[z_spec(s, col) for s in range(ns) for col in range(3)] + [
            full2((CONV_W, W_A)), full2((1, W_A)),
            full3((N_LRU_HEADS, LRU_HEAD, LRU_HEAD)), full2((1, W_A)),
            full3((N_LRU_HEADS, LRU_HEAD, LRU_HEAD)), full2((1, W_A)),
            full2((1, W_A)),
            full3((N_POOL_GROUPS, POOL_GC, POOL_GC)), full2((1, W_B)),
        ],
        out_specs=[
            pl.BlockSpec(memory_space=pl.ANY),
            state(1, W_A),
            state(CONV_CARRY, W_A),
            state(POOL_CARRY, W_B),
        ],
        out_shape=[
            jax.ShapeDtypeStruct((N_TOK, D_MODEL), BF16),
            jax.ShapeDtypeStruct((BATCH, 1, W_A), F32),
            jax.ShapeDtypeStruct((BATCH, CONV_CARRY, W_A), F32),
            jax.ShapeDtypeStruct((BATCH, POOL_CARRY, W_B), F32),
        ],
        scratch_shapes=[
            pltpu.VMEM((ns, 8, W_A), F32),
            pltpu.VMEM((ns, CONV_CARRY, W_A), F32),
            pltpu.VMEM((ns, POOL_CARRY, W_B), F32),
            pltpu.VMEM((ns, TB, W_A), F32),
            pltpu.VMEM((ns, TB, W_A), F32),
            pltpu.VMEM((2, ns, TB, D_MODEL), BF16),
            pltpu.SemaphoreType.DMA((2, ns)),
        ],
        compiler_params=_params("arbitrary", "arbitrary"),
        name="prompt_mixer",
    )(*([z] * (3 * ns)), cw, cb, wr, br, wi, bi, lam, wp, ps)


def _sample_mixer_kernel(xa_ref, ga_ref, ub_ref, h0_ref, cs_ref, pst_ref,
                         cw_ref, cb_ref, wr_ref, br_ref, wi_ref, bi_ref, lam_ref, wp_ref, ps_ref,
                         *aliased_and_out_refs):
    y_ref, hn_ref, cn_ref, pn_ref = aliased_and_out_refs[-4:]
    c = pl.program_id(0)
    heads = SAMPLE_CB // LRU_HEAD
    rows = lambda t: slice(t * DEC_BATCH, (t + 1) * DEC_BATCH)

    ext = [cs_ref[k] for k in range(CONV_W - 1)] + [xa_ref[rows(t), :] for t in range(DEC_SEQ)]
    logsig = _log_sigmoid(lam_ref[...])
    h = h0_ref[...]
    ya = []
    for t in range(DEC_SEQ):
        xc = cb_ref[...] + cw_ref[0:1, :] * ext[t]
        for k in range(1, CONV_W):
            xc = xc + cw_ref[k:k + 1, :] * ext[t + k]
        xcb = xc.astype(BF16)
        r_parts, i_parts = [], []
        for hd in range(heads):
            sl = slice(hd * LRU_HEAD, (hd + 1) * LRU_HEAD)
            r_parts.append(_dot(xcb[:, sl], wr_ref[hd]))
            i_parts.append(_dot(xcb[:, sl], wi_ref[hd]))
        r = jax.nn.sigmoid(jnp.concatenate(r_parts, axis=1) + br_ref[...])
        i = jax.nn.sigmoid(jnp.concatenate(i_parts, axis=1) + bi_ref[...])
        a, mult, ix = _lru_coeffs(xc, r, i, logsig)
        if PAST_LEN + t == 0:
            a, mult = jnp.zeros_like(a), jnp.ones_like(mult)
        h = a * h + mult * ix
        ya.append((ga_ref[rows(t), :] * h).astype(BF16))
    hn_ref[...] = h
    for k in range(CONV_W - 1):
        cn_ref[k] = ext[DEC_SEQ + k]

    for k in range(POOL_PAD):
        src = k + DEC_SEQ
        pn_ref[k] = pst_ref[src] if src < POOL_PAD else ub_ref[rows(src - POOL_PAD), :]

    for g, w in enumerate(POOL_WINDOWS):
        @pl.when(c == g)
        def _(g=g, w=w):
            pext = ([pst_ref[k] for k in range(POOL_PAD - w + 1, POOL_PAD)]
                    + [ub_ref[rows(t), :] for t in range(DEC_SEQ)])
            for t in range(DEC_SEQ):
                s = pext[t]
                for j in range(1, w):
                    s = s + pext[t + j]
                cnt = float(min(PAST_LEN + t + 1, w))
                d = s / cnt - pext[t + w - 1]
                yb = _dot(d.astype(BF16), wp_ref[0]) * ps_ref[...]
                y_ref[rows(t), g * SAMPLE_CB:(g + 1) * SAMPLE_CB] = ya[t]
                y_ref[rows(t), W_A + g * POOL_GC:W_A + (g + 1) * POOL_GC] = yb.astype(BF16)


def _sample_mixer(z, y, state_h, state_conv, state_pool, prev_states, l,
                  cw, cb, wr, br, wi, bi, lam, wp, ps):
    cbw = SAMPLE_CB
    na = W_A // cbw
    heads = cbw // LRU_HEAD
    rb = N_PROMPT // N_SAMPLE
    vec = pl.BlockSpec((None, 1, cbw), lambda c: (l, 0, c))
    state3 = lambda steps: pl.BlockSpec((None, steps, DEC_BATCH, cbw), lambda c: (l, 0, 0, c))
    state2 = pl.BlockSpec((None, DEC_BATCH, cbw), lambda c: (l, 0, c))
    aliased = (y,) + (() if prev_states is None else tuple(prev_states))
    n_in = 15
    return pl.pallas_call(
        _sample_mixer_kernel,
        grid=(na,),
        in_specs=[
            pl.BlockSpec((N_SAMPLE, cbw), lambda c: (rb, c)),
            pl.BlockSpec((N_SAMPLE, cbw), lambda c: (rb, na + c)),
            pl.BlockSpec((N_SAMPLE, cbw), lambda c: (rb, 2 * na + c)),
            state2,
            state3(CONV_W - 1),
            state3(POOL_PAD),
            pl.BlockSpec((None, CONV_W, cbw), lambda c: (l, 0, c)), vec,
            pl.BlockSpec((None, heads, LRU_HEAD, LRU_HEAD), lambda c: (l, c, 0, 0)), vec,
            pl.BlockSpec((None, heads, LRU_HEAD, LRU_HEAD), lambda c: (l, c, 0, 0)), vec,
            vec,
            pl.BlockSpec((None, 1, POOL_GC, POOL_GC), lambda c: (l, c, 0, 0)), vec,
        ] + [pl.BlockSpec(memory_space=pl.ANY)] * len(aliased),
        out_specs=[
            pl.BlockSpec((N_SAMPLE, D_MODEL), lambda c: (rb, 0)),
            state2,
            state3(CONV_W - 1),
            state3(POOL_PAD),
        ],
        out_shape=[
            jax.ShapeDtypeStruct((N_TOK, D_MODEL), BF16),
            jax.ShapeDtypeStruct((DEPTH, DEC_BATCH, W_A), F32),
            jax.ShapeDtypeStruct((DEPTH, CONV_W - 1, DEC_BATCH, W_A), F32),
            jax.ShapeDtypeStruct((DEPTH, POOL_PAD, DEC_BATCH, W_B), F32),
        ],
        input_output_aliases={n_in + a: a for a in range(len(aliased))},
        compiler_params=_params("arbitrary"),
        name="sample_mixer",
    )(z, z, z, state_h, state_conv, state_pool, cw, cb, wr, br, wi, bi, lam, wp, ps, *aliased)


def _out_proj_kernel(y_ref, w_ref, x_ref, g_ref, o_ref, xg_ref, inv_ref):
    w = w_ref[...].astype(BF16)
    x_new = lambda rows: x_ref[rows, :] + _dot(y_ref[rows, :], w)
    _store_residual(x_new, g_ref, o_ref, xg_ref, inv_ref)


def _out_proj(y, w_all, l, x, g_next):
    m, k = y.shape
    n = w_all.shape[2]
    out_specs, out_shape = _residual_out(m, n, TM)
    return pl.pallas_call(
        _out_proj_kernel,
        grid=(m // TM, n // TN),
        in_specs=[
            pl.BlockSpec((TM, k), _row_map),
            pl.BlockSpec((None, k, TN), _layer_col_map(l, n // TN)),
            pl.BlockSpec((TM, TN), _tile_map(n // TN)),
            pl.BlockSpec((1, TN), _col_map(n // TN)),
        ],
        out_specs=out_specs,
        out_shape=out_shape,
        compiler_params=_params("parallel", "arbitrary"),
        name="out_proj",
    )(y, w_all, x, g_next)


def _ffn_up_kernel(xg_ref, inv_ref, wg_ref, wu_ref, wd_ref, h_ref, wdb_ref):
    wg = wg_ref[...].astype(BF16)
    wu = wu_ref[...].astype(BF16)
    for r0 in range(0, TM_UP, UP_CHUNK):
        rows = slice(r0, r0 + UP_CHUNK)
        xg = xg_ref[rows, :]
        inv = inv_ref.at[rows, :]
        gate = _scale_rows(_dot(xg, wg), inv)
        up = _scale_rows(_dot(xg, wu), inv)
        h_ref[rows, :] = (jax.nn.silu(gate) * up).astype(BF16)
    wdb_ref[...] = wd_ref[...].astype(BF16)


def _ffn_up(xg, inv, wg_all, wu_all, wd_all, l):
    m, k = xg.shape
    nf = D_FF // TF
    steps = (m // TM_UP) * nf
    wd_rows = D_FF // steps
    assert wd_rows * steps == D_FF and wd_rows % 16 == 0
    return pl.pallas_call(
        _ffn_up_kernel,
        grid=(m // TM_UP, nf),
        in_specs=[
            pl.BlockSpec((TM_UP, k), _row_map),
            pl.BlockSpec((TM_UP, LANES), _row_map, pipeline_mode=pl.Buffered(1)),
            pl.BlockSpec((None, k, TF), _layer_col_map(l, nf)),
            pl.BlockSpec((None, k, TF), _layer_col_map(l, nf)),
            pl.BlockSpec((None, wd_rows, k), lambda i, j: (l, i * nf + j, 0)),
        ],
        out_specs=[
            pl.BlockSpec((TM_UP, TF), _tile_map(nf)),
            pl.BlockSpec((wd_rows, k), lambda i, j: (i * nf + j, 0)),
        ],
        out_shape=[
            jax.ShapeDtypeStruct((m, D_FF), BF16),
            jax.ShapeDtypeStruct((D_FF, k), BF16),
        ],
        compiler_params=pltpu.CompilerParams(dimension_semantics=("arbitrary", "arbitrary"),
                                             vmem_limit_bytes=VMEM_LIMIT_UP_BYTES),
        name="ffn_up",
    )(xg, inv, wg_all, wu_all, wd_all)


def _ffn_down_kernel(h_ref, w_ref, x_ref, g_ref, o_ref, xg_ref, inv_ref):
    x_new = lambda rows: x_ref[rows, :] + _dot(h_ref[rows, :], w_ref[...])
    _store_residual(x_new, g_ref, o_ref, xg_ref, inv_ref)


def _ffn_down(h, wdb, x, g_next):
    m, kf = h.shape
    n = wdb.shape[1]
    out_specs, out_shape = _residual_out(m, n, TM_DOWN)
    return pl.pallas_call(
        _ffn_down_kernel,
        grid=(m // TM_DOWN, n // TN),
        in_specs=[
            pl.BlockSpec((TM_DOWN, kf), _row_map),
            pl.BlockSpec((kf, TN), _col_map(n // TN)),
            pl.BlockSpec((TM_DOWN, TN), _tile_map(n // TN)),
            pl.BlockSpec((1, TN), _col_map(n // TN)),
        ],
        out_specs=out_specs,
        out_shape=out_shape,
        compiler_params=_params("parallel", "arbitrary"),
        name="ffn_down",
    )(h, wdb, x, g_next)


def _ple_kernel(xg_ref, inv_ref, wg_ref, p_ref, wp_ref, x_ref, g_ref, o_ref, xgn_ref, invn_ref):
    wg = wg_ref[...].astype(BF16)
    wp = wp_ref[...].astype(BF16)

    def x_new(rows):
        gate = jax.nn.sigmoid(_scale_rows(_dot(xg_ref[rows, :], wg), inv_ref.at[rows, :]))
        return x_ref[rows, :] + gate * _dot(p_ref[rows, :], wp)

    _store_residual(x_new, g_ref, o_ref, xgn_ref, invn_ref)


def _ple(xg, inv, wg_all, p, wp_all, l, x, g_next):
    m, k = xg.shape
    out_specs, out_shape = _residual_out(m, k, TM)
    return pl.pallas_call(
        _ple_kernel,
        grid=(m // TM, k // TN),
        in_specs=[
            pl.BlockSpec((TM, k), _row_map),
            pl.BlockSpec((TM, LANES), _row_map),
            pl.BlockSpec((None, k, TN), _layer_col_map(l, k // TN)),
            pl.BlockSpec((None, TM, PLE_DIM), lambda i, j: (l, i, 0)),
            pl.BlockSpec((None, PLE_DIM, TN), _layer_col_map(l, k // TN)),
            pl.BlockSpec((TM, TN), _tile_map(k // TN)),
            pl.BlockSpec((1, TN), _col_map(k // TN)),
        ],
        out_specs=out_specs,
        out_shape=out_shape,
        compiler_params=_params("parallel", "arbitrary"),
        name="ple",
    )(xg, inv, wg_all, p, wp_all, x, g_next)


def _time_major(a):
    return jnp.swapaxes(a, 0, 1)


def kernel(x_prompt, x_sample, state_h, state_conv, state_pool, p_prompt, p_sample, g_mix, w_in, conv_w, conv_b, w_rg, b_rg, w_ig, b_ig, lam, w_pool, pool_scale, w_out, g_ffn, w_gate, w_up, w_down, g_pe, w_pe_gate, w_pe_proj, g_final):
    row = lambda v: v.reshape(1, -1)
    layer_rows = lambda v: v.reshape(DEPTH, 1, -1)
    x, xg, inv = _embed(x_prompt.reshape(N_PROMPT, D_MODEL),
                       _time_major(x_sample).reshape(N_SAMPLE, D_MODEL), row(g_mix[0]))
    mixer_params = (conv_w, layer_rows(conv_b),
                    w_rg.astype(BF16), layer_rows(b_rg), w_ig.astype(BF16), layer_rows(b_ig),
                    layer_rows(lam), w_pool.astype(BF16), layer_rows(pool_scale))
    p_all = jnp.concatenate(
        [p_prompt.reshape(DEPTH, N_PROMPT, PLE_DIM),
         jnp.swapaxes(p_sample, 1, 2).reshape(DEPTH, N_SAMPLE, PLE_DIM)], axis=1).astype(BF16)
    conv_tm, pool_tm = jnp.swapaxes(state_conv, 1, 2), jnp.swapaxes(state_pool, 1, 2)
    hp_l, cp_l, pp_l = [], [], []
    sample_states = None
    for l in range(DEPTH):
        g_after_ple = row(g_mix[l + 1]) if l + 1 < DEPTH else row(g_final)

        z = _in_proj(xg, inv, w_in, l)

        y, hp, ct, pt = _prompt_mixer(z, l, *mixer_params)
        y, *sample_states = _sample_mixer(z, y, state_h, conv_tm, pool_tm, sample_states, l,
                                          *mixer_params)

        x, xg, inv = _out_proj(y, w_out, l, x, row(g_ffn[l]))
        h, wdb = _ffn_up(xg, inv, w_gate, w_up, w_down, l)
        x, xg, inv = _ffn_down(h, wdb, x, row(g_pe[l]))
        x, xg, inv = _ple(xg, inv, w_pe_gate, p_all, w_pe_proj, l, x, g_after_ple)

        hp_l.append(hp.reshape(BATCH, W_A))
        cp_l.append(ct[:, CONV_CARRY - (CONV_W - 1):])
        pp_l.append(pt[:, POOL_CARRY - POOL_PAD:])

    hs_all, cs_tm, ps_tm = sample_states
    gf = row(g_final)
    y_prompt = _final_norm(x, inv, gf, N_PROMPT, 0).reshape(BATCH, SEQ, D_MODEL)
    y_s = _final_norm(x, inv, gf, N_SAMPLE, N_PROMPT)
    y_sample = _time_major(y_s.reshape(DEC_SEQ, DEC_BATCH, D_MODEL))
    return (y_prompt, y_sample,
            jnp.stack(hp_l), jnp.stack(cp_l), jnp.stack(pp_l),
            hs_all, jnp.swapaxes(cs_tm, 1, 2), jnp.swapaxes(ps_tm, 1, 2))
```

```python
import jax
import jax.numpy as jnp
from jax import lax
from jax.experimental import pallas as pl
from jax.experimental.pallas import tpu as pltpu

D_MODEL = 4096
BATCH = 4
SEQ = 2048
DEPTH = 2
DEC_BATCH = 128
DEC_SEQ = 4
PAST_LEN = 16384
W_A = D_MODEL // 2
W_B = D_MODEL - W_A
LRU_HEAD = 256
N_LRU_HEADS = W_A // LRU_HEAD
LRU_C = 8.0
CONV_W = 4
POOL_WINDOWS = (2, 4, 8, 16)
N_POOL_GROUPS = len(POOL_WINDOWS)
POOL_GC = W_B // N_POOL_GROUPS
POOL_PAD = max(POOL_WINDOWS) - 1
D_FF = 11008
PLE_DIM = 256
EPS = 1e-6

N_PROMPT = BATCH * SEQ
N_SAMPLE = DEC_BATCH * DEC_SEQ
N_TOK = N_PROMPT + N_SAMPLE
Z_W = 2 * W_A + W_B

F32 = jnp.float32
BF16 = jnp.bfloat16

VMEM_LIMIT_BYTES = 60 * 1024 * 1024
VMEM_LIMIT_UP_BYTES = 63 * 1024 * 1024

TM = 1088
TM_UP = 2176
UP_CHUNK = 272
ROW_CHUNK = 272
TM_DOWN = 544
TM_EMBED = 256
TM_FINAL = 512
LANES = 128
TN = 512
TF = 256
TB = 256
SEQ_PER_STEP = 2
CONV_CARRY = 8
POOL_CARRY = 16
SAMPLE_CB = 512


def _params(*sem):
    return pltpu.CompilerParams(dimension_semantics=sem, vmem_limit_bytes=VMEM_LIMIT_BYTES)


def _dot(a, b):
    return jnp.dot(a, b, preferred_element_type=F32)


def _snake(i, j, nj):
    return jnp.where(i % 2 == 0, j, nj - 1 - j)


def _row_map(i, j):
    return (i, 0)


def _tile_map(nj):
    return lambda i, j: (i, _snake(i, j, nj))


def _col_map(nj):
    return lambda i, j: (0, _snake(i, j, nj))


def _layer_col_map(l, nj):
    return lambda i, j: (l, 0, _snake(i, j, nj))


def _log_sigmoid(x):
    return jnp.minimum(x, 0.0) - jnp.log1p(jnp.exp(-jnp.abs(x)))


def _lru_coeffs(xc, r, i, logsig):
    log_a = r * (LRU_C * logsig)
    a = jnp.exp(log_a)
    u = 1.0 - a * a
    mult = jnp.where(u > 0.0, u * lax.rsqrt(u), 0.0)
    return a, mult, i * xc


def _lane_partial_sumsq(x):
    sq = x * x
    part = sq[:, 0:LANES]
    for c in range(1, x.shape[1] // LANES):
        part = part + sq[:, c * LANES:(c + 1) * LANES]
    return part


def _inv_rms_lanes(partial_sumsq):
    tot = jnp.sum(partial_sumsq, axis=-1, keepdims=True)
    return jnp.broadcast_to(lax.rsqrt(tot * (1.0 / D_MODEL) + EPS), partial_sumsq.shape)


def _scale_rows(acc, inv_ref):
    inv = inv_ref[...]
    return jnp.concatenate(
        [acc[:, c * LANES:(c + 1) * LANES] * inv for c in range(acc.shape[1] // LANES)], axis=1)


def _store_residual(x_new, g_ref, o_ref, xg_ref, inv_ref):
    j, last = pl.program_id(1), pl.num_programs(1) - 1
    parts = []
    for r0 in range(0, o_ref.shape[0], ROW_CHUNK):
        rows = slice(r0, r0 + ROW_CHUNK)
        xc = x_new(rows)
        o_ref[rows, :] = xc
        xg_ref[rows, :] = (xc * g_ref[...]).astype(BF16)
        parts.append(_lane_partial_sumsq(xc))
    part = jnp.concatenate(parts, axis=0)

    @pl.when(j == 0)
    def _():
        inv_ref[...] = part

    @pl.when(jnp.logical_and(j > 0, j < last))
    def _():
        inv_ref[...] += part

    @pl.when(j == last)
    def _():
        inv_ref[...] = _inv_rms_lanes(inv_ref[...] + part)


def _residual_out(m, n, tm):
    specs = [
        pl.BlockSpec((tm, TN), _tile_map(n // TN)),
        pl.BlockSpec((tm, TN), _tile_map(n // TN)),
        pl.BlockSpec((tm, LANES), _row_map),
    ]
    shapes = [
        jax.ShapeDtypeStruct((m, n), F32),
        jax.ShapeDtypeStruct((m, n), BF16),
        jax.ShapeDtypeStruct((m, LANES), F32),
    ]
    return specs, shapes


def _embed_kernel(xp_ref, xs_ref, g_ref, x_ref, xg_ref, inv_ref):
    def emit(x):
        x_ref[...] = x
        xg_ref[...] = (x * g_ref[...]).astype(BF16)
        inv_ref[...] = _inv_rms_lanes(_lane_partial_sumsq(x))

    is_prompt = pl.program_id(0) < N_PROMPT // TM_EMBED

    @pl.when(is_prompt)
    def _():
        emit(xp_ref[...])

    @pl.when(jnp.logical_not(is_prompt))
    def _():
        emit(xs_ref[...])


def _embed(xp, xs, g):
    k = xp.shape[1]
    npb = N_PROMPT // TM_EMBED
    rows = lambda i: (i, 0)
    return pl.pallas_call(
        _embed_kernel,
        grid=(N_TOK // TM_EMBED,),
        in_specs=[
            pl.BlockSpec((TM_EMBED, k), lambda i: (jnp.minimum(i, npb - 1), 0)),
            pl.BlockSpec((TM_EMBED, k), lambda i: (jnp.maximum(i - npb, 0), 0)),
            pl.BlockSpec((1, k), lambda i: (0, 0)),
        ],
        out_specs=[
            pl.BlockSpec((TM_EMBED, k), rows),
            pl.BlockSpec((TM_EMBED, k), rows),
            pl.BlockSpec((TM_EMBED, LANES), rows),
        ],
        out_shape=[
            jax.ShapeDtypeStruct((N_TOK, k), F32),
            jax.ShapeDtypeStruct((N_TOK, k), BF16),
            jax.ShapeDtypeStruct((N_TOK, LANES), F32),
        ],
        compiler_params=_params("arbitrary"),
        name="embed",
    )(xp, xs, g)


def _final_norm_kernel(x_ref, inv_ref, g_ref, o_ref):
    o_ref[...] = _scale_rows(x_ref[...], inv_ref) * g_ref[...]


def _final_norm(x, inv, g, rows, row0):
    k = x.shape[1]
    tm = TM_FINAL
    assert rows % tm == 0 and row0 % tm == 0
    rb0 = row0 // tm
    return pl.pallas_call(
        _final_norm_kernel,
        grid=(rows // tm,),
        in_specs=[pl.BlockSpec((tm, k), lambda i: (rb0 + i, 0)),
                  pl.BlockSpec((tm, LANES), lambda i: (rb0 + i, 0)),
                  pl.BlockSpec((1, k), lambda i: (0, 0))],
        out_specs=pl.BlockSpec((tm, k), lambda i: (i, 0)),
        out_shape=jax.ShapeDtypeStruct((rows, k), F32),
        compiler_params=_params("parallel"),
        name="final_norm",
    )(x, inv, g)


def _in_proj_kernel(xg_ref, inv_ref, w_ref, o_ref):
    jb = _snake(pl.program_id(0), pl.program_id(1), pl.num_programs(1))
    is_gate = jnp.logical_and(jb >= W_A // TN, jb < 2 * W_A // TN)

    def project(act):
        w = w_ref[...].astype(BF16)
        for r0 in range(0, TM, ROW_CHUNK):
            rows = slice(r0, r0 + ROW_CHUNK)
            o_ref[rows, :] = act(_scale_rows(_dot(xg_ref[rows, :], w), inv_ref.at[rows, :]))

    @pl.when(is_gate)
    def _():
        project(jax.nn.gelu)

    @pl.when(jnp.logical_not(is_gate))
    def _():
        project(lambda z: z)


def _in_proj(xg, inv, w_all, l):
    m, k = xg.shape
    n = w_all.shape[2]
    nj = n // TN

    def run(xg_hbm, inv_hbm, w_hbm, z_hbm):
        pltpu.emit_pipeline(
            _in_proj_kernel,
            grid=(m // TM, nj),
            in_specs=[
                pl.BlockSpec((TM, k), _row_map),
                pl.BlockSpec((TM, LANES), _row_map),
                pl.BlockSpec((k, TN), _col_map(nj), pipeline_mode=pl.Buffered(3)),
            ],
            out_specs=[pl.BlockSpec((TM, TN), _tile_map(nj))],
        )(xg_hbm, inv_hbm, w_hbm.at[l], z_hbm)

    return pl.pallas_call(
        run,
        in_specs=[pl.BlockSpec(memory_space=pl.ANY)] * 3,
        out_specs=pl.BlockSpec(memory_space=pl.ANY),
        out_shape=jax.ShapeDtypeStruct((m, n), F32),
        compiler_params=pltpu.CompilerParams(vmem_limit_bytes=VMEM_LIMIT_BYTES),
        name="in_proj",
    )(xg, inv, w_all)


def _prompt_mixer_kernel(*refs):
    ns = SEQ_PER_STEP
    z_refs = refs[:3 * ns]
    cw_ref, cb_ref, wr_ref, br_ref, wi_ref, bi_ref, lam_ref, wp_ref, ps_ref = refs[3 * ns:3 * ns + 9]
    y_hbm, hl_ref, ct_ref, pt_ref = refs[3 * ns + 9:3 * ns + 13]
    hc_ref, cc_ref, pc_ref, a_ref, b_ref, ybuf, sem = refs[3 * ns + 13:]
    bp, t = pl.program_id(0), pl.program_id(1)
    nt = pl.num_programs(1)
    step_no = bp * nt + t
    last_step = pl.num_programs(0) * nt - 1
    slot = step_no % 2

    def y_copy(s, sl):
        row0 = pl.multiple_of(((bp * ns + s) * nt + t) * TB, TB)
        return pltpu.make_async_copy(ybuf.at[sl, s], y_hbm.at[pl.ds(row0, TB)], sem.at[sl, s])

    @pl.when(step_no >= 2)
    def _():
        for s in range(ns):
            y_copy(s, slot).wait()

    @pl.when(t == 0)
    def _():
        hc_ref[...] = jnp.zeros_like(hc_ref)
        cc_ref[...] = jnp.zeros_like(cc_ref)
        pc_ref[...] = jnp.zeros_like(pc_ref)

    pos = lax.broadcasted_iota(jnp.int32, (TB, 1), 0) + t * TB
    logsig = _log_sigmoid(lam_ref[...])

    for s in range(ns):
        xa = z_refs[3 * s][...]
        ext = jnp.concatenate([cc_ref[s], xa], axis=0)
        xc = cb_ref[...] + cw_ref[CONV_W - 1:CONV_W, :] * xa
        for sh in range(1, CONV_W):
            xc = xc + cw_ref[CONV_W - 1 - sh:CONV_W - sh, :] * pltpu.roll(ext, sh, axis=0)[CONV_CARRY:]
        xcb = xc.astype(BF16)
        r_parts, i_parts = [], []
        for hd in range(N_LRU_HEADS):
            sl = slice(hd * LRU_HEAD, (hd + 1) * LRU_HEAD)
            r_parts.append(_dot(xcb[:, sl], wr_ref[hd]))
            i_parts.append(_dot(xcb[:, sl], wi_ref[hd]))
        r = jax.nn.sigmoid(jnp.concatenate(r_parts, axis=1) + br_ref[...])
        i = jax.nn.sigmoid(jnp.concatenate(i_parts, axis=1) + bi_ref[...])
        a, mult, ix = _lru_coeffs(xc, r, i, logsig)
        a_ref[s] = a
        b_ref[s] = mult * ix

        @pl.when(t == 0)
        def _(s=s, ix=ix):
            a_ref[s, 0:1, :] = jnp.zeros((1, W_A), F32)
            b_ref[s, 0:1, :] = ix[0:1, :]

        cc_ref[s] = xa[TB - CONV_CARRY:]
        ct_ref[s] = xa[TB - CONV_CARRY:]

    def step(row, hs):
        out = []
        for s in range(ns):
            h = a_ref[s, pl.ds(row, 1), :] * hs[s] + b_ref[s, pl.ds(row, 1), :]
            b_ref[s, pl.ds(row, 1), :] = h
            out.append(h)
        return tuple(out)

    hs = lax.fori_loop(0, TB, step, tuple(hc_ref[s, 0:1, :] for s in range(ns)), unroll=8)

    for s in range(ns):
        hc_ref[s, 0:1, :] = hs[s]
        hl_ref[s] = hs[s]
        ybuf[slot, s, :, 0:W_A] = (z_refs[3 * s + 1][...] * b_ref[s]).astype(BF16)

        ub = z_refs[3 * s + 2][...]
        pext = jnp.concatenate([pc_ref[s], ub], axis=0)
        for g, w in enumerate(POOL_WINDOWS):
            sl = slice(g * POOL_GC, (g + 1) * POOL_GC)
            acc = pext[:, sl]
            shift = 1
            while shift < w:
                acc = acc + pltpu.roll(acc, shift, axis=0)
                shift *= 2
            cnt = jnp.minimum(pos + 1, w).astype(F32)
            d = acc[POOL_CARRY:] / cnt - ub[:, sl]
            yb = _dot(d.astype(BF16), wp_ref[g]) * ps_ref[:, sl]
            ybuf[slot, s, :, W_A + g * POOL_GC:W_A + (g + 1) * POOL_GC] = yb.astype(BF16)
        pc_ref[s] = ub[TB - POOL_CARRY:]
        pt_ref[s] = ub[TB - POOL_CARRY:]

    for s in range(ns):
        y_copy(s, slot).start()

    @pl.when(step_no == last_step)
    def _():
        for s in range(ns):
            y_copy(s, 1 - slot).wait()
            y_copy(s, slot).wait()


def _prompt_mixer(z, l, cw, cb, wr, br, wi, bi, lam, wp, ps):
    ns = SEQ_PER_STEP
    nt = SEQ // TB
    assert BATCH % ns == 0 and (BATCH // ns) * nt >= 2
    full2 = lambda shape: pl.BlockSpec((None,) + shape, lambda bp, t: (l, 0, 0))
    full3 = lambda shape: pl.BlockSpec((None,) + shape, lambda bp, t: (l, 0, 0, 0))

    def z_spec(s, col):
        return pl.BlockSpec((TB, W_A), lambda bp, t: ((bp * ns + s) * nt + t, col))

    state = lambda rows, w: pl.BlockSpec((ns, rows, w), lambda bp, t: (bp, 0, 0))
    return pl.pallas_call(
        _prompt_mixer_kernel,
        grid=(BATCH // ns, nt),
        in_specs=[z_spec(s, col) for s in range(ns) for col in range(3)] + [
            full2((CONV_W, W_A)), full2((1, W_A)),
            full3((N_LRU_HEADS, LRU_HEAD, LRU_HEAD)), full2((1, W_A)),
            full3((N_LRU_HEADS, LRU_HEAD, LRU_HEAD)), full2((1, W_A)),
            full2((1, W_A)),
            full3((N_POOL_GROUPS, POOL_GC, POOL_GC)), full2((1, W_B)),
        ],
        out_specs=[
            pl.BlockSpec(memory_space=pl.ANY),
            state(1, W_A),
            state(CONV_CARRY, W_A),
            state(POOL_CARRY, W_B),
        ],
        out_shape=[
            jax.ShapeDtypeStruct((N_TOK, D_MODEL), BF16),
            jax.ShapeDtypeStruct((BATCH, 1, W_A), F32),
            jax.ShapeDtypeStruct((BATCH, CONV_CARRY, W_A), F32),
            jax.ShapeDtypeStruct((BATCH, POOL_CARRY, W_B), F32),
        ],
        scratch_shapes=[
            pltpu.VMEM((ns, 8, W_A), F32),
            pltpu.VMEM((ns, CONV_CARRY, W_A), F32),
            pltpu.VMEM((ns, POOL_CARRY, W_B), F32),
            pltpu.VMEM((ns, TB, W_A), F32),
            pltpu.VMEM((ns, TB, W_A), F32),
            pltpu.VMEM((2, ns, TB, D_MODEL), BF16),
            pltpu.SemaphoreType.DMA((2, ns)),
        ],
        compiler_params=_params("arbitrary", "arbitrary"),
        name="prompt_mixer",
    )(*([z] * (3 * ns)), cw, cb, wr, br, wi, bi, lam, wp, ps)


def _sample_mixer_kernel(xa_ref, ga_ref, ub_ref, h0_ref, cs_ref, pst_ref,
                         cw_ref, cb_ref, wr_ref, br_ref, wi_ref, bi_ref, lam_ref, wp_ref, ps_ref,
                         *aliased_and_out_refs):
    y_ref, hn_ref, cn_ref, pn_ref = aliased_and_out_refs[-4:]
    c = pl.program_id(0)
    heads = SAMPLE_CB // LRU_HEAD
    rows = lambda t: slice(t * DEC_BATCH, (t + 1) * DEC_BATCH)

    ext = [cs_ref[k] for k in range(CONV_W - 1)] + [xa_ref[rows(t), :] for t in range(DEC_SEQ)]
    logsig = _log_sigmoid(lam_ref[...])
    h = h0_ref[...]
    ya = []
    for t in range(DEC_SEQ):
        xc = cb_ref[...] + cw_ref[0:1, :] * ext[t]
        for k in range(1, CONV_W):
            xc = xc + cw_ref[k:k + 1, :] * ext[t + k]
        xcb = xc.astype(BF16)
        r_parts, i_parts = [], []
        for hd in range(heads):
            sl = slice(hd * LRU_HEAD, (hd + 1) * LRU_HEAD)
            r_parts.append(_dot(xcb[:, sl], wr_ref[hd]))
            i_parts.append(_dot(xcb[:, sl], wi_ref[hd]))
        r = jax.nn.sigmoid(jnp.concatenate(r_parts, axis=1) + br_ref[...])
        i = jax.nn.sigmoid(jnp.concatenate(i_parts, axis=1) + bi_ref[...])
        a, mult, ix = _lru_coeffs(xc, r, i, logsig)
        if PAST_LEN + t == 0:
            a, mult = jnp.zeros_like(a), jnp.ones_like(mult)
        h = a * h + mult * ix
        ya.append((ga_ref[rows(t), :] * h).astype(BF16))
    hn_ref[...] = h
    for k in range(CONV_W - 1):
        cn_ref[k] = ext[DEC_SEQ + k]

    for k in range(POOL_PAD):
        src = k + DEC_SEQ
        pn_ref[k] = pst_ref[src] if src < POOL_PAD else ub_ref[rows(src - POOL_PAD), :]

    for g, w in enumerate(POOL_WINDOWS):
        @pl.when(c == g)
        def _(g=g, w=w):
            pext = ([pst_ref[k] for k in range(POOL_PAD - w + 1, POOL_PAD)]
                    + [ub_ref[rows(t), :] for t in range(DEC_SEQ)])
            for t in range(DEC_SEQ):
                s = pext[t]
                for j in range(1, w):
                    s = s + pext[t + j]
                cnt = float(min(PAST_LEN + t + 1, w))
                d = s / cnt - pext[t + w - 1]
                yb = _dot(d.astype(BF16), wp_ref[0]) * ps_ref[...]
                y_ref[rows(t), g * SAMPLE_CB:(g + 1) * SAMPLE_CB] = ya[t]
                y_ref[rows(t), W_A + g * POOL_GC:W_A + (g + 1) * POOL_GC] = yb.astype(BF16)


def _sample_mixer(z, y, state_h, state_conv, state_pool, prev_states, l,
                  cw, cb, wr, br, wi, bi, lam, wp, ps):
    cbw = SAMPLE_CB
    na = W_A // cbw
    heads = cbw // LRU_HEAD
    rb = N_PROMPT // N_SAMPLE
    vec = pl.BlockSpec((None, 1, cbw), lambda c: (l, 0, c))
    state3 = lambda steps: pl.BlockSpec((None, steps, DEC_BATCH, cbw), lambda c: (l, 0, 0, c))
    state2 = pl.BlockSpec((None, DEC_BATCH, cbw), lambda c: (l, 0, c))
    aliased = (y,) + (() if prev_states is None else tuple(prev_states))
    n_in = 15
    return pl.pallas_call(
        _sample_mixer_kernel,
        grid=(na,),
        in_specs=[
            pl.BlockSpec((N_SAMPLE, cbw), lambda c: (rb, c)),
            pl.BlockSpec((N_SAMPLE, cbw), lambda c: (rb, na + c)),
            pl.BlockSpec((N_SAMPLE, cbw), lambda c: (rb, 2 * na + c)),
            state2,
            state3(CONV_W - 1),
            state3(POOL_PAD),
            pl.BlockSpec((None, CONV_W, cbw), lambda c: (l, 0, c)), vec,
            pl.BlockSpec((None, heads, LRU_HEAD, LRU_HEAD), lambda c: (l, c, 0, 0)), vec,
            pl.BlockSpec((None, heads, LRU_HEAD, LRU_HEAD), lambda c: (l, c, 0, 0)), vec,
            vec,
            pl.BlockSpec((None, 1, POOL_GC, POOL_GC), lambda c: (l, c, 0, 0)), vec,
        ] + [pl.BlockSpec(memory_space=pl.ANY)] * len(aliased),
        out_specs=[
            pl.BlockSpec((N_SAMPLE, D_MODEL), lambda c: (rb, 0)),
            state2,
            state3(CONV_W - 1),
            state3(POOL_PAD),
        ],
        out_shape=[
            jax.ShapeDtypeStruct((N_TOK, D_MODEL), BF16),
            jax.ShapeDtypeStruct((DEPTH, DEC_BATCH, W_A), F32),
            jax.ShapeDtypeStruct((DEPTH, CONV_W - 1, DEC_BATCH, W_A), F32),
            jax.ShapeDtypeStruct((DEPTH, POOL_PAD, DEC_BATCH, W_B), F32),
        ],
        input_output_aliases={n_in + a: a for a in range(len(aliased))},
        compiler_params=_params("arbitrary"),
        name="sample_mixer",
    )(z, z, z, state_h, state_conv, state_pool, cw, cb, wr, br, wi, bi, lam, wp, ps, *aliased)


def _out_proj_kernel(y_ref, w_ref, x_ref, g_ref, o_ref, xg_ref, inv_ref):
    w = w_ref[...].astype(BF16)
    x_new = lambda rows: x_ref[rows, :] + _dot(y_ref[rows, :], w)
    _store_residual(x_new, g_ref, o_ref, xg_ref, inv_ref)


def _out_proj(y, w_all, l, x, g_next):
    m, k = y.shape
    n = w_all.shape[2]
    out_specs, out_shape = _residual_out(m, n, TM)
    return pl.pallas_call(
        _out_proj_kernel,
        grid=(m // TM, n // TN),
        in_specs=[
            pl.BlockSpec((TM, k), _row_map),
            pl.BlockSpec((None, k, TN), _layer_col_map(l, n // TN)),
            pl.BlockSpec((TM, TN), _tile_map(n // TN)),
            pl.BlockSpec((1, TN), _col_map(n // TN)),
        ],
        out_specs=out_specs,
        out_shape=out_shape,
        compiler_params=_params("parallel", "arbitrary"),
        name="out_proj",
    )(y, w_all, x, g_next)


def _ffn_up_kernel(xg_ref, inv_ref, wg_ref, wu_ref, wd_ref, h_ref, wdb_ref):
    wg = wg_ref[...].astype(BF16)
    wu = wu_ref[...].astype(BF16)
    for r0 in range(0, TM_UP, UP_CHUNK):
        rows = slice(r0, r0 + UP_CHUNK)
        xg = xg_ref[rows, :]
        inv = inv_ref.at[rows, :]
        gate = _scale_rows(_dot(xg, wg), inv)
        up = _scale_rows(_dot(xg, wu), inv)
        h_ref[rows, :] = (jax.nn.silu(gate) * up).astype(BF16)
    wdb_ref[...] = wd_ref[...].astype(BF16)


def _ffn_up(xg, inv, wg_all, wu_all, wd_all, l):
    m, k = xg.shape
    nf = D_FF // TF
    steps = (m // TM_UP) * nf
    wd_rows = D_FF // steps
    assert wd_rows * steps == D_FF and wd_rows % 16 == 0
    return pl.pallas_call(
        _ffn_up_kernel,
        grid=(m // TM_UP, nf),
        in_specs=[
            pl.BlockSpec((TM_UP, k), _row_map),
            pl.BlockSpec((TM_UP, LANES), _row_map, pipeline_mode=pl.Buffered(1)),
            pl.BlockSpec((None, k, TF), _layer_col_map(l, nf)),
            pl.BlockSpec((None, k, TF), _layer_col_map(l, nf)),
            pl.BlockSpec((None, wd_rows, k), lambda i, j: (l, i * nf + j, 0)),
        ],
        out_specs=[
            pl.BlockSpec((TM_UP, TF), _tile_map(nf)),
            pl.BlockSpec((wd_rows, k), lambda i, j: (i * nf + j, 0)),
        ],
        out_shape=[
            jax.ShapeDtypeStruct((m, D_FF), BF16),
            jax.ShapeDtypeStruct((D_FF, k), BF16),
        ],
        compiler_params=pltpu.CompilerParams(dimension_semantics=("arbitrary", "arbitrary"),
                                             vmem_limit_bytes=VMEM_LIMIT_UP_BYTES),
        name="ffn_up",
    )(xg, inv, wg_all, wu_all, wd_all)


def _ffn_down_kernel(h_ref, w_ref, x_ref, g_ref, o_ref, xg_ref, inv_ref):
    x_new = lambda rows: x_ref[rows, :] + _dot(h_ref[rows, :], w_ref[...])
    _store_residual(x_new, g_ref, o_ref, xg_ref, inv_ref)


def _ffn_down(h, wdb, x, g_next):
    m, kf = h.shape
    n = wdb.shape[1]
    out_specs, out_shape = _residual_out(m, n, TM_DOWN)
    return pl.pallas_call(
        _ffn_down_kernel,
        grid=(m // TM_DOWN, n // TN),
        in_specs=[
            pl.BlockSpec((TM_DOWN, kf), _row_map),
            pl.BlockSpec((kf, TN), _col_map(n // TN)),
            pl.BlockSpec((TM_DOWN, TN), _tile_map(n // TN)),
            pl.BlockSpec((1, TN), _col_map(n // TN)),
        ],
        out_specs=out_specs,
        out_shape=out_shape,
        compiler_params=_params("parallel", "arbitrary"),
        name="ffn_down",
    )(h, wdb, x, g_next)


def _ple_kernel(xg_ref, inv_ref, wg_ref, p_ref, wp_ref, x_ref, g_ref, o_ref, xgn_ref, invn_ref):
    wg = wg_ref[...].astype(BF16)
    wp = wp_ref[...].astype(BF16)

    def x_new(rows):
        gate = jax.nn.sigmoid(_scale_rows(_dot(xg_ref[rows, :], wg), inv_ref.at[rows, :]))
        return x_ref[rows, :] + gate * _dot(p_ref[rows, :], wp)

    _store_residual(x_new, g_ref, o_ref, xgn_ref, invn_ref)


def _ple(xg, inv, wg_all, p, wp_all, l, x, g_next):
    m, k = xg.shape
    out_specs, out_shape = _residual_out(m, k, TM)
    return pl.pallas_call(
        _ple_kernel,
        grid=(m // TM, k // TN),
        in_specs=[
            pl.BlockSpec((TM, k), _row_map),
            pl.BlockSpec((TM, LANES), _row_map),
            pl.BlockSpec((None, k, TN), _layer_col_map(l, k // TN)),
            pl.BlockSpec((None, TM, PLE_DIM), lambda i, j: (l, i, 0)),
            pl.BlockSpec((None, PLE_DIM, TN), _layer_col_map(l, k // TN)),
            pl.BlockSpec((TM, TN), _tile_map(k // TN)),
            pl.BlockSpec((1, TN), _col_map(k // TN)),
        ],
        out_specs=out_specs,
        out_shape=out_shape,
        compiler_params=_params("parallel", "arbitrary"),
        name="ple",
    )(xg, inv, wg_all, p, wp_all, x, g_next)


def _time_major(a):
    return jnp.swapaxes(a, 0, 1)


def kernel(x_prompt, x_sample, state_h, state_conv, state_pool, p_prompt, p_sample, g_mix, w_in, conv_w, conv_b, w_rg, b_rg, w_ig, b_ig, lam, w_pool, pool_scale, w_out, g_ffn, w_gate, w_up, w_down, g_pe, w_pe_gate, w_pe_proj, g_final):
    row = lambda v: v.reshape(1, -1)
    layer_rows = lambda v: v.reshape(DEPTH, 1, -1)
    x, xg, inv = _embed(x_prompt.reshape(N_PROMPT, D_MODEL),
                       _time_major(x_sample).reshape(N_SAMPLE, D_MODEL), row(g_mix[0]))
    mixer_params = (conv_w, layer_rows(conv_b),
                    w_rg.astype(BF16), layer_rows(b_rg), w_ig.astype(BF16), layer_rows(b_ig),
                    layer_rows(lam), w_pool.astype(BF16), layer_rows(pool_scale))
    p_all = jnp.concatenate(
        [p_prompt.reshape(DEPTH, N_PROMPT, PLE_DIM),
         jnp.swapaxes(p_sample, 1, 2).reshape(DEPTH, N_SAMPLE, PLE_DIM)], axis=1).astype(BF16)
    conv_tm, pool_tm = jnp.swapaxes(state_conv, 1, 2), jnp.swapaxes(state_pool, 1, 2)
    hp_l, cp_l, pp_l = [], [], []
    sample_states = None
    for l in range(DEPTH):
        g_after_ple = row(g_mix[l + 1]) if l + 1 < DEPTH else row(g_final)

        z = _in_proj(xg, inv, w_in, l)

        y, hp, ct, pt = _prompt_mixer(z, l, *mixer_params)
        y, *sample_states = _sample_mixer(z, y, state_h, conv_tm, pool_tm, sample_states, l,
                                          *mixer_params)

        x, xg, inv = _out_proj(y, w_out, l, x, row(g_ffn[l]))
        h, wdb = _ffn_up(xg, inv, w_gate, w_up, w_down, l)
        x, xg, inv = _ffn_down(h, wdb, x, row(g_pe[l]))
        x, xg, inv = _ple(xg, inv, w_pe_gate, p_all, w_pe_proj, l, x, g_after_ple)

        hp_l.append(hp.reshape(BATCH, W_A))
        cp_l.append(ct[:, CONV_CARRY - (CONV_W - 1):])
        pp_l.append(pt[:, POOL_CARRY - POOL_PAD:])

    hs_all, cs_tm, ps_tm = sample_states
    gf = row(g_final)
    y_prompt = _final_norm(x, inv, gf, N_PROMPT, 0).reshape(BATCH, SEQ, D_MODEL)
    y_s = _final_norm(x, inv, gf, N_SAMPLE, N_PROMPT)
    y_sample = _time_major(y_s.reshape(DEC_SEQ, DEC_BATCH, D_MODEL))
    return (y_prompt, y_sample,
            jnp.stack(hp_l), jnp.stack(cp_l), jnp.stack(pp_l),
            hs_all, jnp.swapaxes(cs_tm, 1, 2), jnp.swapaxes(ps_tm, 1, 2))
```

```python
import jax
import jax.numpy as jnp
from jax import lax
from jax.experimental import pallas as pl
from jax.experimental.pallas import tpu as pltpu

D_MODEL = 4096
BATCH = 4
SEQ = 2048
DEPTH = 2
DEC_BATCH = 128
DEC_SEQ = 4
PAST_LEN = 16384
W_A = D_MODEL // 2
W_B = D_MODEL - W_A
LRU_HEAD = 256
N_LRU_HEADS = W_A // LRU_HEAD
LRU_C = 8.0
CONV_W = 4
POOL_WINDOWS = (2, 4, 8, 16)
N_POOL_GROUPS = len(POOL_WINDOWS)
POOL_GC = W_B // N_POOL_GROUPS
POOL_PAD = max(POOL_WINDOWS) - 1
D_FF = 11008
PLE_DIM = 256
EPS = 1e-6

N_PROMPT = BATCH * SEQ
N_SAMPLE = DEC_BATCH * DEC_SEQ
N_TOK = N_PROMPT + N_SAMPLE
Z_W = 2 * W_A + W_B

F32 = jnp.float32
BF16 = jnp.bfloat16

VMEM_LIMIT_BYTES = 60 * 1024 * 1024
VMEM_LIMIT_UP_BYTES = 63 * 1024 * 1024

TM = 1088
TM_UP = 2176
UP_CHUNK = 272
ROW_CHUNK = 272
TM_DOWN = 544
TM_EMBED = 256
TM_FINAL = 512
LANES = 128
TN = 512
TF = 256
TB = 256
SEQ_PER_STEP = 2
CONV_CARRY = 8
POOL_CARRY = 16
SAMPLE_CB = 512


def _params(*sem):
    return pltpu.CompilerParams(dimension_semantics=sem, vmem_limit_bytes=VMEM_LIMIT_BYTES)


def _dot(a, b):
    return jnp.dot(a, b, preferred_element_type=F32)


def _snake(i, j, nj):
    return jnp.where(i % 2 == 0, j, nj - 1 - j)


def _row_map(i, j):
    return (i, 0)


def _tile_map(nj):
    return lambda i, j: (i, _snake(i, j, nj))


def _col_map(nj):
    return lambda i, j: (0, _snake(i, j, nj))


def _layer_col_map(l, nj):
    return lambda i, j: (l, 0, _snake(i, j, nj))


def _log_sigmoid(x):
    return jnp.minimum(x, 0.0) - jnp.log1p(jnp.exp(-jnp.abs(x)))


def _lru_coeffs(xc, r, i, logsig):
    log_a = r * (LRU_C * logsig)
    a = jnp.exp(log_a)
    u = 1.0 - a * a
    mult = jnp.where(u > 0.0, u * lax.rsqrt(u), 0.0)
    return a, mult, i * xc


def _lane_partial_sumsq(x):
    sq = x * x
    part = sq[:, 0:LANES]
    for c in range(1, x.shape[1] // LANES):
        part = part + sq[:, c * LANES:(c + 1) * LANES]
    return part


def _inv_rms_lanes(partial_sumsq):
    tot = jnp.sum(partial_sumsq, axis=-1, keepdims=True)
    return jnp.broadcast_to(lax.rsqrt(tot * (1.0 / D_MODEL) + EPS), partial_sumsq.shape)


def _scale_rows(acc, inv_ref):
    inv = inv_ref[...]
    return jnp.concatenate(
        [acc[:, c * LANES:(c + 1) * LANES] * inv for c in range(acc.shape[1] // LANES)], axis=1)


def _store_residual(x_new, g_ref, o_ref, xg_ref, inv_ref):
    j, last = pl.program_id(1), pl.num_programs(1) - 1
    parts = []
    for r0 in range(0, o_ref.shape[0], ROW_CHUNK):
        rows = slice(r0, r0 + ROW_CHUNK)
        xc = x_new(rows)
        o_ref[rows, :] = xc
        xg_ref[rows, :] = (xc * g_ref[...]).astype(BF16)
        parts.append(_lane_partial_sumsq(xc))
    part = jnp.concatenate(parts, axis=0)

    @pl.when(j == 0)
    def _():
        inv_ref[...] = part

    @pl.when(jnp.logical_and(j > 0, j < last))
    def _():
        inv_ref[...] += part

    @pl.when(j == last)
    def _():
        inv_ref[...] = _inv_rms_lanes(inv_ref[...] + part)


def _residual_out(m, n, tm):
    specs = [
        pl.BlockSpec((tm, TN), _tile_map(n // TN)),
        pl.BlockSpec((tm, TN), _tile_map(n // TN)),
        pl.BlockSpec((tm, LANES), _row_map),
    ]
    shapes = [
        jax.ShapeDtypeStruct((m, n), F32),
        jax.ShapeDtypeStruct((m, n), BF16),
        jax.ShapeDtypeStruct((m, LANES), F32),
    ]
    return specs, shapes


def _embed_kernel(xp_ref, xs_ref, g_ref, x_ref, xg_ref, inv_ref):
    def emit(x):
        x_ref[...] = x
        xg_ref[...] = (x * g_ref[...]).astype(BF16)
        inv_ref[...] = _inv_rms_lanes(_lane_partial_sumsq(x))

    is_prompt = pl.program_id(0) < N_PROMPT // TM_EMBED

    @pl.when(is_prompt)
    def _():
        emit(xp_ref[...])

    @pl.when(jnp.logical_not(is_prompt))
    def _():
        emit(xs_ref[...])


def _embed(xp, xs, g):
    k = xp.shape[1]
    npb = N_PROMPT // TM_EMBED
    rows = lambda i: (i, 0)
    return pl.pallas_call(
        _embed_kernel,
        grid=(N_TOK // TM_EMBED,),
        in_specs=[
            pl.BlockSpec((TM_EMBED, k), lambda i: (jnp.minimum(i, npb - 1), 0)),
            pl.BlockSpec((TM_EMBED, k), lambda i: (jnp.maximum(i - npb, 0), 0)),
            pl.BlockSpec((1, k), lambda i: (0, 0)),
        ],
        out_specs=[
            pl.BlockSpec((TM_EMBED, k), rows),
            pl.BlockSpec((TM_EMBED, k), rows),
            pl.BlockSpec((TM_EMBED, LANES), rows),
        ],
        out_shape=[
            jax.ShapeDtypeStruct((N_TOK, k), F32),
            jax.ShapeDtypeStruct((N_TOK, k), BF16),
            jax.ShapeDtypeStruct((N_TOK, LANES), F32),
        ],
        compiler_params=_params("arbitrary"),
        name="embed",
    )(xp, xs, g)


def _final_norm_kernel(x_ref, inv_ref, g_ref, o_ref):
    o_ref[...] = _scale_rows(x_ref[...], inv_ref) * g_ref[...]


def _final_norm(x, inv, g, rows, row0):
    k = x.shape[1]
    tm = TM_FINAL
    assert rows % tm == 0 and row0 % tm == 0
    rb0 = row0 // tm
    return pl.pallas_call(
        _final_norm_kernel,
        grid=(rows // tm,),
        in_specs=[pl.BlockSpec((tm, k), lambda i: (rb0 + i, 0)),
                  pl.BlockSpec((tm, LANES), lambda i: (rb0 + i, 0)),
                  pl.BlockSpec((1, k), lambda i: (0, 0))],
        out_specs=pl.BlockSpec((tm, k), lambda i: (i, 0)),
        out_shape=jax.ShapeDtypeStruct((rows, k), F32),
        compiler_params=_params("parallel"),
        name="final_norm",
    )(x, inv, g)


def _in_proj_kernel(xg_ref, inv_ref, w_ref, o_ref):
    jb = _snake(pl.program_id(0), pl.program_id(1), pl.num_programs(1))
    is_gate = jnp.logical_and(jb >= W_A // TN, jb < 2 * W_A // TN)

    def project(act):
        w = w_ref[...].astype(BF16)
        for r0 in range(0, TM, ROW_CHUNK):
            rows = slice(r0, r0 + ROW_CHUNK)
            o_ref[rows, :] = act(_scale_rows(_dot(xg_ref[rows, :], w), inv_ref.at[rows, :]))

    @pl.when(is_gate)
    def _():
        project(jax.nn.gelu)

    @pl.when(jnp.logical_not(is_gate))
    def _():
        project(lambda z: z)


def _in_proj(xg, inv, w_all, l):
    m, k = xg.shape
    n = w_all.shape[2]
    nj = n // TN

    def run(xg_hbm, inv_hbm, w_hbm, z_hbm):
        pltpu.emit_pipeline(
            _in_proj_kernel,
            grid=(m // TM, nj),
            in_specs=[
                pl.BlockSpec((TM, k), _row_map, pipeline_mode=pl.Buffered(2, use_lookahead=True)),
                pl.BlockSpec((TM, LANES), _row_map),
                pl.BlockSpec((k, TN), _col_map(nj), pipeline_mode=pl.Buffered(3)),
            ],
            out_specs=[pl.BlockSpec((TM, TN), _tile_map(nj))],
        )(xg_hbm, inv_hbm, w_hbm.at[l], z_hbm)

    return pl.pallas_call(
        run,
        in_specs=[pl.BlockSpec(memory_space=pl.ANY)] * 3,
        out_specs=pl.BlockSpec(memory_space=pl.ANY),
        out_shape=jax.ShapeDtypeStruct((m, n), F32),
        compiler_params=pltpu.CompilerParams(vmem_limit_bytes=VMEM_LIMIT_BYTES),
        name="in_proj",
    )(xg, inv, w_all)


def _prompt_mixer_kernel(*refs):
    ns = SEQ_PER_STEP
    z_refs = refs[:3 * ns]
    cw_ref, cb_ref, wr_ref, br_ref, wi_ref, bi_ref, lam_ref, wp_ref, ps_ref = refs[3 * ns:3 * ns + 9]
    y_hbm, hl_ref, ct_ref, pt_ref = refs[3 * ns + 9:3 * ns + 13]
    hc_ref, cc_ref, pc_ref, a_ref, b_ref, ybuf, sem = refs[3 * ns + 13:]
    bp, t = pl.program_id(0), pl.program_id(1)
    nt = pl.num_programs(1)
    step_no = bp * nt + t
    last_step = pl.num_programs(0) * nt - 1
    slot = step_no % 2

    def y_copy(s, sl):
        row0 = pl.multiple_of(((bp * ns + s) * nt + t) * TB, TB)
        return pltpu.make_async_copy(ybuf.at[sl, s], y_hbm.at[pl.ds(row0, TB)], sem.at[sl, s])

    @pl.when(step_no >= 2)
    def _():
        for s in range(ns):
            y_copy(s, slot).wait()

    @pl.when(t == 0)
    def _():
        hc_ref[...] = jnp.zeros_like(hc_ref)
        cc_ref[...] = jnp.zeros_like(cc_ref)
        pc_ref[...] = jnp.zeros_like(pc_ref)

    pos = lax.broadcasted_iota(jnp.int32, (TB, 1), 0) + t * TB
    logsig = _log_sigmoid(lam_ref[...])

    for s in range(ns):
        xa = z_refs[3 * s][...]
        ext = jnp.concatenate([cc_ref[s], xa], axis=0)
        xc = cb_ref[...] + cw_ref[CONV_W - 1:CONV_W, :] * xa
        for sh in range(1, CONV_W):
            xc = xc + cw_ref[CONV_W - 1 - sh:CONV_W - sh, :] * pltpu.roll(ext, sh, axis=0)[CONV_CARRY:]
        xcb = xc.astype(BF16)
        r_parts, i_parts = [], []
        for hd in range(N_LRU_HEADS):
            sl = slice(hd * LRU_HEAD, (hd + 1) * LRU_HEAD)
            r_parts.append(_dot(xcb[:, sl], wr_ref[hd]))
            i_parts.append(_dot(xcb[:, sl], wi_ref[hd]))
        r = jax.nn.sigmoid(jnp.concatenate(r_parts, axis=1) + br_ref[...])
        i = jax.nn.sigmoid(jnp.concatenate(i_parts, axis=1) + bi_ref[...])
        a, mult, ix = _lru_coeffs(xc, r, i, logsig)
        a_ref[s] = a
        b_ref[s] = mult * ix

        @pl.when(t == 0)
        def _(s=s, ix=ix):
            a_ref[s, 0:1, :] = jnp.zeros((1, W_A), F32)
            b_ref[s, 0:1, :] = ix[0:1, :]

        cc_ref[s] = xa[TB - CONV_CARRY:]
        ct_ref[s] = xa[TB - CONV_CARRY:]

    def step(row, hs):
        out = []
        for s in range(ns):
            h = a_ref[s, pl.ds(row, 1), :] * hs[s] + b_ref[s, pl.ds(row, 1), :]
            b_ref[s, pl.ds(row, 1), :] = h
            out.append(h)
        return tuple(out)

    hs = lax.fori_loop(0, TB, step, tuple(hc_ref[s, 0:1, :] for s in range(ns)), unroll=8)

    for s in range(ns):
        hc_ref[s, 0:1, :] = hs[s]
        hl_ref[s] = hs[s]
        ybuf[slot, s, :, 0:W_A] = (z_refs[3 * s + 1][...] * b_ref[s]).astype(BF16)

        ub = z_refs[3 * s + 2][...]
        pext = jnp.concatenate([pc_ref[s], ub], axis=0)
        for g, w in enumerate(POOL_WINDOWS):
            sl = slice(g * POOL_GC, (g + 1) * POOL_GC)
            acc = pext[:, sl]
            shift = 1
            while shift < w:
                acc = acc + pltpu.roll(acc, shift, axis=0)
                shift *= 2
            cnt = jnp.minimum(pos + 1, w).astype(F32)
            d = acc[POOL_CARRY:] / cnt - ub[:, sl]
            yb = _dot(d.astype(BF16), wp_ref[g]) * ps_ref[:, sl]
            ybuf[slot, s, :, W_A + g * POOL_GC:W_A + (g + 1) * POOL_GC] = yb.astype(BF16)
        pc_ref[s] = ub[TB - POOL_CARRY:]
        pt_ref[s] = ub[TB - POOL_CARRY:]

    for s in range(ns):
        y_copy(s, slot).start()

    @pl.when(step_no == last_step)
    def _():
        for s in range(ns):
            y_copy(s, 1 - slot).wait()
            y_copy(s, slot).wait()


def _prompt_mixer(z, l, cw, cb, wr, br, wi, bi, lam, wp, ps):
    ns = SEQ_PER_STEP
    nt = SEQ // TB
    assert BATCH % ns == 0 and (BATCH // ns) * nt >= 2
    full2 = lambda shape: pl.BlockSpec((None,) + shape, lambda bp, t: (l, 0, 0))
    full3 = lambda shape: pl.BlockSpec((None,) + shape, lambda bp, t: (l, 0, 0, 0))

    def z_spec(s, col):
        return pl.BlockSpec((TB, W_A), lambda bp, t: ((bp * ns + s) * nt + t, col))

    state = lambda rows, w: pl.BlockSpec((ns, rows, w), lambda bp, t: (bp, 0, 0))
    return pl.pallas_call(
        _prompt_mixer_kernel,
        grid=(BATCH // ns, nt),
        in_specs=[z_spec(s, col) for s in range(ns) for col in range(3)] + [
            full2((CONV_W, W_A)), full2((1, W_A)),
            full3((N_LRU_HEADS, LRU_HEAD, LRU_HEAD)), full2((1, W_A)),
            full3((N_LRU_HEADS, LRU_HEAD, LRU_HEAD)), full2((1, W_A)),
            full2((1, W_A)),
            full3((N_POOL_GROUPS, POOL_GC, POOL_GC)), full2((1, W_B)),
        ],
        out_specs=[
            pl.BlockSpec(memory_space=pl.ANY),
            state(1, W_A),
            state(CONV_CARRY, W_A),
            state(POOL_CARRY, W_B),
        ],
        out_shape=[
            jax.ShapeDtypeStruct((N_TOK, D_MODEL), BF16),
            jax.ShapeDtypeStruct((BATCH, 1, W_A), F32),
            jax.ShapeDtypeStruct((BATCH, CONV_CARRY, W_A), F32),
            jax.ShapeDtypeStruct((BATCH, POOL_CARRY, W_B), F32),
        ],
        scratch_shapes=[
            pltpu.VMEM((ns, 8, W_A), F32),
            pltpu.VMEM((ns, CONV_CARRY, W_A), F32),
            pltpu.VMEM((ns, POOL_CARRY, W_B), F32),
            pltpu.VMEM((ns, TB, W_A), F32),
            pltpu.VMEM((ns, TB, W_A), F32),
            pltpu.VMEM((2, ns, TB, D_MODEL), BF16),
            pltpu.SemaphoreType.DMA((2, ns)),
        ],
        compiler_params=_params("arbitrary", "arbitrary"),
        name="prompt_mixer",
    )(*([z] * (3 * ns)), cw, cb, wr, br, wi, bi, lam, wp, ps)


def _sample_mixer_kernel(xa_ref, ga_ref, ub_ref, h0_ref, cs_ref, pst_ref,
                         cw_ref, cb_ref, wr_ref, br_ref, wi_ref, bi_ref, lam_ref, wp_ref, ps_ref,
                         *aliased_and_out_refs):
    y_ref, hn_ref, cn_ref, pn_ref = aliased_and_out_refs[-4:]
    c = pl.program_id(0)
    heads = SAMPLE_CB // LRU_HEAD
    rows = lambda t: slice(t * DEC_BATCH, (t + 1) * DEC_BATCH)

    ext = [cs_ref[k] for k in range(CONV_W - 1)] + [xa_ref[rows(t), :] for t in range(DEC_SEQ)]
    logsig = _log_sigmoid(lam_ref[...])
    h = h0_ref[...]
    ya = []
    for t in range(DEC_SEQ):
        xc = cb_ref[...] + cw_ref[0:1, :] * ext[t]
        for k in range(1, CONV_W):
            xc = xc + cw_ref[k:k + 1, :] * ext[t + k]
        xcb = xc.astype(BF16)
        r_parts, i_parts = [], []
        for hd in range(heads):
            sl = slice(hd * LRU_HEAD, (hd + 1) * LRU_HEAD)
            r_parts.append(_dot(xcb[:, sl], wr_ref[hd]))
            i_parts.append(_dot(xcb[:, sl], wi_ref[hd]))
        r = jax.nn.sigmoid(jnp.concatenate(r_parts, axis=1) + br_ref[...])
        i = jax.nn.sigmoid(jnp.concatenate(i_parts, axis=1) + bi_ref[...])
        a, mult, ix = _lru_coeffs(xc, r, i, logsig)
        if PAST_LEN + t == 0:
            a, mult = jnp.zeros_like(a), jnp.ones_like(mult)
        h = a * h + mult * ix
        ya.append((ga_ref[rows(t), :] * h).astype(BF16))
    hn_ref[...] = h
    for k in range(CONV_W - 1):
        cn_ref[k] = ext[DEC_SEQ + k]

    for k in range(POOL_PAD):
        src = k + DEC_SEQ
        pn_ref[k] = pst_ref[src] if src < POOL_PAD else ub_ref[rows(src - POOL_PAD), :]

    for g, w in enumerate(POOL_WINDOWS):
        @pl.when(c == g)
        def _(g=g, w=w):
            pext = ([pst_ref[k] for k in range(POOL_PAD - w + 1, POOL_PAD)]
                    + [ub_ref[rows(t), :] for t in range(DEC_SEQ)])
            for t in range(DEC_SEQ):
                s = pext[t]
                for j in range(1, w):
                    s = s + pext[t + j]
                cnt = float(min(PAST_LEN + t + 1, w))
                d = s / cnt - pext[t + w - 1]
                yb = _dot(d.astype(BF16), wp_ref[0]) * ps_ref[...]
                y_ref[rows(t), g * SAMPLE_CB:(g + 1) * SAMPLE_CB] = ya[t]
                y_ref[rows(t), W_A + g * POOL_GC:W_A + (g + 1) * POOL_GC] = yb.astype(BF16)


def _sample_mixer(z, y, state_h, state_conv, state_pool, prev_states, l,
                  cw, cb, wr, br, wi, bi, lam, wp, ps):
    cbw = SAMPLE_CB
    na = W_A // cbw
    heads = cbw // LRU_HEAD
    rb = N_PROMPT // N_SAMPLE
    vec = pl.BlockSpec((None, 1, cbw), lambda c: (l, 0, c))
    state3 = lambda steps: pl.BlockSpec((None, steps, DEC_BATCH, cbw), lambda c: (l, 0, 0, c))
    state2 = pl.BlockSpec((None, DEC_BATCH, cbw), lambda c: (l, 0, c))
    aliased = (y,) + (() if prev_states is None else tuple(prev_states))
    n_in = 15
    return pl.pallas_call(
        _sample_mixer_kernel,
        grid=(na,),
        in_specs=[
            pl.BlockSpec((N_SAMPLE, cbw), lambda c: (rb, c)),
            pl.BlockSpec((N_SAMPLE, cbw), lambda c: (rb, na + c)),
            pl.BlockSpec((N_SAMPLE, cbw), lambda c: (rb, 2 * na + c)),
            state2,
            state3(CONV_W - 1),
            state3(POOL_PAD),
            pl.BlockSpec((None, CONV_W, cbw), lambda c: (l, 0, c)), vec,
            pl.BlockSpec((None, heads, LRU_HEAD, LRU_HEAD), lambda c: (l, c, 0, 0)), vec,
            pl.BlockSpec((None, heads, LRU_HEAD, LRU_HEAD), lambda c: (l, c, 0, 0)), vec,
            vec,
            pl.BlockSpec((None, 1, POOL_GC, POOL_GC), lambda c: (l, c, 0, 0)), vec,
        ] + [pl.BlockSpec(memory_space=pl.ANY)] * len(aliased),
        out_specs=[
            pl.BlockSpec((N_SAMPLE, D_MODEL), lambda c: (rb, 0)),
            state2,
            state3(CONV_W - 1),
            state3(POOL_PAD),
        ],
        out_shape=[
            jax.ShapeDtypeStruct((N_TOK, D_MODEL), BF16),
            jax.ShapeDtypeStruct((DEPTH, DEC_BATCH, W_A), F32),
            jax.ShapeDtypeStruct((DEPTH, CONV_W - 1, DEC_BATCH, W_A), F32),
            jax.ShapeDtypeStruct((DEPTH, POOL_PAD, DEC_BATCH, W_B), F32),
        ],
        input_output_aliases={n_in + a: a for a in range(len(aliased))},
        compiler_params=_params("arbitrary"),
        name="sample_mixer",
    )(z, z, z, state_h, state_conv, state_pool, cw, cb, wr, br, wi, bi, lam, wp, ps, *aliased)


def _out_proj_kernel(y_ref, w_ref, x_ref, g_ref, o_ref, xg_ref, inv_ref):
    w = w_ref[...].astype(BF16)
    x_new = lambda rows: x_ref[rows, :] + _dot(y_ref[rows, :], w)
    _store_residual(x_new, g_ref, o_ref, xg_ref, inv_ref)


def _out_proj(y, w_all, l, x, g_next):
    m, k = y.shape
    n = w_all.shape[2]
    out_specs, out_shape = _residual_out(m, n, TM)
    return pl.pallas_call(
        _out_proj_kernel,
        grid=(m // TM, n // TN),
        in_specs=[
            pl.BlockSpec((TM, k), _row_map),
            pl.BlockSpec((None, k, TN), _layer_col_map(l, n // TN)),
            pl.BlockSpec((TM, TN), _tile_map(n // TN)),
            pl.BlockSpec((1, TN), _col_map(n // TN)),
        ],
        out_specs=out_specs,
        out_shape=out_shape,
        compiler_params=_params("parallel", "arbitrary"),
        name="out_proj",
    )(y, w_all, x, g_next)


def _ffn_up_kernel(xg_ref, inv_ref, wg_ref, wu_ref, wd_ref, h_ref, wdb_ref):
    wg = wg_ref[...].astype(BF16)
    wu = wu_ref[...].astype(BF16)
    for r0 in range(0, TM_UP, UP_CHUNK):
        rows = slice(r0, r0 + UP_CHUNK)
        xg = xg_ref[rows, :]
        inv = inv_ref.at[rows, :]
        gate = _scale_rows(_dot(xg, wg), inv)
        up = _scale_rows(_dot(xg, wu), inv)
        h_ref[rows, :] = (jax.nn.silu(gate) * up).astype(BF16)
    wdb_ref[...] = wd_ref[...].astype(BF16)


def _ffn_up(xg, inv, wg_all, wu_all, wd_all, l):
    m, k = xg.shape
    nf = D_FF // TF
    steps = (m // TM_UP) * nf
    wd_rows = D_FF // steps
    assert wd_rows * steps == D_FF and wd_rows % 16 == 0
    return pl.pallas_call(
        _ffn_up_kernel,
        grid=(m // TM_UP, nf),
        in_specs=[
            pl.BlockSpec((TM_UP, k), _row_map),
            pl.BlockSpec((TM_UP, LANES), _row_map, pipeline_mode=pl.Buffered(1)),
            pl.BlockSpec((None, k, TF), _layer_col_map(l, nf)),
            pl.BlockSpec((None, k, TF), _layer_col_map(l, nf)),
            pl.BlockSpec((None, wd_rows, k), lambda i, j: (l, i * nf + j, 0)),
        ],
        out_specs=[
            pl.BlockSpec((TM_UP, TF), _tile_map(nf)),
            pl.BlockSpec((wd_rows, k), lambda i, j: (i * nf + j, 0)),
        ],
        out_shape=[
            jax.ShapeDtypeStruct((m, D_FF), BF16),
            jax.ShapeDtypeStruct((D_FF, k), BF16),
        ],
        compiler_params=pltpu.CompilerParams(dimension_semantics=("arbitrary", "arbitrary"),
                                             vmem_limit_bytes=VMEM_LIMIT_UP_BYTES),
        name="ffn_up",
    )(xg, inv, wg_all, wu_all, wd_all)


def _ffn_down_kernel(h_ref, w_ref, x_ref, g_ref, o_ref, xg_ref, inv_ref):
    x_new = lambda rows: x_ref[rows, :] + _dot(h_ref[rows, :], w_ref[...])
    _store_residual(x_new, g_ref, o_ref, xg_ref, inv_ref)


def _ffn_down(h, wdb, x, g_next):
    m, kf = h.shape
    n = wdb.shape[1]
    out_specs, out_shape = _residual_out(m, n, TM_DOWN)
    return pl.pallas_call(
        _ffn_down_kernel,
        grid=(m // TM_DOWN, n // TN),
        in_specs=[
            pl.BlockSpec((TM_DOWN, kf), _row_map),
            pl.BlockSpec((kf, TN), _col_map(n // TN)),
            pl.BlockSpec((TM_DOWN, TN), _tile_map(n // TN)),
            pl.BlockSpec((1, TN), _col_map(n // TN)),
        ],
        out_specs=out_specs,
        out_shape=out_shape,
        compiler_params=_params("parallel", "arbitrary"),
        name="ffn_down",
    )(h, wdb, x, g_next)


def _ple_kernel(xg_ref, inv_ref, wg_ref, p_ref, wp_ref, x_ref, g_ref, o_ref, xgn_ref, invn_ref):
    wg = wg_ref[...].astype(BF16)
    wp = wp_ref[...].astype(BF16)

    def x_new(rows):
        gate = jax.nn.sigmoid(_scale_rows(_dot(xg_ref[rows, :], wg), inv_ref.at[rows, :]))
        return x_ref[rows, :] + gate * _dot(p_ref[rows, :], wp)

    _store_residual(x_new, g_ref, o_ref, xgn_ref, invn_ref)


def _ple(xg, inv, wg_all, p, wp_all, l, x, g_next):
    m, k = xg.shape
    out_specs, out_shape = _residual_out(m, k, TM)
    nj = k // TN

    def run(xg_hbm, inv_hbm, wg_hbm, p_hbm, wp_hbm, x_hbm, g_hbm, o_hbm, xgn_hbm, invn_hbm):
        pltpu.emit_pipeline(
            _ple_kernel,
            grid=(m // TM, nj),
            in_specs=[
                pl.BlockSpec((TM, k), _row_map, pipeline_mode=pl.Buffered(2, use_lookahead=True)),
                pl.BlockSpec((TM, LANES), _row_map),
                pl.BlockSpec((k, TN), _col_map(nj)),
                pl.BlockSpec((TM, PLE_DIM), _row_map),
                pl.BlockSpec((PLE_DIM, TN), _col_map(nj)),
                pl.BlockSpec((TM, TN), _tile_map(nj)),
                pl.BlockSpec((1, TN), _col_map(nj)),
            ],
            out_specs=out_specs,
        )(xg_hbm, inv_hbm, wg_hbm.at[l], p_hbm.at[l], wp_hbm.at[l], x_hbm, g_hbm,
          o_hbm, xgn_hbm, invn_hbm)

    return pl.pallas_call(
        run,
        in_specs=[pl.BlockSpec(memory_space=pl.ANY)] * 7,
        out_specs=[pl.BlockSpec(memory_space=pl.ANY)] * 3,
        out_shape=out_shape,
        compiler_params=pltpu.CompilerParams(vmem_limit_bytes=VMEM_LIMIT_BYTES),
        name="ple",
    )(xg, inv, wg_all, p, wp_all, x, g_next)


def _time_major(a):
    return jnp.swapaxes(a, 0, 1)


def kernel(x_prompt, x_sample, state_h, state_conv, state_pool, p_prompt, p_sample, g_mix, w_in, conv_w, conv_b, w_rg, b_rg, w_ig, b_ig, lam, w_pool, pool_scale, w_out, g_ffn, w_gate, w_up, w_down, g_pe, w_pe_gate, w_pe_proj, g_final):
    row = lambda v: v.reshape(1, -1)
    layer_rows = lambda v: v.reshape(DEPTH, 1, -1)
    x, xg, inv = _embed(x_prompt.reshape(N_PROMPT, D_MODEL),
                       _time_major(x_sample).reshape(N_SAMPLE, D_MODEL), row(g_mix[0]))
    mixer_params = (conv_w, layer_rows(conv_b),
                    w_rg.astype(BF16), layer_rows(b_rg), w_ig.astype(BF16), layer_rows(b_ig),
                    layer_rows(lam), w_pool.astype(BF16), layer_rows(pool_scale))
    p_all = jnp.concatenate(
        [p_prompt.reshape(DEPTH, N_PROMPT, PLE_DIM),
         jnp.swapaxes(p_sample, 1, 2).reshape(DEPTH, N_SAMPLE, PLE_DIM)], axis=1).astype(BF16)
    conv_tm, pool_tm = jnp.swapaxes(state_conv, 1, 2), jnp.swapaxes(state_pool, 1, 2)
    hp_l, cp_l, pp_l = [], [], []
    sample_states = None
    for l in range(DEPTH):
        g_after_ple = row(g_mix[l + 1]) if l + 1 < DEPTH else row(g_final)

        z = _in_proj(xg, inv, w_in, l)

        y, hp, ct, pt = _prompt_mixer(z, l, *mixer_params)
        y, *sample_states = _sample_mixer(z, y, state_h, conv_tm, pool_tm, sample_states, l,
                                          *mixer_params)

        x, xg, inv = _out_proj(y, w_out, l, x, row(g_ffn[l]))
        h, wdb = _ffn_up(xg, inv, w_gate, w_up, w_down, l)
        x, xg, inv = _ffn_down(h, wdb, x, row(g_pe[l]))
        x, xg, inv = _ple(xg, inv, w_pe_gate, p_all, w_pe_proj, l, x, g_after_ple)

        hp_l.append(hp.reshape(BATCH, W_A))
        cp_l.append(ct[:, CONV_CARRY - (CONV_W - 1):])
        pp_l.append(pt[:, POOL_CARRY - POOL_PAD:])

    hs_all, cs_tm, ps_tm = sample_states
    gf = row(g_final)
    y_prompt = _final_norm(x, inv, gf, N_PROMPT, 0).reshape(BATCH, SEQ, D_MODEL)
    y_s = _final_norm(x, inv, gf, N_SAMPLE, N_PROMPT)
    y_sample = _time_major(y_s.reshape(DEC_SEQ, DEC_BATCH, D_MODEL))
    return (y_prompt, y_sample,
            jnp.stack(hp_l), jnp.stack(cp_l), jnp.stack(pp_l),
            hs_all, jnp.swapaxes(cs_tm, 1, 2), jnp.swapaxes(ps_tm, 1, 2))
```
